```python
import jax
import jax.numpy as jnp
from jax import lax
import numpy as np

D_MODEL = 1024
BATCH = 8
SEQ = 2048
DEPTH = 2

GRID_W = 64
CTX_LEN = 256
N_EVEN = (DEPTH + 1) // 2
N_ODD = DEPTH // 2
N_MOD = 6
EPS = 1e-6

A_DV = 128
A_HEADS = (D_MODEL // 2) // A_DV
A_DK = A_DV // 2
A_GATE_RANK = 16
A_GATE_TAU = 16.0
GLA_CHUNK = 64

B_WIDTH = D_MODEL // 2
B_CONV_W = 3

AB_SPLITS = (A_HEADS * A_DK, A_HEADS * A_DK, A_HEADS * A_DV, A_GATE_RANK, A_GATE_RANK,
             A_HEADS * A_DV, B_WIDTH, B_WIDTH, B_WIDTH)
AB_IN = sum(AB_SPLITS)
AB_MIX = A_HEADS * A_DV + B_WIDTH

C_HEAD_DIM = 128
C_HEADS = D_MODEL // C_HEAD_DIM
C_KV_HEADS = C_HEADS // 4
C_GROUP = C_HEADS // C_KV_HEADS
C_QKV = (C_HEADS + 2 * C_KV_HEADS) * C_HEAD_DIM
ROPE_THETA = 10000.0
Q_BLOCK = 128

N_EXPERTS = 32
TOP_K = 4
D_EXPERT = D_MODEL
SWIGLU_LIMIT = 7.0
SWIGLU_ALPHA = 1.702
MOE_BLOCK = 128

kernel_name = 'hybrid_gla_conv_gqa_moe_prefix_dit'


def rms_norm(x, g):
    xf = x.astype(jnp.float32)
    y = xf * lax.rsqrt(jnp.mean(xf * xf, axis=-1, keepdims=True) + EPS)
    return (y * g.astype(jnp.float32)).astype(x.dtype)


def ada_params(cond, w, b):
    return jnp.split(jax.nn.silu(cond) @ w + b, N_MOD, axis=-1)


def modulate(h, shift, scale):
    return h * (1.0 + scale) + shift


def to_heads(t, n_heads):
    b, l, _ = t.shape
    return t.reshape(b, l, n_heads, -1).transpose(0, 2, 1, 3)


def gla_log_decay(lr, w, b):
    z = (lr @ w + b).astype(jnp.float32)
    return to_heads(jax.nn.log_sigmoid(z) / A_GATE_TAU, A_HEADS)


def gla_chunked(q, k, v, log_a, state0):
    b_, h_, l_, _ = q.shape
    n_chunks = l_ // GLA_CHUNK

    def chunks(t):
        return t.reshape(b_, h_, n_chunks, GLA_CHUNK, t.shape[-1]).transpose(2, 0, 1, 3, 4)

    causal = jnp.tril(jnp.ones((GLA_CHUNK, GLA_CHUNK), dtype=bool))

    def step(state, inp):
        qi, ki, vi, ai = inp
        cum = jnp.cumsum(ai, axis=2)
        o_inter = jnp.einsum('bhid,bhde->bhie', qi * jnp.exp(cum), state)
        diff = cum[:, :, :, None, :] - cum[:, :, None, :, :]
        decay = jnp.exp(jnp.where(causal[:, :, None], diff, -jnp.inf))
        scores = jnp.einsum('bhid,bhjd,bhijd->bhij', qi, ki, decay)
        o_intra = jnp.einsum('bhij,bhje->bhie', scores, vi)
        last = cum[:, :, -1:, :]
        state = (jnp.exp(last[:, :, 0, :, None]) * state
                 + jnp.einsum('bhjd,bhje->bhde', ki * jnp.exp(last - cum), vi))
        return state, o_inter + o_intra

    state, o = lax.scan(step, state0, (chunks(q), chunks(k), chunks(v), chunks(log_a)))
    o = o.transpose(1, 2, 0, 3, 4).reshape(b_, h_, l_, v.shape[-1])
    return o, state


def short_conv(u, w, b):
    y = lax.conv_general_dilated(u, w[:, None, :].astype(u.dtype), window_strides=(1,),
                                 padding=[(B_CONV_W // 2, B_CONV_W // 2)],
                                 dimension_numbers=('NWC', 'WIO', 'NWC'),
                                 feature_group_count=u.shape[-1])
    return y + b


def gla_conv_mixer(h_c, h_l, w_in, gate_w, gate_b, out_norm, conv_w, conv_b, w_out):
    split_at = [int(i) for i in np.cumsum(AB_SPLITS)[:-1]]

    def prepare(h):
        q, k, v, lr_f, lr_b, r, g_b, g_c, u = jnp.split(h @ w_in, split_at, axis=-1)
        heads = lambda t: to_heads(t, A_HEADS).astype(jnp.float32)
        return (heads(q) * (A_DK ** -0.5), heads(k), heads(v),
                gla_log_decay(lr_f, gate_w[0], gate_b[0]),
                gla_log_decay(lr_b, gate_w[1], gate_b[1]), r, g_b, g_c, u)

    qc, kc, vc, afc, abc, rc, gbc, gcc, uc = prepare(h_c)
    ql, kl, vl, afl, abl, rl, gbl, gcl, ul = prepare(h_l)
    zero = jnp.zeros(qc.shape[:2] + (A_DK, A_DV), jnp.float32)
    flip = lambda t: jnp.flip(t, axis=2)
    o_cf, s_cf = gla_chunked(qc, kc, vc, afc, zero)
    o_lf, _ = gla_chunked(ql, kl, vl, afl, s_cf)
    o_cb, s_cb = gla_chunked(flip(qc), flip(kc), flip(vc), flip(abc), zero)
    o_lb, _ = gla_chunked(flip(ql), flip(kl), flip(vl), flip(abl), s_cb)

    def merge(o, r, g_b, g_c, u):
        b_, h_, l_, dv = o.shape
        o = o * lax.rsqrt(jnp.mean(o * o, axis=-1, keepdims=True) + EPS)
        o = o.transpose(0, 2, 1, 3).reshape(b_, l_, h_ * dv) * out_norm.astype(jnp.float32)
        y_gla = o.astype(r.dtype) * jax.nn.silu(r)
        y_conv = g_b * short_conv(g_c * u, conv_w, conv_b)
        return jnp.concatenate([y_gla, y_conv], axis=-1) @ w_out

    y_c = merge(o_cf + flip(o_cb), rc, gbc, gcc, uc)
    y_l = merge(o_lf + flip(o_lb), rl, gbl, gcl, ul)
    return y_c, y_l


def axial_rope_tables(n_tokens):
    rows = n_tokens // GRID_W
    row = jnp.repeat(jnp.arange(rows), GRID_W).astype(jnp.float32)
    col = jnp.tile(jnp.arange(GRID_W), rows).astype(jnp.float32)
    half = C_HEAD_DIM // 2
    inv_freq = ROPE_THETA ** (-jnp.arange(0, half, 2, dtype=jnp.float32) / half)
    ang = jnp.concatenate([row[:, None] * inv_freq, col[:, None] * inv_freq], axis=-1)
    return jnp.cos(ang), jnp.sin(ang)


def rope_rotate(x, cos, sin):
    x1, x2 = jnp.split(x, 2, axis=-1)
    return jnp.concatenate([x1 * cos - x2 * sin, x2 * cos + x1 * sin], axis=-1)


def apply_axial_rope(x, cos, sin):
    xf = x.astype(jnp.float32)
    half = C_HEAD_DIM // 2
    m = half // 2
    c_ = cos[:, None, :]
    s_ = sin[:, None, :]
    xr = rope_rotate(xf[..., :half], c_[..., :m], s_[..., :m])
    xc = rope_rotate(xf[..., half:], c_[..., m:], s_[..., m:])
    return jnp.concatenate([xr, xc], axis=-1).astype(x.dtype)


def grouped_attention(q, k, v):
    b_, nq = q.shape[:2]
    qg = q.reshape(b_, nq, C_KV_HEADS, C_GROUP, C_HEAD_DIM)
    s = jnp.einsum('bqkgd,bskd->bkgqs', qg, k).astype(jnp.float32) * (C_HEAD_DIM ** -0.5)
    p = jax.nn.softmax(s, axis=-1).astype(v.dtype)
    o = jnp.einsum('bkgqs,bskd->bqkgd', p, v)
    return o.reshape(b_, nq, C_HEADS * C_HEAD_DIM)


def attention_mixer(h_c, h_l, w_qkv, q_norm, k_norm, w_o, need_ctx):
    def project(h):
        b_, l_, _ = h.shape
        q, k, v = jnp.split(h @ w_qkv, [C_HEADS * C_HEAD_DIM, (C_HEADS + C_KV_HEADS) * C_HEAD_DIM], axis=-1)
        q = rms_norm(q.reshape(b_, l_, C_HEADS, C_HEAD_DIM), q_norm)
        k = rms_norm(k.reshape(b_, l_, C_KV_HEADS, C_HEAD_DIM), k_norm)
        return q, k, v.reshape(b_, l_, C_KV_HEADS, C_HEAD_DIM)

    q_c, k_c, v_c = project(h_c)
    q_l, k_l, v_l = project(h_l)
    b_, l_ = h_l.shape[:2]
    cos, sin = axial_rope_tables(l_)
    q_l = apply_axial_rope(q_l, cos, sin)
    k_l = apply_axial_rope(k_l, cos, sin)
    k_all = jnp.concatenate([k_c, k_l], axis=1)
    v_all = jnp.concatenate([v_c, v_l], axis=1)
    n_blocks = l_ // Q_BLOCK
    qb = q_l.reshape(b_, n_blocks, Q_BLOCK, C_HEADS, C_HEAD_DIM).transpose(1, 0, 2, 3, 4)
    o_l = lax.map(lambda qq: grouped_attention(qq, k_all, v_all), qb)
    y_l = o_l.transpose(1, 0, 2, 3).reshape(b_, l_, C_HEADS * C_HEAD_DIM) @ w_o
    y_c = grouped_attention(q_c, k_c, v_c) @ w_o if need_ctx else None
    return y_c, y_l


def moe_ffn(h, router_w, router_b, w1, b1, w2, b2):
    shape = h.shape
    xt = h.reshape(-1, shape[-1])
    n_tok = xt.shape[0]
    logits = (xt @ router_w + router_b).astype(jnp.float32)
    top_v, top_e = lax.top_k(logits, TOP_K)
    gates = jax.nn.softmax(top_v, axis=-1)
    n_assign = n_tok * TOP_K
    flat_e = top_e.reshape(n_assign)
    order = jnp.argsort(flat_e)
    sorted_e = flat_e[order]
    sorted_tok = order // TOP_K
    sorted_g = gates.reshape(n_assign)[order]
    counts = jnp.bincount(flat_e, length=N_EXPERTS)
    padded = (counts + MOE_BLOCK - 1) // MOE_BLOCK * MOE_BLOCK
    pad_end = jnp.cumsum(padded)
    start = jnp.cumsum(counts) - counts
    dest = (pad_end - padded)[sorted_e] + jnp.arange(n_assign) - start[sorted_e]
    n_rows = -(-(n_assign + N_EXPERTS * (MOE_BLOCK - 1)) // MOE_BLOCK) * MOE_BLOCK
    n_blocks = n_rows // MOE_BLOCK
    row_tok = jnp.full((n_rows,), n_tok, jnp.int32).at[dest].set(sorted_tok)
    row_g = jnp.zeros((n_rows,), jnp.float32).at[dest].set(sorted_g)
    block_e = jnp.minimum(jnp.searchsorted(pad_end, jnp.arange(n_blocks) * MOE_BLOCK, side='right'),
                          N_EXPERTS - 1)
    xp = jnp.concatenate([xt, jnp.zeros((1, xt.shape[1]), xt.dtype)], axis=0)

    def expert_block(args):
        toks, e = args
        gate, up = jnp.split(xp[toks] @ w1[e] + b1[e], 2, axis=-1)
        gate = jnp.minimum(gate, SWIGLU_LIMIT)
        up = jnp.clip(up, -SWIGLU_LIMIT, SWIGLU_LIMIT)
        act = (up + 1.0) * gate * jax.nn.sigmoid(SWIGLU_ALPHA * gate)
        return act @ w2[e] + b2[e]

    y_rows = lax.map(expert_block, (row_tok.reshape(n_blocks, MOE_BLOCK), block_e))
    y_rows = y_rows.reshape(n_rows, -1) * row_g[:, None].astype(xt.dtype)
    y = jnp.zeros_like(xp).at[row_tok].add(y_rows)[:n_tok]
    return y.reshape(shape)


def setup_inputs(seed: int = 0) -> dict:
    key = jax.random.key(seed)
    ks = jax.random.split(key, 26)

    def nrm(i, shape, scale):
        return jax.random.normal(ks[i], shape, jnp.float32) * scale

    def gain(i, shape):
        return 1.0 + nrm(i, shape, 0.05)

    return {
        'x': nrm(0, (BATCH, SEQ, D_MODEL), 1.0),
        'c': nrm(1, (BATCH, D_MODEL), 1.0),
        'ctx': nrm(2, (BATCH, CTX_LEN, D_MODEL), 1.0),
        'c_ctx': nrm(3, (D_MODEL,), 1.0),
        'ada_w': nrm(4, (DEPTH, D_MODEL, N_MOD * D_MODEL), 0.5 * D_MODEL ** -0.5),
        'ada_b': nrm(5, (DEPTH, N_MOD * D_MODEL), 0.02),
        'norm_mix': gain(6, (DEPTH, D_MODEL)),
        'norm_ffn': gain(7, (DEPTH, D_MODEL)),
        'ab_w_in': nrm(8, (N_EVEN, D_MODEL, AB_IN), D_MODEL ** -0.5),
        'ab_gate_w': nrm(9, (N_EVEN, 2, A_GATE_RANK, A_HEADS * A_DK), A_GATE_RANK ** -0.5),
        'ab_gate_b': nrm(10, (N_EVEN, 2, A_HEADS * A_DK), 0.1),
        'ab_out_norm': gain(11, (N_EVEN, A_HEADS * A_DV)),
        'ab_conv_w': nrm(12, (N_EVEN, B_CONV_W, B_WIDTH), B_CONV_W ** -0.5),
        'ab_conv_b': nrm(13, (N_EVEN, B_WIDTH), 0.02),
        'ab_w_out': nrm(14, (N_EVEN, AB_MIX, D_MODEL), AB_MIX ** -0.5),
        'attn_w_qkv': nrm(15, (N_ODD, D_MODEL, C_QKV), D_MODEL ** -0.5),
        'attn_q_norm': gain(16, (N_ODD, C_HEAD_DIM)),
        'attn_k_norm': gain(17, (N_ODD, C_HEAD_DIM)),
        'attn_w_o': nrm(18, (N_ODD, C_HEADS * C_HEAD_DIM, D_MODEL), (C_HEADS * C_HEAD_DIM) ** -0.5),
        'router_w': nrm(19, (DEPTH, D_MODEL, N_EXPERTS), D_MODEL ** -0.5),
        'router_b': nrm(20, (DEPTH, N_EXPERTS), 0.01),
        'moe_w1': nrm(21, (DEPTH, N_EXPERTS, D_MODEL, 2 * D_EXPERT), D_MODEL ** -0.5),
        'moe_b1': nrm(22, (DEPTH, N_EXPERTS, 2 * D_EXPERT), 0.02),
        'moe_w2': nrm(23, (DEPTH, N_EXPERTS, D_EXPERT, D_MODEL), D_EXPERT ** -0.5),
        'moe_b2': nrm(24, (DEPTH, N_EXPERTS, D_MODEL), 0.02),
        'final_norm': gain(25, (D_MODEL,)),
    }


def reference(x, c, ctx, c_ctx, ada_w, ada_b, norm_mix, norm_ffn, ab_w_in, ab_gate_w, ab_gate_b,
              ab_out_norm, ab_conv_w, ab_conv_b, ab_w_out, attn_w_qkv, attn_q_norm, attn_k_norm,
              attn_w_o, router_w, router_b, moe_w1, moe_b1, moe_w2, moe_b2, final_norm):
    n_ctx = ctx.shape[1]
    x_c, x_l = ctx, x
    for layer in range(DEPTH):
        last = layer == DEPTH - 1
        i = layer // 2
        mod_l = [m[:, None, :] for m in ada_params(c, ada_w[layer], ada_b[layer])]
        mod_c = ada_params(c_ctx, ada_w[layer], ada_b[layer])
        h_l = modulate(rms_norm(x_l, norm_mix[layer]), mod_l[0], mod_l[1])
        h_c = modulate(rms_norm(x_c, norm_mix[layer]), mod_c[0], mod_c[1])
        if layer % 2 == 0:
            y_c, y_l = gla_conv_mixer(h_c, h_l, ab_w_in[i], ab_gate_w[i], ab_gate_b[i], ab_out_norm[i],
                                      ab_conv_w[i], ab_conv_b[i], ab_w_out[i])
        else:
            y_c, y_l = attention_mixer(h_c, h_l, attn_w_qkv[i], attn_q_norm[i], attn_k_norm[i],
                                       attn_w_o[i], need_ctx=not last)
        x_l = x_l + mod_l[2] * y_l
        if last:
            h_l = modulate(rms_norm(x_l, norm_ffn[layer]), mod_l[3], mod_l[4])
            x_l = x_l + mod_l[5] * moe_ffn(h_l, router_w[layer], router_b[layer], moe_w1[layer],
                                           moe_b1[layer], moe_w2[layer], moe_b2[layer])
        else:
            x_c = x_c + mod_c[2] * y_c
            h = jnp.concatenate([modulate(rms_norm(x_c, norm_ffn[layer]), mod_c[3], mod_c[4]),
                                 modulate(rms_norm(x_l, norm_ffn[layer]), mod_l[3], mod_l[4])], axis=1)
            f = moe_ffn(h, router_w[layer], router_b[layer], moe_w1[layer], moe_b1[layer],
                        moe_w2[layer], moe_b2[layer])
            x_c = x_c + mod_c[5] * f[:, :n_ctx]
            x_l = x_l + mod_l[5] * f[:, n_ctx:]
    return rms_norm(x_l, final_norm)
```

```python
import functools

import jax
import jax.numpy as jnp
from jax import lax
from jax.experimental import pallas as pl
from jax.experimental.pallas import tpu as pltpu

F32 = jnp.float32
BF16 = jnp.bfloat16

D_MODEL = 1024
BATCH = 8
SEQ = 2048
DEPTH = 2
GRID_W = 64
CTX_LEN = 256
L_ALL = CTX_LEN + SEQ
N_MOD = 6
EPS = 1e-6

A_DV = 128
A_HEADS = 4
A_DK = 64
A_GATE_RANK = 16
A_GATE_TAU = 16.0
GLA_CHUNK = 64
B_WIDTH = 512
AB_MAIN = 3072

C_HEAD_DIM = 128
C_HEADS = 8
C_KV_HEADS = 2
C_GROUP = 4
C_QKV = (C_HEADS + 2 * C_KV_HEADS) * C_HEAD_DIM
ROPE_THETA = 10000.0

N_EXPERTS = 32
TOP_K = 4
SWIGLU_LIMIT = 7.0
SWIGLU_ALPHA = 1.702

LANES = 128
SUBLANES = 8
ROW_SLABS = D_MODEL // LANES
TM = 256
MOE_BM = 256
VMEM_LIMIT = 56 * 1024 * 1024

NT_DIMS = (((1,), (1,)), ((), ()))


def _dot(a, b):
    return jnp.dot(a, b, preferred_element_type=F32)


def _dot_nt(a, b):
    return lax.dot_general(a, b, NT_DIMS, preferred_element_type=F32)


def _params(sem, vmem=None):
    return pltpu.CompilerParams(dimension_semantics=sem, vmem_limit_bytes=vmem)


def _norm_mod(x, g, shift, scale):
    ms = jnp.mean(x * x, axis=-1, keepdims=True)
    y = x * lax.rsqrt(ms + EPS) * g
    return y * (1.0 + scale) + shift


def _sigmoid(x):
    return 1.0 / (1.0 + jnp.exp(-x))


def _ada_kernel(cond_ref, w_ref, b_ref, o_ref):
    c = cond_ref[...]
    s = (c * _sigmoid(c)).astype(BF16)
    o_ref[0] = _dot(s, w_ref[0].astype(BF16)) + b_ref[0]


def _ada(cond, ada_w, ada_b):
    tn = 1536
    n = N_MOD * D_MODEL
    return pl.pallas_call(
        _ada_kernel,
        grid=(DEPTH, n // tn),
        in_specs=[
            pl.BlockSpec((16, D_MODEL), lambda l, j: (0, 0)),
            pl.BlockSpec((1, D_MODEL, tn), lambda l, j: (l, 0, j)),
            pl.BlockSpec((1, 1, tn), lambda l, j: (l, 0, j)),
        ],
        out_specs=pl.BlockSpec((1, 16, tn), lambda l, j: (l, 0, j)),
        out_shape=jax.ShapeDtypeStruct((DEPTH, 16, n), F32),
        compiler_params=_params(("arbitrary", "arbitrary"), VMEM_LIMIT),
        name="ada",
    )(cond, ada_w, ada_b.reshape(DEPTH, 1, n))


def _proj_in_kernel(x_ref, mod_ref, g_ref, w_ref, wlr_ref, p_ref, lr_ref):
    mod = mod_ref[0, 0]
    h = _norm_mod(x_ref[0], g_ref[...], mod[0:1], mod[1:2]).astype(BF16)
    p_ref[0] = _dot(h, w_ref[...])
    lr_ref[0] = _dot(h, wlr_ref[...])


def _proj_in(xall, mod, g, w_main, w_lr):
    nj = L_ALL // TM
    return pl.pallas_call(
        _proj_in_kernel,
        grid=(BATCH, nj),
        in_specs=[
            pl.BlockSpec((1, TM, D_MODEL), lambda b, j: (b, j, 0)),
            pl.BlockSpec((1, 1, 8, D_MODEL), lambda b, j: (b, jnp.minimum(j, 1), 0, 0)),
            pl.BlockSpec((1, D_MODEL), lambda b, j: (0, 0)),
            pl.BlockSpec((D_MODEL, AB_MAIN), lambda b, j: (0, 0)),
            pl.BlockSpec((D_MODEL, LANES), lambda b, j: (0, 0)),
        ],
        out_specs=[
            pl.BlockSpec((1, TM, AB_MAIN), lambda b, j: (b, j, 0)),
            pl.BlockSpec((1, TM, LANES), lambda b, j: (b, j, 0)),
        ],
        out_shape=[
            jax.ShapeDtypeStruct((BATCH, L_ALL, AB_MAIN), F32),
            jax.ShapeDtypeStruct((BATCH, L_ALL, LANES), F32),
        ],
        compiler_params=_params(("arbitrary", "arbitrary"), VMEM_LIMIT),
        name="proj_in",
    )(xall, mod, g, w_main, w_lr)


def _log_sigmoid(z):
    return jnp.minimum(z, 0.0) - jnp.log1p(jnp.exp(-jnp.abs(z)))


def _gla_conv_kernel(q_ref, k_ref, v_ref, r_ref, gb_ref, gc_ref, u_ref, lr_ref, gw_ref, gbias_ref,
                     onorm_ref, cw_ref, yg_ref, yc_ref, of_ref, ob_ref, sf_ref, sb_ref, xs_ref):
    ch = GLA_CHUNK
    n_ctx_chunks = CTX_LEN // ch
    n_chunks = L_ALL // ch
    row = lax.broadcasted_iota(jnp.int32, (ch, ch), 0)
    col = lax.broadcasted_iota(jnp.int32, (ch, ch), 1)
    tri = (row >= col, col >= row)
    tri_bf = (tri[0].astype(BF16), tri[1].astype(BF16))
    lane = lax.broadcasted_iota(jnp.int32, (ch, LANES), 1)
    head_mask = (lane < A_DK, lane >= A_DK)

    sf_ref[...] = jnp.zeros_like(sf_ref)
    sb_ref[...] = jnp.zeros_like(sb_ref)

    def chunk(t, direction, s_ref, o_ref):
        rows = pl.ds(pl.multiple_of(t * ch, ch), ch)
        q = q_ref[0, rows, :] * (A_DK ** -0.5)
        k = k_ref[0, rows, :]
        v = v_ref[0, rows, :]
        lr = lr_ref[0, rows, :].astype(BF16)
        z = _dot(lr, gw_ref[direction]) + gbias_ref[direction]
        a = _log_sigmoid(z) * (1.0 / A_GATE_TAU)
        a_hi = a.astype(BF16)
        a_lo = (a - a_hi.astype(F32)).astype(BF16)
        cum = _dot(tri_bf[direction], a_hi) + _dot(tri_bf[direction], a_lo)
        if direction == 0:
            ref, last = cum[ch // 2 - 1:ch // 2], cum[ch - 1:ch]
        else:
            ref, last = cum[ch // 2:ch // 2 + 1], cum[0:1]
        qe = q * jnp.exp(cum)
        qt = q * jnp.exp(cum - ref)
        kt = (k * jnp.exp(ref - cum)).astype(BF16)
        kl = k * jnp.exp(last - cum)
        dec = jnp.exp(last)
        for hh in range(2):
            m = head_mask[hh]
            vh = v[:, hh * A_DV:(hh + 1) * A_DV].astype(BF16)
            sc = _dot_nt(jnp.where(m, qt, 0.0).astype(BF16), kt)
            sc = jnp.where(tri[direction], sc, 0.0)
            o_intra = _dot(sc.astype(BF16), vh)
            st = s_ref[hh]
            o_inter = _dot_nt(jnp.where(m, qe, 0.0).astype(BF16), st.astype(BF16))
            o_ref[rows, hh * A_DV:(hh + 1) * A_DV] = o_inter + o_intra
            klm = jnp.where(m, kl, 0.0).astype(BF16)
            s_ref[hh] = st * dec + _dot(vh.T, klm)

    def body(i, carry):
        chunk(i, 0, sf_ref, of_ref)
        tb = jnp.where(i < n_ctx_chunks, n_ctx_chunks - 1 - i, n_chunks + n_ctx_chunks - 1 - i)
        chunk(tb, 1, sb_ref, ob_ref)
        return carry

    lax.fori_loop(0, n_chunks, body, 0)

    pad = SUBLANES
    xs_ref[0:pad, :] = jnp.zeros((pad, xs_ref.shape[1]), F32)
    xs_ref[pad + L_ALL:, :] = jnp.zeros((pad, xs_ref.shape[1]), F32)
    xs_ref[pad:pad + L_ALL, :] = gc_ref[0] * u_ref[0]

    cw = cw_ref[...]
    onorm = onorm_ref[...]
    trow = lax.broadcasted_iota(jnp.int32, (TM, 1), 0)
    for ti in range(L_ALL // TM):
        s0 = ti * TM
        rows = slice(s0, s0 + TM)
        o = of_ref[rows, :] + ob_ref[rows, :]
        r = r_ref[0, rows, :]
        parts = []
        for hh in range(2):
            oh = o[:, hh * A_DV:(hh + 1) * A_DV]
            ms = jnp.mean(oh * oh, axis=-1, keepdims=True)
            parts.append(oh * lax.rsqrt(ms + EPS) * onorm[:, hh * A_DV:(hh + 1) * A_DV])
        on = jnp.concatenate(parts, axis=1)
        yg_ref[0, rows, :] = (on * (r * _sigmoid(r))).astype(BF16)
        xm1 = xs_ref[pad + s0 - 1:pad + s0 - 1 + TM, :]
        x0 = xs_ref[pad + s0:pad + s0 + TM, :]
        xp1 = xs_ref[pad + s0 + 1:pad + s0 + 1 + TM, :]
        if s0 + TM == CTX_LEN:
            xp1 = jnp.where(trow == TM - 1, 0.0, xp1)
        if s0 == CTX_LEN:
            xm1 = jnp.where(trow == 0, 0.0, xm1)
        conv = cw[0:1] * xm1 + cw[1:2] * x0 + cw[2:3] * xp1 + cw[3:4]
        yc_ref[0, rows, :] = (gb_ref[0, rows, :] * conv).astype(BF16)


def _gla_conv(p, lr, gw_pad, gbias, onorm, cw):
    hw = 2 * A_DV

    def pspec(width, base):
        return pl.BlockSpec((1, L_ALL, width), lambda b, i: (b, 0, base + i))

    return pl.pallas_call(
        _gla_conv_kernel,
        grid=(BATCH, A_HEADS // 2),
        in_specs=[
            pspec(LANES, 0),
            pspec(LANES, 2),
            pspec(hw, 2),
            pspec(hw, 4),
            pspec(hw, 6),
            pspec(hw, 8),
            pspec(hw, 10),
            pl.BlockSpec((1, L_ALL, LANES), lambda b, i: (b, 0, 0)),
            pl.BlockSpec((2, LANES, LANES), lambda b, i: (0, 0, i)),
            pl.BlockSpec((2, 1, LANES), lambda b, i: (0, 0, i)),
            pl.BlockSpec((1, hw), lambda b, i: (0, i)),
            pl.BlockSpec((8, hw), lambda b, i: (0, i)),
        ],
        out_specs=[
            pl.BlockSpec((1, L_ALL, hw), lambda b, i: (b, 0, i)),
            pl.BlockSpec((1, L_ALL, hw), lambda b, i: (b, 0, i)),
        ],
        out_shape=[
            jax.ShapeDtypeStruct((BATCH, L_ALL, A_HEADS * A_DV), BF16),
            jax.ShapeDtypeStruct((BATCH, L_ALL, B_WIDTH), BF16),
        ],
        scratch_shapes=[
            pltpu.VMEM((L_ALL, hw), F32),
            pltpu.VMEM((L_ALL, hw), F32),
            pltpu.VMEM((2, A_DV, LANES), F32),
            pltpu.VMEM((2, A_DV, LANES), F32),
            pltpu.VMEM((L_ALL + 2 * SUBLANES, hw), F32),
        ],
        compiler_params=_params(("arbitrary", "arbitrary"), VMEM_LIMIT),
        name="gla_conv",
    )(p, p, p, p, p, p, p, lr, gw_pad, gbias, onorm, cw)


def _store_row_slabs(ref, val, n_rows):
    for s in range(ROW_SLABS):
        ref[pl.ds(s, n_rows, stride=ROW_SLABS), :] = val[:, s * LANES:(s + 1) * LANES]


def _load_row_slabs(ref, n_rows):
    return jnp.concatenate(
        [ref[pl.ds(s, n_rows, stride=ROW_SLABS), :] for s in range(ROW_SLABS)], axis=1)


def _outproj_kernel(n_in, *refs):
    y_refs = refs[:n_in]
    w_refs = refs[n_in:2 * n_in]
    x_ref, mod_ref, g_ref, rwt_ref, rb_ref, xo_ref, h_ref, te_ref, tg_ref = refs[2 * n_in:]
    acc = _dot(y_refs[0][0], w_refs[0][...])
    for i in range(1, n_in):
        acc = acc + _dot(y_refs[i][0], w_refs[i][...])
    mod = mod_ref[0, 0]
    xn = x_ref[0] + mod[2:3] * acc
    xo_ref[0] = xn
    h = _norm_mod(xn, g_ref[...], mod[3:4], mod[4:5])
    _store_row_slabs(h_ref, h, TM)

    hb = h.astype(BF16)
    hl = (h - hb.astype(F32)).astype(BF16)
    wt = rwt_ref[...]
    wb = wt.astype(BF16)
    wl = (wt - wb.astype(F32)).astype(BF16)
    logits = _dot_nt(wb, hb) + _dot_nt(wb, hl) + _dot_nt(wl, hb) + rb_ref[...]
    eidx = lax.broadcasted_iota(jnp.int32, logits.shape, 0)
    cur = logits
    vals, idxs = [], []
    for _ in range(TOP_K):
        m = jnp.max(cur, axis=0, keepdims=True)
        sel = jnp.min(jnp.where(cur == m, eidx, N_EXPERTS), axis=0, keepdims=True)
        vals.append(m)
        idxs.append(sel)
        cur = jnp.where(eidx == sel, -jnp.inf, cur)
    ex = [jnp.exp(v - vals[0]) for v in vals]
    den = ex[0] + ex[1] + ex[2] + ex[3]
    zi = jnp.zeros_like(idxs[0])
    zf = jnp.zeros_like(den)
    te_ref[...] = jnp.concatenate(idxs + [zi] * (8 - TOP_K), axis=0)
    tg_ref[...] = jnp.concatenate([e / den for e in ex] + [zf] * (8 - TOP_K), axis=0)


def _outproj(ys, ws, xres, x_tile_offset, n_tiles, mod, mod_seg, g, rwt, rb):
    n_in = len(ys)
    n_tok = BATCH * n_tiles * TM
    in_specs = []
    for y in ys:
        in_specs.append(pl.BlockSpec((1, TM, y.shape[2]), lambda b, j: (b, j, 0)))
    for w in ws:
        in_specs.append(pl.BlockSpec(w.shape, lambda b, j: (0, 0)))
    in_specs += [
        pl.BlockSpec((1, TM, D_MODEL), lambda b, j: (b, j + x_tile_offset, 0)),
        pl.BlockSpec((1, 1, 8, D_MODEL), lambda b, j: (b, mod_seg(j), 0, 0)),
        pl.BlockSpec((1, D_MODEL), lambda b, j: (0, 0)),
        pl.BlockSpec((N_EXPERTS, D_MODEL), lambda b, j: (0, 0)),
        pl.BlockSpec((N_EXPERTS, 1), lambda b, j: (0, 0)),
    ]
    return pl.pallas_call(
        functools.partial(_outproj_kernel, n_in),
        grid=(BATCH, n_tiles),
        in_specs=in_specs,
        out_specs=[
            pl.BlockSpec((1, TM, D_MODEL), lambda b, j: (b, j, 0)),
            pl.BlockSpec((TM * ROW_SLABS, LANES), lambda b, j: (b * n_tiles + j, 0)),
            pl.BlockSpec((8, TM), lambda b, j: (0, b * n_tiles + j)),
            pl.BlockSpec((8, TM), lambda b, j: (0, b * n_tiles + j)),
        ],
        out_shape=[
            jax.ShapeDtypeStruct((BATCH, n_tiles * TM, D_MODEL), F32),
            jax.ShapeDtypeStruct((n_tok * ROW_SLABS, LANES), F32),
            jax.ShapeDtypeStruct((8, n_tok), jnp.int32),
            jax.ShapeDtypeStruct((8, n_tok), F32),
        ],
        compiler_params=_params(("arbitrary", "arbitrary"), VMEM_LIMIT),
        name="outproj_router",
    )(*ys, *ws, xres, mod, g, rwt, rb)


def _moe_rows(n_tok):
    n_assign = n_tok * TOP_K
    n_blocks = -(-(n_assign + N_EXPERTS * (MOE_BM - 1)) // MOE_BM)
    return n_assign, n_blocks


def _route(top_e, top_g, n_tok):
    n_assign, n_blocks = _moe_rows(n_tok)
    n_rows = n_blocks * MOE_BM
    flat_e = top_e[:TOP_K].T.reshape(n_assign)
    flat_g = top_g[:TOP_K].T.reshape(n_assign)
    order = jnp.argsort(flat_e).astype(jnp.int32)
    sorted_e = flat_e[order]
    counts = jnp.bincount(flat_e, length=N_EXPERTS).astype(jnp.int32)
    padded = (counts + MOE_BM - 1) // MOE_BM * MOE_BM
    pad_end = jnp.cumsum(padded)
    start = jnp.cumsum(counts) - counts
    dest = (pad_end - padded)[sorted_e] + jnp.arange(n_assign, dtype=jnp.int32) - start[sorted_e]
    row_a = jnp.full((n_rows,), -1, jnp.int32).at[dest].set(order)
    valid = row_a >= 0
    tok = row_a // TOP_K
    row_tok = jnp.where(valid, tok, 0)
    row_dst = jnp.where(valid, (row_a % TOP_K) * n_tok + tok, 0)
    row_g = jnp.where(valid, flat_g[jnp.maximum(row_a, 0)], 0.0)
    block_e = jnp.minimum(
        jnp.searchsorted(pad_end, jnp.arange(n_blocks, dtype=jnp.int32) * MOE_BM, side='right'),
        N_EXPERTS - 1).astype(jnp.int32)
    n_used = (pad_end[-1] // MOE_BM).astype(jnp.int32).reshape(1)
    block_valid = jnp.sum(valid.reshape(n_blocks, MOE_BM), axis=1).astype(jnp.int32)
    return (block_e, n_used, block_valid, row_tok.reshape(n_blocks, 1, MOE_BM),
            row_dst.reshape(n_blocks, 1, MOE_BM), row_g.reshape(n_rows, 1))


def _moe_kernel(be_ref, nu_ref, bv_ref, rt0_ref, rtn_ref, rd_ref, rg_ref, w1_ref, b1_ref, w2_ref, b2_ref,
                h_hbm, y_hbm, xbuf, ybuf, w1b, w2b, gsem, ssem):
    i = pl.program_id(0)
    nu = nu_ref[0]
    slot = i % 2
    slab = ROW_SLABS
    unroll = 8

    def gather_copy(tok, r, sl):
        return pltpu.make_async_copy(
            h_hbm.at[pl.ds(pl.multiple_of(tok * slab, slab), slab)],
            xbuf.at[sl, pl.ds(pl.multiple_of(r * slab, slab), slab)],
            gsem.at[sl])

    def issue_gather(rt_ref, sl):
        def body(o, carry):
            for j in range(unroll):
                r = o * unroll + j
                gather_copy(rt_ref[0, 0, r], r, sl).start()
            return carry
        lax.fori_loop(0, MOE_BM // unroll, body, 0)

    def wait_gather(sl):
        pltpu.make_async_copy(h_hbm.at[pl.ds(0, MOE_BM * slab)], xbuf.at[sl], gsem.at[sl]).wait()

    def scatter_copy(dst, r):
        return pltpu.make_async_copy(
            ybuf.at[pl.ds(pl.multiple_of(r * slab, slab), slab)],
            y_hbm.at[pl.ds(pl.multiple_of(dst * slab, slab), slab)],
            ssem.at[0])

    def issue_scatter(n_valid):
        def group(o, carry):
            for j in range(unroll):
                r = o * unroll + j
                scatter_copy(rd_ref[0, 0, r], r).start()
            return carry

        def single(r, carry):
            scatter_copy(rd_ref[0, 0, r], r).start()
            return carry

        n_groups = lax.shift_right_logical(n_valid, unroll.bit_length() - 1)
        lax.fori_loop(0, n_groups, group, 0)
        lax.fori_loop(n_groups * unroll, n_valid, single, 0)

    def wait_scatter(n_valid):
        n = n_valid * slab
        pltpu.make_async_copy(ybuf.at[pl.ds(0, n)], y_hbm.at[pl.ds(0, n)], ssem.at[0]).wait()

    @pl.when(i == 0)
    def _():
        issue_gather(rt0_ref, 0)

    @pl.when(i + 1 < nu)
    def _():
        issue_gather(rtn_ref, 1 - slot)

    @pl.when(i < nu)
    def _():
        first = jnp.logical_or(i == 0, be_ref[i] != be_ref[jnp.maximum(i - 1, 0)])

        @pl.when(first)
        def _():
            w1b[...] = w1_ref[0].astype(BF16)
            w2b[...] = w2_ref[0].astype(BF16)

        wait_gather(slot)
        x = _load_row_slabs(xbuf.at[slot], MOE_BM).astype(BF16)
        h1 = _dot(x, w1b[...]) + b1_ref[0]
        gate = jnp.minimum(h1[:, :D_MODEL], SWIGLU_LIMIT)
        up = jnp.clip(h1[:, D_MODEL:], -SWIGLU_LIMIT, SWIGLU_LIMIT)
        act = (up + 1.0) * gate * _sigmoid(SWIGLU_ALPHA * gate)
        y = (_dot(act.astype(BF16), w2b[...]) + b2_ref[0]) * rg_ref[...]

        @pl.when(i > 0)
        def _():
            wait_scatter(bv_ref[jnp.maximum(i - 1, 0)])

        _store_row_slabs(ybuf, y, MOE_BM)
        issue_scatter(bv_ref[i])

    @pl.when(i == nu - 1)
    def _():
        wait_scatter(bv_ref[i])


def _moe(h_rows, route, w1, b1, w2, b2, n_tok):
    block_e, n_used, block_valid, row_tok, row_dst, row_g = route
    n_assign, n_blocks = _moe_rows(n_tok)
    d2 = 2 * D_MODEL
    grid_spec = pltpu.PrefetchScalarGridSpec(
        num_scalar_prefetch=3,
        grid=(n_blocks,),
        in_specs=[
            pl.BlockSpec((1, 1, MOE_BM), lambda i, be, nu, bv: (0, 0, 0), memory_space=pltpu.SMEM),
            pl.BlockSpec((1, 1, MOE_BM), lambda i, be, nu, bv: (jnp.minimum(i + 1, n_blocks - 1), 0, 0),
                         memory_space=pltpu.SMEM),
            pl.BlockSpec((1, 1, MOE_BM), lambda i, be, nu, bv: (i, 0, 0), memory_space=pltpu.SMEM),
            pl.BlockSpec((MOE_BM, 1), lambda i, be, nu, bv: (i, 0)),
            pl.BlockSpec((1, D_MODEL, d2), lambda i, be, nu, bv: (be[i], 0, 0)),
            pl.BlockSpec((1, 1, d2), lambda i, be, nu, bv: (be[i], 0, 0)),
            pl.BlockSpec((1, D_MODEL, D_MODEL), lambda i, be, nu, bv: (be[i], 0, 0)),
            pl.BlockSpec((1, 1, D_MODEL), lambda i, be, nu, bv: (be[i], 0, 0)),
            pl.BlockSpec(memory_space=pl.ANY),
        ],
        out_specs=pl.BlockSpec(memory_space=pl.ANY),
        scratch_shapes=[
            pltpu.VMEM((2, MOE_BM * ROW_SLABS, LANES), F32),
            pltpu.VMEM((MOE_BM * ROW_SLABS, LANES), F32),
            pltpu.VMEM((D_MODEL, d2), BF16),
            pltpu.VMEM((D_MODEL, D_MODEL), BF16),
            pltpu.SemaphoreType.DMA((2,)),
            pltpu.SemaphoreType.DMA((1,)),
        ],
    )
    return pl.pallas_call(
        _moe_kernel,
        grid_spec=grid_spec,
        out_shape=jax.ShapeDtypeStruct((n_assign * ROW_SLABS, LANES), F32),
        compiler_params=_params(("arbitrary",), VMEM_LIMIT),
        name="moe",
    )(block_e, n_used, block_valid, row_tok, row_tok, row_dst, row_g, w1, b1.reshape(N_EXPERTS, 1, d2),
      w2, b2.reshape(N_EXPERTS, 1, D_MODEL), h_rows)


def _combine_kernel(final, y0_ref, y1_ref, y2_ref, y3_ref, x_ref, mod_ref, g_ref, o_ref):
    f = _load_row_slabs(y0_ref, TM)
    for yr in (y1_ref, y2_ref, y3_ref):
        f = f + _load_row_slabs(yr, TM)
    mod = mod_ref[0, 0]
    xn = x_ref[0] + mod[5:6] * f
    if final:
        ms = jnp.mean(xn * xn, axis=-1, keepdims=True)
        xn = xn * lax.rsqrt(ms + EPS) * g_ref[...]
    o_ref[0] = xn


def _combine(y_rows, xres, n_tiles, mod, mod_seg, g, final):
    nb = BATCH * n_tiles

    def yspec(k):
        return pl.BlockSpec((TM * ROW_SLABS, LANES), lambda b, j: (k * nb + b * n_tiles + j, 0))

    return pl.pallas_call(
        functools.partial(_combine_kernel, final),
        grid=(BATCH, n_tiles),
        in_specs=[
            yspec(0), yspec(1), yspec(2), yspec(3),
            pl.BlockSpec((1, TM, D_MODEL), lambda b, j: (b, j, 0)),
            pl.BlockSpec((1, 1, 8, D_MODEL), lambda b, j: (b, mod_seg(j), 0, 0)),
            pl.BlockSpec((1, D_MODEL), lambda b, j: (0, 0)),
        ],
        out_specs=pl.BlockSpec((1, TM, D_MODEL), lambda b, j: (b, j, 0)),
        out_shape=jax.ShapeDtypeStruct((BATCH, n_tiles * TM, D_MODEL), F32),
        compiler_params=_params(("arbitrary", "arbitrary"), VMEM_LIMIT),
        name="combine",
    )(y_rows, y_rows, y_rows, y_rows, xres, mod, g)


def _qkv_kernel(x_ref, mod_ref, g_ref, w_ref, qn_ref, kn_ref, cos_ref, sin_ref, q_ref, k_ref, v_ref):
    mod = mod_ref[0, 0]
    h = _norm_mod(x_ref[0], g_ref[...], mod[0:1], mod[1:2]).astype(BF16)
    qkv = _dot(h, w_ref[...])
    cos = cos_ref[...]
    sin = sin_ref[...]
    lane = lax.broadcasted_iota(jnp.int32, (TM, C_HEAD_DIM), 1)
    first_half = (lane % (C_HEAD_DIM // 2)) < (C_HEAD_DIM // 4)
    quarter = C_HEAD_DIM // 4

    def head(xh, gn):
        ms = jnp.mean(xh * xh, axis=-1, keepdims=True)
        y = xh * lax.rsqrt(ms + EPS) * gn
        partner = jnp.where(first_half, pltpu.roll(y, C_HEAD_DIM - quarter, 1), pltpu.roll(y, quarter, 1))
        return y * cos + partner * sin

    qn = qn_ref[...]
    kn = kn_ref[...]
    for hq in range(C_HEADS):
        sl = slice(hq * C_HEAD_DIM, (hq + 1) * C_HEAD_DIM)
        q_ref[0, :, sl] = head(qkv[:, sl], qn).astype(BF16)
    for hk in range(C_KV_HEADS):
        src = slice((C_HEADS + hk) * C_HEAD_DIM, (C_HEADS + hk + 1) * C_HEAD_DIM)
        k_ref[0, :, hk * C_HEAD_DIM:(hk + 1) * C_HEAD_DIM] = head(qkv[:, src], kn).astype(BF16)
    v_ref[0] = qkv[:, (C_HEADS + C_KV_HEADS) * C_HEAD_DIM:].astype(BF16)


def _qkv(xall, mod, g, w, qn, kn, cos, sin):
    nj = L_ALL // TM
    kvw = C_KV_HEADS * C_HEAD_DIM
    return pl.pallas_call(
        _qkv_kernel,
        grid=(BATCH, nj),
        in_specs=[
            pl.BlockSpec((1, TM, D_MODEL), lambda b, j: (b, j, 0)),
            pl.BlockSpec((1, 1, 8, D_MODEL), lambda b, j: (b, jnp.minimum(j, 1), 0, 0)),
            pl.BlockSpec((1, D_MODEL), lambda b, j: (0, 0)),
            pl.BlockSpec((D_MODEL, C_QKV), lambda b, j: (0, 0)),
            pl.BlockSpec((1, C_HEAD_DIM), lambda b, j: (0, 0)),
            pl.BlockSpec((1, C_HEAD_DIM), lambda b, j: (0, 0)),
            pl.BlockSpec((TM, C_HEAD_DIM), lambda b, j: (j, 0)),
            pl.BlockSpec((TM, C_HEAD_DIM), lambda b, j: (j, 0)),
        ],
        out_specs=[
            pl.BlockSpec((1, TM, D_MODEL), lambda b, j: (b, j, 0)),
            pl.BlockSpec((1, TM, kvw), lambda b, j: (b, j, 0)),
            pl.BlockSpec((1, TM, kvw), lambda b, j: (b, j, 0)),
        ],
        out_shape=[
            jax.ShapeDtypeStruct((BATCH, L_ALL, D_MODEL), BF16),
            jax.ShapeDtypeStruct((BATCH, L_ALL, kvw), BF16),
            jax.ShapeDtypeStruct((BATCH, L_ALL, kvw), BF16),
        ],
        compiler_params=_params(("arbitrary", "arbitrary"), VMEM_LIMIT),
        name="qkv_rope",
    )(xall, mod, g, w, qn, kn, cos, sin)


def _rope_tables():
    rows = SEQ // GRID_W
    row = jnp.repeat(jnp.arange(rows), GRID_W).astype(F32)
    col = jnp.tile(jnp.arange(GRID_W), rows).astype(F32)
    half = C_HEAD_DIM // 2
    inv_freq = ROPE_THETA ** (-jnp.arange(0, half, 2, dtype=F32) / half)
    ar = row[:, None] * inv_freq
    ac = col[:, None] * inv_freq
    cos = jnp.concatenate([jnp.cos(ar), jnp.cos(ar), jnp.cos(ac), jnp.cos(ac)], axis=-1)
    sin = jnp.concatenate([-jnp.sin(ar), jnp.sin(ar), -jnp.sin(ac), jnp.sin(ac)], axis=-1)
    cos = jnp.concatenate([jnp.ones((CTX_LEN, C_HEAD_DIM), F32), cos], axis=0)
    sin = jnp.concatenate([jnp.zeros((CTX_LEN, C_HEAD_DIM), F32), sin], axis=0)
    return cos, sin


def _attn_kernel(q_ref, k_ref, v_ref, o_ref):
    scale = C_HEAD_DIM ** -0.5
    for g in range(C_KV_HEADS):
        kg = k_ref[0, :, g * C_HEAD_DIM:(g + 1) * C_HEAD_DIM]
        vg = v_ref[0, :, g * C_HEAD_DIM:(g + 1) * C_HEAD_DIM]
        for hh in range(C_GROUP):
            sl = slice((g * C_GROUP + hh) * C_HEAD_DIM, (g * C_GROUP + hh + 1) * C_HEAD_DIM)
            s = _dot_nt(q_ref[0, :, sl], kg) * scale
            m = jnp.max(s, axis=-1, keepdims=True)
            p = jnp.exp(s - m)
            den = jnp.sum(p, axis=-1, keepdims=True)
            o = _dot(p.astype(BF16), vg) / den
            o_ref[0, :, sl] = o.astype(BF16)


def _attention(q, k, v):
    kvw = C_KV_HEADS * C_HEAD_DIM
    ctx_tiles = CTX_LEN // TM
    return pl.pallas_call(
        _attn_kernel,
        grid=(BATCH, SEQ // TM),
        in_specs=[
            pl.BlockSpec((1, TM, D_MODEL), lambda b, j: (b, j + ctx_tiles, 0)),
            pl.BlockSpec((1, L_ALL, kvw), lambda b, j: (b, 0, 0)),
            pl.BlockSpec((1, L_ALL, kvw), lambda b, j: (b, 0, 0)),
        ],
        out_specs=pl.BlockSpec((1, TM, D_MODEL), lambda b, j: (b, j, 0)),
        out_shape=jax.ShapeDtypeStruct((BATCH, SEQ, D_MODEL), BF16),
        compiler_params=_params(("arbitrary", "arbitrary"), VMEM_LIMIT),
        name="attention",
    )(q, k, v)


def _mod_table(ada_layer):
    m = ada_layer.reshape(16, N_MOD, D_MODEL)
    m_lat = m[:BATCH]
    m_ctx = jnp.broadcast_to(m[BATCH], (BATCH, N_MOD, D_MODEL))
    t = jnp.stack([m_ctx, m_lat], axis=1)
    return jnp.pad(t, ((0, 0), (0, 0), (0, 8 - N_MOD), (0, 0)))


def kernel(x, c, ctx, c_ctx, ada_w, ada_b, norm_mix, norm_ffn, ab_w_in, ab_gate_w, ab_gate_b, ab_out_norm,
           ab_conv_w, ab_conv_b, ab_w_out, attn_w_qkv, attn_q_norm, attn_k_norm, attn_w_o, router_w,
           router_b, moe_w1, moe_b1, moe_w2, moe_b2, final_norm):
    cond = jnp.zeros((16, D_MODEL), F32).at[:BATCH].set(c).at[BATCH].set(c_ctx)
    ada = _ada(cond, ada_w, ada_b)
    mod0 = _mod_table(ada[0])
    mod1 = _mod_table(ada[1])
    seg_all = lambda j: jnp.minimum(j, 1)
    seg_lat = lambda j: 1
    n_tiles_all = L_ALL // TM
    n_tiles_lat = SEQ // TM
    nt0 = BATCH * L_ALL
    nt1 = BATCH * SEQ

    xall = jnp.concatenate([ctx, x], axis=1)
    w_in = ab_w_in[0]
    lr0 = 2 * A_HEADS * A_DK + A_HEADS * A_DV
    lr1 = lr0 + 2 * A_GATE_RANK
    w_main = jnp.concatenate([w_in[:, :lr0], w_in[:, lr1:]], axis=1).astype(BF16)
    w_lr = jnp.pad(w_in[:, lr0:lr1], ((0, 0), (0, LANES - 2 * A_GATE_RANK))).astype(BF16)
    p, lr = _proj_in(xall, mod0, norm_mix[0:1], w_main, w_lr)

    gw = ab_gate_w[0]
    gw_pad = jnp.zeros((2, LANES, A_HEADS * A_DK), F32)
    gw_pad = gw_pad.at[0, :A_GATE_RANK].set(gw[0]).at[1, A_GATE_RANK:2 * A_GATE_RANK].set(gw[1])
    cw = jnp.concatenate([ab_conv_w[0], ab_conv_b[0][None], jnp.zeros((4, B_WIDTH), F32)], axis=0)
    yg, yc = _gla_conv(p, lr, gw_pad.astype(BF16), ab_gate_b[0].reshape(2, 1, A_HEADS * A_DK),
                       ab_out_norm[0:1], cw)

    w_out = ab_w_out[0].astype(BF16)
    ngla = A_HEADS * A_DV
    xmid, h_rows, top_e, top_g = _outproj(
        [yg, yc], [w_out[:ngla], w_out[ngla:]], xall, 0, n_tiles_all, mod0, seg_all,
        norm_ffn[0:1], router_w[0].T, router_b[0].reshape(N_EXPERTS, 1))
    route = _route(top_e, top_g, nt0)
    y_rows = _moe(h_rows, route, moe_w1[0], moe_b1[0], moe_w2[0], moe_b2[0], nt0)
    x1 = _combine(y_rows, xmid, n_tiles_all, mod0, seg_all, final_norm.reshape(1, D_MODEL), False)

    cos, sin = _rope_tables()
    q, k, v = _qkv(x1, mod1, norm_mix[1:2], attn_w_qkv[0].astype(BF16), attn_q_norm[0:1],
                   attn_k_norm[0:1], cos, sin)
    o = _attention(q, k, v)
    xmid, h_rows, top_e, top_g = _outproj(
        [o], [attn_w_o[0].astype(BF16)], x1, CTX_LEN // TM, n_tiles_lat, mod1, seg_lat,
        norm_ffn[1:2], router_w[1].T, router_b[1].reshape(N_EXPERTS, 1))
    route = _route(top_e, top_g, nt1)
    y_rows = _moe(h_rows, route, moe_w1[1], moe_b1[1], moe_w2[1], moe_b2[1], nt1)
    return _combine(y_rows, xmid, n_tiles_lat, mod1, seg_lat, final_norm.reshape(1, D_MODEL), True)
```

```python
import functools

import jax
import jax.numpy as jnp
from jax import lax
from jax.experimental import pallas as pl
from jax.experimental.pallas import tpu as pltpu

F32 = jnp.float32
BF16 = jnp.bfloat16
I32 = jnp.int32

D_MODEL = 1024
BATCH = 8
SEQ = 2048
DEPTH = 2
GRID_W = 64
CTX_LEN = 256
L_ALL = CTX_LEN + SEQ
N_MOD = 6
EPS = 1e-6

A_DV = 128
A_HEADS = 4
A_DK = 64
A_GATE_RANK = 16
A_GATE_TAU = 16.0
GLA_CHUNK = 64
B_WIDTH = 512
AB_MAIN = 3072

C_HEAD_DIM = 128
C_HEADS = 8
C_KV_HEADS = 2
C_GROUP = 4
C_QKV = (C_HEADS + 2 * C_KV_HEADS) * C_HEAD_DIM
ROPE_THETA = 10000.0

N_EXPERTS = 32
TOP_K = 4
SWIGLU_LIMIT = 7.0
SWIGLU_ALPHA = 1.702

LANES = 128
SUBLANES = 8
ROW_SLABS = D_MODEL // LANES
TM = 256
TILE_ROWS = TM * TOP_K
MOE_BM = 256
VMEM_LIMIT = 56 * 1024 * 1024

NT_DIMS = (((1,), (1,)), ((), ()))
TN_DIMS = (((0,), (0,)), ((), ()))


def _dot(a, b):
    return jnp.dot(a, b, preferred_element_type=F32)


def _dot_nt(a, b):
    return lax.dot_general(a, b, NT_DIMS, preferred_element_type=F32)


def _dot_tn(a, b):
    return lax.dot_general(a, b, TN_DIMS, preferred_element_type=F32)


def _params(sem, vmem=None):
    return pltpu.CompilerParams(dimension_semantics=sem, vmem_limit_bytes=vmem)


def _norm_mod(x, g, shift, scale):
    ms = jnp.mean(x * x, axis=-1, keepdims=True)
    y = x * lax.rsqrt(ms + EPS) * g
    return y * (1.0 + scale) + shift


def _sigmoid(x):
    return 1.0 / (1.0 + jnp.exp(-x))


def _ada_kernel(cond_ref, w_ref, b_ref, o_ref):
    c = cond_ref[...]
    s = (c * _sigmoid(c)).astype(BF16)
    o_ref[0] = _dot(s, w_ref[0].astype(BF16)) + b_ref[0]


def _ada(cond, ada_w, ada_b):
    tn = 1536
    n = N_MOD * D_MODEL
    return pl.pallas_call(
        _ada_kernel,
        grid=(DEPTH, n // tn),
        in_specs=[
            pl.BlockSpec((16, D_MODEL), lambda l, j: (0, 0)),
            pl.BlockSpec((1, D_MODEL, tn), lambda l, j: (l, 0, j)),
            pl.BlockSpec((1, 1, tn), lambda l, j: (l, 0, j)),
        ],
        out_specs=pl.BlockSpec((1, 16, tn), lambda l, j: (l, 0, j)),
        out_shape=jax.ShapeDtypeStruct((DEPTH, 16, n), F32),
        compiler_params=_params(("arbitrary", "arbitrary"), VMEM_LIMIT),
        name="ada",
    )(cond, ada_w, ada_b.reshape(DEPTH, 1, n))


def _proj_in_kernel(x_ref, mod_ref, g_ref, w_ref, wlr_ref, p_ref, lr_ref):
    mod = mod_ref[0, 0]
    h = _norm_mod(x_ref[0], g_ref[...], mod[0:1], mod[1:2]).astype(BF16)
    p_ref[0] = _dot(h, w_ref[...])
    lr_ref[0] = _dot(h, wlr_ref[...])


def _proj_in(xall, mod, g, w_main, w_lr):
    nj = L_ALL // TM
    return pl.pallas_call(
        _proj_in_kernel,
        grid=(BATCH, nj),
        in_specs=[
            pl.BlockSpec((1, TM, D_MODEL), lambda b, j: (b, j, 0)),
            pl.BlockSpec((1, 1, 8, D_MODEL), lambda b, j: (b, jnp.minimum(j, 1), 0, 0)),
            pl.BlockSpec((1, D_MODEL), lambda b, j: (0, 0)),
            pl.BlockSpec((D_MODEL, AB_MAIN), lambda b, j: (0, 0)),
            pl.BlockSpec((D_MODEL, LANES), lambda b, j: (0, 0)),
        ],
        out_specs=[
            pl.BlockSpec((1, TM, AB_MAIN), lambda b, j: (b, j, 0)),
            pl.BlockSpec((1, TM, LANES), lambda b, j: (b, j, 0)),
        ],
        out_shape=[
            jax.ShapeDtypeStruct((BATCH, L_ALL, AB_MAIN), F32),
            jax.ShapeDtypeStruct((BATCH, L_ALL, LANES), F32),
        ],
        compiler_params=_params(("arbitrary", "arbitrary"), VMEM_LIMIT),
        name="proj_in",
    )(xall, mod, g, w_main, w_lr)


def _log_sigmoid(z):
    return jnp.minimum(z, 0.0) - jnp.log1p(jnp.exp(-jnp.abs(z)))


def _gla_conv_kernel(q_ref, k_ref, v_ref, r_ref, gb_ref, gc_ref, u_ref, lr_ref, gw_ref, gbias_ref,
                     onorm_ref, cw_ref, yg_ref, yc_ref, of_ref, ob_ref, sf_ref, sb_ref, xs_ref):
    ch = GLA_CHUNK
    n_ctx_chunks = CTX_LEN // ch
    n_chunks = L_ALL // ch
    row = lax.broadcasted_iota(I32, (ch, ch), 0)
    col = lax.broadcasted_iota(I32, (ch, ch), 1)
    tri = (row >= col, col >= row)
    tri_bf = (tri[0].astype(BF16), tri[1].astype(BF16))
    lane = lax.broadcasted_iota(I32, (ch, LANES), 1)
    head_mask = (lane < A_DK, lane >= A_DK)

    sf_ref[...] = jnp.zeros_like(sf_ref)
    sb_ref[...] = jnp.zeros_like(sb_ref)

    def chunk(t, direction, s_ref, o_ref):
        rows = pl.ds(pl.multiple_of(t * ch, ch), ch)
        q = q_ref[0, rows, :] * (A_DK ** -0.5)
        k = k_ref[0, rows, :]
        v = v_ref[0, rows, :]
        lr = lr_ref[0, rows, :].astype(BF16)
        z = _dot(lr, gw_ref[direction]) + gbias_ref[direction]
        a = _log_sigmoid(z) * (1.0 / A_GATE_TAU)
        a_hi = a.astype(BF16)
        a_lo = (a - a_hi.astype(F32)).astype(BF16)
        cum = _dot(tri_bf[direction], a_hi) + _dot(tri_bf[direction], a_lo)
        if direction == 0:
            ref, last = cum[ch // 2 - 1:ch // 2], cum[ch - 1:ch]
        else:
            ref, last = cum[ch // 2:ch // 2 + 1], cum[0:1]
        qe = q * jnp.exp(cum)
        qt = q * jnp.exp(cum - ref)
        kt = (k * jnp.exp(ref - cum)).astype(BF16)
        kl = k * jnp.exp(last - cum)
        dec = jnp.exp(last)
        for hh in range(2):
            m = head_mask[hh]
            vh = v[:, hh * A_DV:(hh + 1) * A_DV].astype(BF16)
            sc = _dot_nt(jnp.where(m, qt, 0.0).astype(BF16), kt)
            sc = jnp.where(tri[direction], sc, 0.0)
            o_intra = _dot(sc.astype(BF16), vh)
            st = s_ref[hh]
            o_inter = _dot_nt(jnp.where(m, qe, 0.0).astype(BF16), st.astype(BF16))
            o_ref[rows, hh * A_DV:(hh + 1) * A_DV] = o_inter + o_intra
            klm = jnp.where(m, kl, 0.0).astype(BF16)
            s_ref[hh] = st * dec + _dot(vh.T, klm)

    def body(i, carry):
        chunk(i, 0, sf_ref, of_ref)
        tb = jnp.where(i < n_ctx_chunks, n_ctx_chunks - 1 - i, n_chunks + n_ctx_chunks - 1 - i)
        chunk(tb, 1, sb_ref, ob_ref)
        return carry

    lax.fori_loop(0, n_chunks, body, 0)

    pad = SUBLANES
    xs_ref[0:pad, :] = jnp.zeros((pad, xs_ref.shape[1]), F32)
    xs_ref[pad + L_ALL:, :] = jnp.zeros((pad, xs_ref.shape[1]), F32)
    xs_ref[pad:pad + L_ALL, :] = gc_ref[0] * u_ref[0]

    cw = cw_ref[...]
    onorm = onorm_ref[...]
    trow = lax.broadcasted_iota(I32, (TM, 1), 0)
    for ti in range(L_ALL // TM):
        s0 = ti * TM
        rows = slice(s0, s0 + TM)
        o = of_ref[rows, :] + ob_ref[rows, :]
        r = r_ref[0, rows, :]
        parts = []
        for hh in range(2):
            oh = o[:, hh * A_DV:(hh + 1) * A_DV]
            ms = jnp.mean(oh * oh, axis=-1, keepdims=True)
            parts.append(oh * lax.rsqrt(ms + EPS) * onorm[:, hh * A_DV:(hh + 1) * A_DV])
        on = jnp.concatenate(parts, axis=1)
        yg_ref[0, rows, :] = (on * (r * _sigmoid(r))).astype(BF16)
        xm1 = xs_ref[pad + s0 - 1:pad + s0 - 1 + TM, :]
        x0 = xs_ref[pad + s0:pad + s0 + TM, :]
        xp1 = xs_ref[pad + s0 + 1:pad + s0 + 1 + TM, :]
        if s0 + TM == CTX_LEN:
            xp1 = jnp.where(trow == TM - 1, 0.0, xp1)
        if s0 == CTX_LEN:
            xm1 = jnp.where(trow == 0, 0.0, xm1)
        conv = cw[0:1] * xm1 + cw[1:2] * x0 + cw[2:3] * xp1 + cw[3:4]
        yc_ref[0, rows, :] = (gb_ref[0, rows, :] * conv).astype(BF16)


def _gla_conv(p, lr, gw_pad, gbias, onorm, cw):
    hw = 2 * A_DV

    def pspec(width, base):
        return pl.BlockSpec((1, L_ALL, width), lambda b, i: (b, 0, base + i))

    return pl.pallas_call(
        _gla_conv_kernel,
        grid=(BATCH, A_HEADS // 2),
        in_specs=[
            pspec(LANES, 0),
            pspec(LANES, 2),
            pspec(hw, 2),
            pspec(hw, 4),
            pspec(hw, 6),
            pspec(hw, 8),
            pspec(hw, 10),
            pl.BlockSpec((1, L_ALL, LANES), lambda b, i: (b, 0, 0)),
            pl.BlockSpec((2, LANES, LANES), lambda b, i: (0, 0, i)),
            pl.BlockSpec((2, 1, LANES), lambda b, i: (0, 0, i)),
            pl.BlockSpec((1, hw), lambda b, i: (0, i)),
            pl.BlockSpec((8, hw), lambda b, i: (0, i)),
        ],
        out_specs=[
            pl.BlockSpec((1, L_ALL, hw), lambda b, i: (b, 0, i)),
            pl.BlockSpec((1, L_ALL, hw), lambda b, i: (b, 0, i)),
        ],
        out_shape=[
            jax.ShapeDtypeStruct((BATCH, L_ALL, A_HEADS * A_DV), BF16),
            jax.ShapeDtypeStruct((BATCH, L_ALL, B_WIDTH), BF16),
        ],
        scratch_shapes=[
            pltpu.VMEM((L_ALL, hw), F32),
            pltpu.VMEM((L_ALL, hw), F32),
            pltpu.VMEM((2, A_DV, LANES), F32),
            pltpu.VMEM((2, A_DV, LANES), F32),
            pltpu.VMEM((L_ALL + 2 * SUBLANES, hw), F32),
        ],
        compiler_params=_params(("arbitrary", "arbitrary"), VMEM_LIMIT),
        name="gla_conv",
    )(p, p, p, p, p, p, p, lr, gw_pad, gbias, onorm, cw)


def _outproj_kernel(n_in, *refs):
    y_refs = refs[:n_in]
    w_refs = refs[n_in:2 * n_in]
    x_ref, mod_ref, g_ref, rwt_ref, rb_ref, xo_ref, h_ref, te_ref, tg_ref, cnt_ref = refs[2 * n_in:]
    acc = _dot(y_refs[0][0], w_refs[0][...])
    for i in range(1, n_in):
        acc = acc + _dot(y_refs[i][0], w_refs[i][...])
    mod = mod_ref[0, 0]
    xn = x_ref[0] + mod[2:3] * acc
    xo_ref[0] = xn
    h = _norm_mod(xn, g_ref[...], mod[3:4], mod[4:5])
    h_ref[...] = h

    hb = h.astype(BF16)
    hl = (h - hb.astype(F32)).astype(BF16)
    wt = rwt_ref[...]
    wb = wt.astype(BF16)
    wl = (wt - wb.astype(F32)).astype(BF16)
    logits = _dot_nt(wb, hb) + _dot_nt(wb, hl) + _dot_nt(wl, hb) + rb_ref[...]
    eidx = lax.broadcasted_iota(I32, logits.shape, 0)
    cur = logits
    vals, idxs = [], []
    for _ in range(TOP_K):
        m = jnp.max(cur, axis=0, keepdims=True)
        sel = jnp.min(jnp.where(cur == m, eidx, N_EXPERTS), axis=0, keepdims=True)
        vals.append(m)
        idxs.append(sel)
        cur = jnp.where(eidx == sel, -jnp.inf, cur)
    ex = [jnp.exp(v - vals[0]) for v in vals]
    den = ex[0] + ex[1] + ex[2] + ex[3]
    zi = jnp.zeros_like(idxs[0])
    zf = jnp.zeros_like(den)
    te_ref[...] = jnp.concatenate(idxs + [zi] * (8 - TOP_K), axis=0)
    tg_ref[...] = jnp.concatenate([e / den for e in ex] + [zf] * (8 - TOP_K), axis=0)
    onehot = jnp.where(eidx == idxs[0], 1, 0)
    for k in range(1, TOP_K):
        onehot = onehot + jnp.where(eidx == idxs[k], 1, 0)
    cnt_ref[0] = jnp.broadcast_to(jnp.sum(onehot, axis=1, keepdims=True), (N_EXPERTS, LANES))


def _outproj(ys, ws, xres, x_tile_offset, n_tiles, mod, mod_seg, g, rwt, rb):
    n_in = len(ys)
    n_tok = BATCH * n_tiles * TM
    in_specs = []
    for y in ys:
        in_specs.append(pl.BlockSpec((1, TM, y.shape[2]), lambda b, j: (b, j, 0)))
    for w in ws:
        in_specs.append(pl.BlockSpec(w.shape, lambda b, j: (0, 0)))
    in_specs += [
        pl.BlockSpec((1, TM, D_MODEL), lambda b, j: (b, j + x_tile_offset, 0)),
        pl.BlockSpec((1, 1, 8, D_MODEL), lambda b, j: (b, mod_seg(j), 0, 0)),
        pl.BlockSpec((1, D_MODEL), lambda b, j: (0, 0)),
        pl.BlockSpec((N_EXPERTS, D_MODEL), lambda b, j: (0, 0)),
        pl.BlockSpec((N_EXPERTS, 1), lambda b, j: (0, 0)),
    ]
    return pl.pallas_call(
        functools.partial(_outproj_kernel, n_in),
        grid=(BATCH, n_tiles),
        in_specs=in_specs,
        out_specs=[
            pl.BlockSpec((1, TM, D_MODEL), lambda b, j: (b, j, 0)),
            pl.BlockSpec((TM, D_MODEL), lambda b, j: (b * n_tiles + j, 0)),
            pl.BlockSpec((8, TM), lambda b, j: (0, b * n_tiles + j)),
            pl.BlockSpec((8, TM), lambda b, j: (0, b * n_tiles + j)),
            pl.BlockSpec((1, N_EXPERTS, LANES), lambda b, j: (b * n_tiles + j, 0, 0)),
        ],
        out_shape=[
            jax.ShapeDtypeStruct((BATCH, n_tiles * TM, D_MODEL), F32),
            jax.ShapeDtypeStruct((n_tok, D_MODEL), F32),
            jax.ShapeDtypeStruct((8, n_tok), I32),
            jax.ShapeDtypeStruct((8, n_tok), F32),
            jax.ShapeDtypeStruct((BATCH * n_tiles, N_EXPERTS, LANES), I32),
        ],
        compiler_params=_params(("arbitrary", "arbitrary"), VMEM_LIMIT),
        name="outproj_router",
    )(*ys, *ws, xres, mod, g, rwt, rb)


def _moe_rows(n_tok):
    n_assign = n_tok * TOP_K
    n_blocks = -(-(n_assign + N_EXPERTS * (MOE_BM - 1)) // MOE_BM)
    return n_assign, n_blocks


def _route_tables(cnt, n_tok):
    _, n_blocks = _moe_rows(n_tok)
    counts = jnp.sum(cnt, axis=0)
    padded = (counts + MOE_BM - 1) // MOE_BM * MOE_BM
    pad_end = jnp.cumsum(padded)
    block_start = pad_end - padded
    seg_start = block_start[None, :] + jnp.cumsum(cnt, axis=0) - cnt
    blk_row = jnp.arange(n_blocks, dtype=I32) * MOE_BM
    block_e = jnp.minimum(jnp.sum((pad_end[None, :] <= blk_row[:, None]).astype(I32), axis=1), N_EXPERTS - 1)
    n_used = (pad_end[-1] // MOE_BM).reshape(1)
    return dict(block_e=block_e.astype(I32), n_used=n_used.astype(I32),
                seg_start=seg_start.reshape(-1).astype(I32), cnt=cnt.reshape(-1).astype(I32),
                pad_start=(block_start + counts).astype(I32), pad_len=(padded - counts).astype(I32))


def _store_row_slabs(ref, val, n_rows):
    for s in range(ROW_SLABS):
        ref[pl.ds(s, n_rows, stride=ROW_SLABS), :] = val[:, s * LANES:(s + 1) * LANES]


def _load_row_slabs(ref, n_rows):
    return jnp.concatenate(
        [ref[pl.ds(s, n_rows, stride=ROW_SLABS), :] for s in range(ROW_SLABS)], axis=1)


def _slab_rows(start, n):
    return pl.ds(pl.multiple_of(start * ROW_SLABS, ROW_SLABS), n * ROW_SLABS)


def _local_positions(te):
    eidx = lax.broadcasted_iota(I32, (N_EXPERTS, TM), 0)
    hits = [te[k:k + 1] == eidx for k in range(TOP_K)]
    onehot = jnp.where(hits[0], 1.0, 0.0)
    for k in range(1, TOP_K):
        onehot = onehot + jnp.where(hits[k], 1.0, 0.0)
    mb = onehot.astype(BF16)
    trow = lax.broadcasted_iota(I32, (TM, TM), 0)
    tcol = lax.broadcasted_iota(I32, (TM, TM), 1)
    before = _dot(mb, (trow < tcol).astype(BF16))
    totals = _dot(mb, jnp.ones((TM, TM), BF16))
    erow = lax.broadcasted_iota(I32, (N_EXPERTS, N_EXPERTS), 0)
    ecol = lax.broadcasted_iota(I32, (N_EXPERTS, N_EXPERTS), 1)
    first = _dot((ecol < erow).astype(BF16), totals.astype(BF16))
    base = first + before
    return [jnp.sum(jnp.where(hits[k], base, 0.0), axis=0, keepdims=True).astype(I32) for k in range(TOP_K)]


def _dispatch_kernel(n_blocks, ss_ref, cn_ref, ps_ref, pl_ref, nu_ref, h_ref, te_ref, xs_hbm, lpos_ref,
                     sbuf, zbuf, sem):
    j = pl.program_id(0)
    n_tiles = pl.num_programs(0)

    def zero_fill(act):
        def per_expert(e, carry):
            n = pl_ref[e]

            @pl.when(n > 0)
            def _():
                act(pltpu.make_async_copy(zbuf.at[_slab_rows(0, n)], xs_hbm.at[_slab_rows(ps_ref[e], n)],
                                          sem.at[1]))
            return carry
        lax.fori_loop(0, N_EXPERTS, per_expert, 0)

        def per_block(b, carry):
            act(pltpu.make_async_copy(zbuf, xs_hbm.at[_slab_rows(b * MOE_BM, MOE_BM)], sem.at[1]))
            return carry
        lax.fori_loop(nu_ref[0], n_blocks, per_block, 0)

    @pl.when(j == 0)
    def _():
        zbuf[...] = jnp.zeros_like(zbuf)
        zero_fill(lambda cp: cp.start())

    lpos = _local_positions(te_ref[...])
    zi = jnp.zeros_like(lpos[0])
    lpos_ref[...] = jnp.concatenate(lpos + [zi] * (8 - TOP_K), axis=0)
    riota = lax.broadcasted_iota(I32, (TILE_ROWS, TM), 0)
    hit = riota == lpos[0]
    for k in range(1, TOP_K):
        hit = jnp.logical_or(hit, riota == lpos[k])
    perm = jnp.where(hit, 1.0, 0.0).astype(BF16)
    xl = _dot(perm, h_ref[...].astype(BF16))

    def rows_copy():
        return pltpu.make_async_copy(sbuf, xs_hbm.at[_slab_rows(0, TILE_ROWS)], sem.at[0])

    @pl.when(j > 0)
    def _():
        rows_copy().wait()

    _store_row_slabs(sbuf, xl, TILE_ROWS)

    def segment(e, local):
        n = cn_ref[j * N_EXPERTS + e]

        @pl.when(n > 0)
        def _():
            pltpu.make_async_copy(sbuf.at[_slab_rows(local, n)],
                                  xs_hbm.at[_slab_rows(ss_ref[j * N_EXPERTS + e], n)], sem.at[0]).start()
        return local + n
    lax.fori_loop(0, N_EXPERTS, segment, 0)

    @pl.when(j == n_tiles - 1)
    def _():
        rows_copy().wait()
        zero_fill(lambda cp: cp.wait())


def _dispatch(h, top_e, tabs, n_tok):
    _, n_blocks = _moe_rows(n_tok)
    n_tiles = n_tok // TM
    grid_spec = pltpu.PrefetchScalarGridSpec(
        num_scalar_prefetch=5,
        grid=(n_tiles,),
        in_specs=[
            pl.BlockSpec((TM, D_MODEL), lambda j, *_: (j, 0)),
            pl.BlockSpec((8, TM), lambda j, *_: (0, j)),
        ],
        out_specs=[
            pl.BlockSpec(memory_space=pl.ANY),
            pl.BlockSpec((8, TM), lambda j, *_: (0, j)),
        ],
        scratch_shapes=[
            pltpu.VMEM((TILE_ROWS * ROW_SLABS, LANES), F32),
            pltpu.VMEM((MOE_BM * ROW_SLABS, LANES), F32),
            pltpu.SemaphoreType.DMA((2,)),
        ],
    )
    return pl.pallas_call(
        functools.partial(_dispatch_kernel, n_blocks),
        grid_spec=grid_spec,
        out_shape=[
            jax.ShapeDtypeStruct((n_blocks * MOE_BM * ROW_SLABS, LANES), F32),
            jax.ShapeDtypeStruct((8, n_tok), I32),
        ],
        compiler_params=_params(("arbitrary",), VMEM_LIMIT),
        name="moe_dispatch",
    )(tabs["seg_start"], tabs["cnt"], tabs["pad_start"], tabs["pad_len"], tabs["n_used"], h, top_e)


def _moe_kernel(be_ref, nu_ref, x_ref, w1_ref, b1_ref, w2_ref, b2_ref, y_ref, w1b, w2b):
    i = pl.program_id(0)
    nu = nu_ref[0]

    @pl.when(i < nu)
    def _():
        first = jnp.logical_or(i == 0, be_ref[i] != be_ref[jnp.maximum(i - 1, 0)])

        @pl.when(first)
        def _():
            w1b[...] = w1_ref[0, 0].astype(BF16)
            w2b[...] = w2_ref[0, 0].astype(BF16)

        x = _load_row_slabs(x_ref, MOE_BM).astype(BF16)
        h1 = _dot(x, w1b[...]) + b1_ref[0, 0]
        gate = jnp.minimum(h1[:, :D_MODEL], SWIGLU_LIMIT)
        up = jnp.clip(h1[:, D_MODEL:], -SWIGLU_LIMIT, SWIGLU_LIMIT)
        act = (up + 1.0) * gate * _sigmoid(SWIGLU_ALPHA * gate)
        y = _dot(act.astype(BF16), w2b[...]) + b2_ref[0, 0]
        _store_row_slabs(y_ref, y, MOE_BM)

    @pl.when(i >= nu)
    def _():
        y_ref[...] = jnp.zeros_like(y_ref)


def _moe(xs, tabs, layer, w1, b1, w2, b2, n_tok):
    _, n_blocks = _moe_rows(n_tok)
    d2 = 2 * D_MODEL
    blk = MOE_BM * ROW_SLABS
    grid_spec = pltpu.PrefetchScalarGridSpec(
        num_scalar_prefetch=2,
        grid=(n_blocks,),
        in_specs=[
            pl.BlockSpec((blk, LANES), lambda i, be, nu: (jnp.minimum(i, nu[0] - 1), 0)),
            pl.BlockSpec((1, 1, D_MODEL, d2), lambda i, be, nu: (layer, be[i], 0, 0)),
            pl.BlockSpec((1, 1, 1, d2), lambda i, be, nu: (layer, be[i], 0, 0)),
            pl.BlockSpec((1, 1, D_MODEL, D_MODEL), lambda i, be, nu: (layer, be[i], 0, 0)),
            pl.BlockSpec((1, 1, 1, D_MODEL), lambda i, be, nu: (layer, be[i], 0, 0)),
        ],
        out_specs=pl.BlockSpec((blk, LANES), lambda i, be, nu: (i, 0)),
        scratch_shapes=[
            pltpu.VMEM((D_MODEL, d2), BF16),
            pltpu.VMEM((D_MODEL, D_MODEL), BF16),
        ],
    )
    return pl.pallas_call(
        _moe_kernel,
        grid_spec=grid_spec,
        out_shape=jax.ShapeDtypeStruct((n_blocks * blk, LANES), F32),
        compiler_params=_params(("arbitrary",), VMEM_LIMIT),
        name="moe_experts",
    )(tabs["block_e"], tabs["n_used"], xs, w1, b1.reshape(DEPTH, N_EXPERTS, 1, d2),
      w2, b2.reshape(DEPTH, N_EXPERTS, 1, D_MODEL))


def _combine_kernel(final, n_tiles, ss_ref, cn_ref, lpos_ref, tg_ref, x_ref, mod_ref, g_ref, ys_hbm, o_ref,
                    cbuf, sem):
    t = pl.program_id(0) * n_tiles + pl.program_id(1)
    n_total = pl.num_programs(0) * n_tiles
    slot = t % 2

    def fetch(tile, sl):
        def segment(e, local):
            n = cn_ref[tile * N_EXPERTS + e]

            @pl.when(n > 0)
            def _():
                pltpu.make_async_copy(ys_hbm.at[_slab_rows(ss_ref[tile * N_EXPERTS + e], n)],
                                      cbuf.at[sl, _slab_rows(local, n)], sem.at[sl]).start()
            return local + n
        lax.fori_loop(0, N_EXPERTS, segment, 0)

    @pl.when(t == 0)
    def _():
        fetch(0, 0)

    @pl.when(t + 1 < n_total)
    def _():
        fetch(t + 1, 1 - slot)

    pltpu.make_async_copy(ys_hbm.at[_slab_rows(0, TILE_ROWS)], cbuf.at[slot], sem.at[slot]).wait()
    yb = _load_row_slabs(cbuf.at[slot], TILE_ROWS).astype(BF16)

    lpos = lpos_ref[...]
    tg = tg_ref[...]
    riota = lax.broadcasted_iota(I32, (TILE_ROWS, TM), 0)
    gsel = jnp.where(riota == lpos[0:1], tg[0:1], 0.0)
    for k in range(1, TOP_K):
        gsel = gsel + jnp.where(riota == lpos[k:k + 1], tg[k:k + 1], 0.0)
    g_hi = gsel.astype(BF16)
    g_lo = (gsel - g_hi.astype(F32)).astype(BF16)
    f = _dot_tn(g_hi, yb) + _dot_tn(g_lo, yb)

    mod = mod_ref[0, 0]
    xn = x_ref[0] + mod[5:6] * f
    if final:
        ms = jnp.mean(xn * xn, axis=-1, keepdims=True)
        xn = xn * lax.rsqrt(ms + EPS) * g_ref[...]
    o_ref[0] = xn


def _combine(ys, tabs, lpos, top_g, xres, n_tiles, mod, mod_seg, g, final):
    grid_spec = pltpu.PrefetchScalarGridSpec(
        num_scalar_prefetch=2,
        grid=(BATCH, n_tiles),
        in_specs=[
            pl.BlockSpec((8, TM), lambda b, j, *_: (0, b * n_tiles + j)),
            pl.BlockSpec((8, TM), lambda b, j, *_: (0, b * n_tiles + j)),
            pl.BlockSpec((1, TM, D_MODEL), lambda b, j, *_: (b, j, 0)),
            pl.BlockSpec((1, 1, 8, D_MODEL), lambda b, j, *_: (b, mod_seg(j), 0, 0)),
            pl.BlockSpec((1, D_MODEL), lambda b, j, *_: (0, 0)),
            pl.BlockSpec(memory_space=pl.ANY),
        ],
        out_specs=pl.BlockSpec((1, TM, D_MODEL), lambda b, j, *_: (b, j, 0)),
        scratch_shapes=[
            pltpu.VMEM((2, TILE_ROWS * ROW_SLABS, LANES), F32),
            pltpu.SemaphoreType.DMA((2,)),
        ],
    )
    return pl.pallas_call(
        functools.partial(_combine_kernel, final, n_tiles),
        grid_spec=grid_spec,
        out_shape=jax.ShapeDtypeStruct((BATCH, n_tiles * TM, D_MODEL), F32),
        compiler_params=_params(("arbitrary", "arbitrary"), VMEM_LIMIT),
        name="moe_combine",
    )(tabs["seg_start"], tabs["cnt"], lpos, top_g, xres, mod, g, ys)


def _moe_ffn(h, top_e, top_g, cnt, layer, xres, n_tiles, mod, mod_seg, w1, b1, w2, b2, g, final):
    n_tok = BATCH * n_tiles * TM
    tabs = _route_tables(cnt[:, :, 0], n_tok)
    xs, lpos = _dispatch(h, top_e, tabs, n_tok)
    ys = _moe(xs, tabs, layer, w1, b1, w2, b2, n_tok)
    return _combine(ys, tabs, lpos, top_g, xres, n_tiles, mod, mod_seg, g, final)


def _qkv_kernel(x_ref, mod_ref, g_ref, w_ref, qn_ref, kn_ref, cos_ref, sin_ref, q_ref, k_ref, v_ref):
    mod = mod_ref[0, 0]
    h = _norm_mod(x_ref[0], g_ref[...], mod[0:1], mod[1:2]).astype(BF16)
    qkv = _dot(h, w_ref[...])
    cos = cos_ref[...]
    sin = sin_ref[...]
    lane = lax.broadcasted_iota(I32, (TM, C_HEAD_DIM), 1)
    first_half = (lane % (C_HEAD_DIM // 2)) < (C_HEAD_DIM // 4)
    quarter = C_HEAD_DIM // 4

    def head(xh, gn):
        ms = jnp.mean(xh * xh, axis=-1, keepdims=True)
        y = xh * lax.rsqrt(ms + EPS) * gn
        partner = jnp.where(first_half, pltpu.roll(y, C_HEAD_DIM - quarter, 1), pltpu.roll(y, quarter, 1))
        return y * cos + partner * sin

    qn = qn_ref[...]
    kn = kn_ref[...]
    for hq in range(C_HEADS):
        sl = slice(hq * C_HEAD_DIM, (hq + 1) * C_HEAD_DIM)
        q_ref[0, :, sl] = head(qkv[:, sl], qn).astype(BF16)
    for hk in range(C_KV_HEADS):
        src = slice((C_HEADS + hk) * C_HEAD_DIM, (C_HEADS + hk + 1) * C_HEAD_DIM)
        k_ref[0, :, hk * C_HEAD_DIM:(hk + 1) * C_HEAD_DIM] = head(qkv[:, src], kn).astype(BF16)
    v_ref[0] = qkv[:, (C_HEADS + C_KV_HEADS) * C_HEAD_DIM:].astype(BF16)


def _qkv(xall, mod, g, w, qn, kn, cos, sin):
    nj = L_ALL // TM
    kvw = C_KV_HEADS * C_HEAD_DIM
    return pl.pallas_call(
        _qkv_kernel,
        grid=(BATCH, nj),
        in_specs=[
            pl.BlockSpec((1, TM, D_MODEL), lambda b, j: (b, j, 0)),
            pl.BlockSpec((1, 1, 8, D_MODEL), lambda b, j: (b, jnp.minimum(j, 1), 0, 0)),
            pl.BlockSpec((1, D_MODEL), lambda b, j: (0, 0)),
            pl.BlockSpec((D_MODEL, C_QKV), lambda b, j: (0, 0)),
            pl.BlockSpec((1, C_HEAD_DIM), lambda b, j: (0, 0)),
            pl.BlockSpec((1, C_HEAD_DIM), lambda b, j: (0, 0)),
            pl.BlockSpec((TM, C_HEAD_DIM), lambda b, j: (j, 0)),
            pl.BlockSpec((TM, C_HEAD_DIM), lambda b, j: (j, 0)),
        ],
        out_specs=[
            pl.BlockSpec((1, TM, D_MODEL), lambda b, j: (b, j, 0)),
            pl.BlockSpec((1, TM, kvw), lambda b, j: (b, j, 0)),
            pl.BlockSpec((1, TM, kvw), lambda b, j: (b, j, 0)),
        ],
        out_shape=[
            jax.ShapeDtypeStruct((BATCH, L_ALL, D_MODEL), BF16),
            jax.ShapeDtypeStruct((BATCH, L_ALL, kvw), BF16),
            jax.ShapeDtypeStruct((BATCH, L_ALL, kvw), BF16),
        ],
        compiler_params=_params(("arbitrary", "arbitrary"), VMEM_LIMIT),
        name="qkv_rope",
    )(xall, mod, g, w, qn, kn, cos, sin)


def _rope_tables():
    rows = SEQ // GRID_W
    row = jnp.repeat(jnp.arange(rows), GRID_W).astype(F32)
    col = jnp.tile(jnp.arange(GRID_W), rows).astype(F32)
    half = C_HEAD_DIM // 2
    inv_freq = ROPE_THETA ** (-jnp.arange(0, half, 2, dtype=F32) / half)
    ar = row[:, None] * inv_freq
    ac = col[:, None] * inv_freq
    cos = jnp.concatenate([jnp.cos(ar), jnp.cos(ar), jnp.cos(ac), jnp.cos(ac)], axis=-1)
    sin = jnp.concatenate([-jnp.sin(ar), jnp.sin(ar), -jnp.sin(ac), jnp.sin(ac)], axis=-1)
    cos = jnp.concatenate([jnp.ones((CTX_LEN, C_HEAD_DIM), F32), cos], axis=0)
    sin = jnp.concatenate([jnp.zeros((CTX_LEN, C_HEAD_DIM), F32), sin], axis=0)
    return cos, sin


def _attn_kernel(q_ref, k_ref, v_ref, o_ref):
    scale = C_HEAD_DIM ** -0.5
    for g in range(C_KV_HEADS):
        kg = k_ref[0, :, g * C_HEAD_DIM:(g + 1) * C_HEAD_DIM]
        vg = v_ref[0, :, g * C_HEAD_DIM:(g + 1) * C_HEAD_DIM]
        for hh in range(C_GROUP):
            sl = slice((g * C_GROUP + hh) * C_HEAD_DIM, (g * C_GROUP + hh + 1) * C_HEAD_DIM)
            s = _dot_nt(q_ref[0, :, sl], kg) * scale
            m = jnp.max(s, axis=-1, keepdims=True)
            p = jnp.exp(s - m)
            den = jnp.sum(p, axis=-1, keepdims=True)
            o = _dot(p.astype(BF16), vg) / den
            o_ref[0, :, sl] = o.astype(BF16)


def _attention(q, k, v):
    kvw = C_KV_HEADS * C_HEAD_DIM
    ctx_tiles = CTX_LEN // TM
    return pl.pallas_call(
        _attn_kernel,
        grid=(BATCH, SEQ // TM),
        in_specs=[
            pl.BlockSpec((1, TM, D_MODEL), lambda b, j: (b, j + ctx_tiles, 0)),
            pl.BlockSpec((1, L_ALL, kvw), lambda b, j: (b, 0, 0)),
            pl.BlockSpec((1, L_ALL, kvw), lambda b, j: (b, 0, 0)),
        ],
        out_specs=pl.BlockSpec((1, TM, D_MODEL), lambda b, j: (b, j, 0)),
        out_shape=jax.ShapeDtypeStruct((BATCH, SEQ, D_MODEL), BF16),
        compiler_params=_params(("arbitrary", "arbitrary"), VMEM_LIMIT),
        name="attention",
    )(q, k, v)


def _mod_table(ada_layer):
    m = ada_layer.reshape(16, N_MOD, D_MODEL)
    m_lat = m[:BATCH]
    m_ctx = jnp.broadcast_to(m[BATCH], (BATCH, N_MOD, D_MODEL))
    t = jnp.stack([m_ctx, m_lat], axis=1)
    return jnp.pad(t, ((0, 0), (0, 0), (0, 8 - N_MOD), (0, 0)))


def kernel(x, c, ctx, c_ctx, ada_w, ada_b, norm_mix, norm_ffn, ab_w_in, ab_gate_w, ab_gate_b, ab_out_norm,
           ab_conv_w, ab_conv_b, ab_w_out, attn_w_qkv, attn_q_norm, attn_k_norm, attn_w_o, router_w,
           router_b, moe_w1, moe_b1, moe_w2, moe_b2, final_norm):
    cond = jnp.zeros((16, D_MODEL), F32).at[:BATCH].set(c).at[BATCH].set(c_ctx)
    ada = _ada(cond, ada_w, ada_b)
    mod0 = _mod_table(ada[0])
    mod1 = _mod_table(ada[1])
    seg_all = lambda j: jnp.minimum(j, 1)
    seg_lat = lambda j: 1
    n_tiles_all = L_ALL // TM
    n_tiles_lat = SEQ // TM
    fnorm = final_norm.reshape(1, D_MODEL)

    xall = jnp.concatenate([ctx, x], axis=1)
    w_in = ab_w_in[0]
    lr0 = 2 * A_HEADS * A_DK + A_HEADS * A_DV
    lr1 = lr0 + 2 * A_GATE_RANK
    w_main = jnp.concatenate([w_in[:, :lr0], w_in[:, lr1:]], axis=1).astype(BF16)
    w_lr = jnp.pad(w_in[:, lr0:lr1], ((0, 0), (0, LANES - 2 * A_GATE_RANK))).astype(BF16)
    p, lr = _proj_in(xall, mod0, norm_mix[0:1], w_main, w_lr)

    gw = ab_gate_w[0]
    gw_pad = jnp.zeros((2, LANES, A_HEADS * A_DK), F32)
    gw_pad = gw_pad.at[0, :A_GATE_RANK].set(gw[0]).at[1, A_GATE_RANK:2 * A_GATE_RANK].set(gw[1])
    cw = jnp.concatenate([ab_conv_w[0], ab_conv_b[0][None], jnp.zeros((4, B_WIDTH), F32)], axis=0)
    yg, yc = _gla_conv(p, lr, gw_pad.astype(BF16), ab_gate_b[0].reshape(2, 1, A_HEADS * A_DK),
                       ab_out_norm[0:1], cw)

    w_out = ab_w_out[0].astype(BF16)
    ngla = A_HEADS * A_DV
    xmid, h, top_e, top_g, cnt = _outproj(
        [yg, yc], [w_out[:ngla], w_out[ngla:]], xall, 0, n_tiles_all, mod0, seg_all,
        norm_ffn[0:1], router_w[0].T, router_b[0].reshape(N_EXPERTS, 1))
    x1 = _moe_ffn(h, top_e, top_g, cnt, 0, xmid, n_tiles_all, mod0, seg_all,
                  moe_w1, moe_b1, moe_w2, moe_b2, fnorm, False)

    cos, sin = _rope_tables()
    q, k, v = _qkv(x1, mod1, norm_mix[1:2], attn_w_qkv[0].astype(BF16), attn_q_norm[0:1],
                   attn_k_norm[0:1], cos, sin)
    o = _attention(q, k, v)
    xmid, h, top_e, top_g, cnt = _outproj(
        [o], [attn_w_o[0].astype(BF16)], x1, CTX_LEN // TM, n_tiles_lat, mod1, seg_lat,
        norm_ffn[1:2], router_w[1].T, router_b[1].reshape(N_EXPERTS, 1))
    return _moe_ffn(h, top_e, top_g, cnt, 1, xmid, n_tiles_lat, mod1, seg_lat,
                    moe_w1, moe_b1, moe_w2, moe_b2, fnorm, True)
```

```python
import functools

import jax
import jax.numpy as jnp
from jax import lax
from jax.experimental import pallas as pl
from jax.experimental.pallas import tpu as pltpu

F32 = jnp.float32
BF16 = jnp.bfloat16
I32 = jnp.int32

D_MODEL = 1024
BATCH = 8
SEQ = 2048
DEPTH = 2
GRID_W = 64
CTX_LEN = 256
L_ALL = CTX_LEN + SEQ
N_MOD = 6
EPS = 1e-6

A_DV = 128
A_HEADS = 4
A_DK = 64
A_GATE_RANK = 16
A_GATE_TAU = 16.0
GLA_CHUNK = 64
B_WIDTH = 512
AB_MAIN = 3072

C_HEAD_DIM = 128
C_HEADS = 8
C_KV_HEADS = 2
C_GROUP = 4
C_QKV = (C_HEADS + 2 * C_KV_HEADS) * C_HEAD_DIM
ROPE_THETA = 10000.0
LOG2_E = 1.4426950408889634

N_EXPERTS = 32
TOP_K = 4
SWIGLU_LIMIT = 7.0
SWIGLU_ALPHA = 1.702

LANES = 128
SUBLANES = 8
ROW_SLABS = D_MODEL // LANES
TM = 256
TILE_ROWS = TM * TOP_K
MOE_BM = 256
VMEM_LIMIT = 56 * 1024 * 1024

NT_DIMS = (((1,), (1,)), ((), ()))
TN_DIMS = (((0,), (0,)), ((), ()))


def _dot(a, b):
    return jnp.dot(a, b, preferred_element_type=F32)


def _dot_nt(a, b):
    return lax.dot_general(a, b, NT_DIMS, preferred_element_type=F32)


def _dot_tn(a, b):
    return lax.dot_general(a, b, TN_DIMS, preferred_element_type=F32)


def _params(sem, vmem=None):
    return pltpu.CompilerParams(dimension_semantics=sem, vmem_limit_bytes=vmem)


def _norm_mod(x, g, shift, scale):
    ms = jnp.mean(x * x, axis=-1, keepdims=True)
    y = x * lax.rsqrt(ms + EPS) * g
    return y * (1.0 + scale) + shift


def _sigmoid(x):
    return 1.0 / (1.0 + jnp.exp(-x))


def _ada_kernel(cond_ref, w_ref, b_ref, o_ref):
    c = cond_ref[...]
    s = (c * _sigmoid(c)).astype(BF16)
    o_ref[0] = _dot(s, w_ref[0].astype(BF16)) + b_ref[0]


def _ada(cond, ada_w, ada_b):
    tn = 1536
    n = N_MOD * D_MODEL
    return pl.pallas_call(
        _ada_kernel,
        grid=(DEPTH, n // tn),
        in_specs=[
            pl.BlockSpec((16, D_MODEL), lambda l, j: (0, 0)),
            pl.BlockSpec((1, D_MODEL, tn), lambda l, j: (l, 0, j)),
            pl.BlockSpec((1, 1, tn), lambda l, j: (l, 0, j)),
        ],
        out_specs=pl.BlockSpec((1, 16, tn), lambda l, j: (l, 0, j)),
        out_shape=jax.ShapeDtypeStruct((DEPTH, 16, n), F32),
        compiler_params=_params(("arbitrary", "arbitrary"), VMEM_LIMIT),
        name="ada",
    )(cond, ada_w, ada_b.reshape(DEPTH, 1, n))


def _proj_in_kernel(x_ref, mod_ref, g_ref, w_ref, wlr_ref, p_ref, lr_ref):
    mod = mod_ref[0, 0]
    h = _norm_mod(x_ref[0], g_ref[...], mod[0:1], mod[1:2]).astype(BF16)
    p_ref[0] = _dot(h, w_ref[...])
    lr_ref[0] = _dot(h, wlr_ref[...])


def _proj_in(xall, mod, g, w_main, w_lr):
    nj = L_ALL // TM
    return pl.pallas_call(
        _proj_in_kernel,
        grid=(BATCH, nj),
        in_specs=[
            pl.BlockSpec((1, TM, D_MODEL), lambda b, j: (b, j, 0)),
            pl.BlockSpec((1, 1, 8, D_MODEL), lambda b, j: (b, jnp.minimum(j, 1), 0, 0)),
            pl.BlockSpec((1, D_MODEL), lambda b, j: (0, 0)),
            pl.BlockSpec((D_MODEL, AB_MAIN), lambda b, j: (0, 0)),
            pl.BlockSpec((D_MODEL, LANES), lambda b, j: (0, 0)),
        ],
        out_specs=[
            pl.BlockSpec((1, TM, AB_MAIN), lambda b, j: (b, j, 0)),
            pl.BlockSpec((1, TM, LANES), lambda b, j: (b, j, 0)),
        ],
        out_shape=[
            jax.ShapeDtypeStruct((BATCH, L_ALL, AB_MAIN), F32),
            jax.ShapeDtypeStruct((BATCH, L_ALL, LANES), F32),
        ],
        compiler_params=_params(("arbitrary", "arbitrary"), VMEM_LIMIT),
        name="proj_in",
    )(xall, mod, g, w_main, w_lr)


def _log_sigmoid(z):
    return jnp.minimum(z, 0.0) - jnp.log1p(jnp.exp(-jnp.abs(z)))


def _gla_conv_kernel(q_ref, k_ref, v_ref, r_ref, gb_ref, gc_ref, u_ref, lr_ref, gw_ref, gbias_ref,
                     onorm_ref, cw_ref, yg_ref, yc_ref, of_ref, ob_ref, sf_ref, sb_ref, xs_ref):
    ch = GLA_CHUNK
    grp = TM
    n_grp = L_ALL // grp
    cpg = grp // ch
    row = lax.broadcasted_iota(I32, (grp, grp), 0)
    col = lax.broadcasted_iota(I32, (grp, grp), 1)
    same_chunk = (row // ch) == (col // ch)
    tri = (jnp.logical_and(same_chunk, row >= col), jnp.logical_and(same_chunk, col >= row))
    tri_bf = (tri[0].astype(BF16), tri[1].astype(BF16))
    lane = lax.broadcasted_iota(I32, (grp, LANES), 1)
    head_mask = (lane < A_DK, lane >= A_DK)
    chunk_of_row = lax.broadcasted_iota(I32, (grp, LANES), 0) // ch

    sf_ref[...] = jnp.zeros_like(sf_ref)
    sb_ref[...] = jnp.zeros_like(sb_ref)

    def chunk_rows(x, idx):
        return jnp.concatenate(
            [jnp.broadcast_to(x[c * ch + idx:c * ch + idx + 1], (ch, x.shape[1])) for c in range(cpg)], axis=0)

    def group(g, direction, s_ref, o_ref):
        rows = pl.ds(pl.multiple_of(g * grp, grp), grp)
        q = q_ref[0, rows, :] * (A_DK ** -0.5)
        k = k_ref[0, rows, :]
        v = v_ref[0, rows, :]
        lr = lr_ref[0, rows, :].astype(BF16)
        z = _dot(lr, gw_ref[direction]) + gbias_ref[direction]
        a = _log_sigmoid(z) * (1.0 / A_GATE_TAU)
        a_hi = a.astype(BF16)
        a_lo = (a - a_hi.astype(F32)).astype(BF16)
        cum = _dot(tri_bf[direction], a_hi) + _dot(tri_bf[direction], a_lo)
        if direction == 0:
            ref, last, order = chunk_rows(cum, ch // 2 - 1), chunk_rows(cum, ch - 1), range(cpg)
        else:
            ref, last, order = chunk_rows(cum, ch // 2), chunk_rows(cum, 0), range(cpg - 1, -1, -1)
        qe = q * jnp.exp(cum)
        qt = q * jnp.exp(cum - ref)
        kt = (k * jnp.exp(ref - cum)).astype(BF16)
        kl = k * jnp.exp(last - cum)
        dec = jnp.exp(last)
        for hh in range(2):
            m = head_mask[hh]
            vh = v[:, hh * A_DV:(hh + 1) * A_DV].astype(BF16)
            sc = _dot_nt(jnp.where(m, qt, 0.0).astype(BF16), kt)
            sc = jnp.where(tri[direction], sc, 0.0)
            o_intra = _dot(sc.astype(BF16), vh)
            qx = jnp.concatenate([jnp.where(jnp.logical_and(m, chunk_of_row == c), qe, 0.0) for c in range(cpg)],
                                 axis=1).astype(BF16)
            kx = jnp.concatenate([jnp.where(jnp.logical_and(m, chunk_of_row == c), kl, 0.0) for c in range(cpg)],
                                 axis=1).astype(BF16)
            kv = _dot(vh.T, kx)
            st = s_ref[hh]
            states = [None] * cpg
            for c in order:
                states[c] = st
                st = st * dec[c * ch:c * ch + 1] + kv[:, c * LANES:(c + 1) * LANES]
            s_ref[hh] = st
            o_inter = _dot_nt(qx, jnp.concatenate(states, axis=1).astype(BF16))
            o_ref[rows, hh * A_DV:(hh + 1) * A_DV] = o_intra + o_inter

    def body(i, carry):
        group(i, 0, sf_ref, of_ref)
        group(jnp.where(i == 0, 0, n_grp - i), 1, sb_ref, ob_ref)
        return carry

    lax.fori_loop(0, n_grp, body, 0)

    pad = SUBLANES
    xs_ref[0:pad, :] = jnp.zeros((pad, xs_ref.shape[1]), F32)
    xs_ref[pad + L_ALL:, :] = jnp.zeros((pad, xs_ref.shape[1]), F32)
    xs_ref[pad:pad + L_ALL, :] = gc_ref[0] * u_ref[0]

    cw = cw_ref[...]
    onorm = onorm_ref[...]
    trow = lax.broadcasted_iota(I32, (TM, 1), 0)
    for ti in range(L_ALL // TM):
        s0 = ti * TM
        rows = slice(s0, s0 + TM)
        o = of_ref[rows, :] + ob_ref[rows, :]
        r = r_ref[0, rows, :]
        parts = []
        for hh in range(2):
            oh = o[:, hh * A_DV:(hh + 1) * A_DV]
            ms = jnp.mean(oh * oh, axis=-1, keepdims=True)
            parts.append(oh * lax.rsqrt(ms + EPS) * onorm[:, hh * A_DV:(hh + 1) * A_DV])
        on = jnp.concatenate(parts, axis=1)
        yg_ref[0, rows, :] = (on * (r * _sigmoid(r))).astype(BF16)
        xm1 = xs_ref[pad + s0 - 1:pad + s0 - 1 + TM, :]
        x0 = xs_ref[pad + s0:pad + s0 + TM, :]
        xp1 = xs_ref[pad + s0 + 1:pad + s0 + 1 + TM, :]
        if s0 + TM == CTX_LEN:
            xp1 = jnp.where(trow == TM - 1, 0.0, xp1)
        if s0 == CTX_LEN:
            xm1 = jnp.where(trow == 0, 0.0, xm1)
        conv = cw[0:1] * xm1 + cw[1:2] * x0 + cw[2:3] * xp1 + cw[3:4]
        yc_ref[0, rows, :] = (gb_ref[0, rows, :] * conv).astype(BF16)


def _gla_conv(p, lr, gw_pad, gbias, onorm, cw):
    hw = 2 * A_DV

    def pspec(width, base):
        return pl.BlockSpec((1, L_ALL, width), lambda b, i: (b, 0, base + i))

    return pl.pallas_call(
        _gla_conv_kernel,
        grid=(BATCH, A_HEADS // 2),
        in_specs=[
            pspec(LANES, 0),
            pspec(LANES, 2),
            pspec(hw, 2),
            pspec(hw, 4),
            pspec(hw, 6),
            pspec(hw, 8),
            pspec(hw, 10),
            pl.BlockSpec((1, L_ALL, LANES), lambda b, i: (b, 0, 0)),
            pl.BlockSpec((2, LANES, LANES), lambda b, i: (0, 0, i)),
            pl.BlockSpec((2, 1, LANES), lambda b, i: (0, 0, i)),
            pl.BlockSpec((1, hw), lambda b, i: (0, i)),
            pl.BlockSpec((8, hw), lambda b, i: (0, i)),
        ],
        out_specs=[
            pl.BlockSpec((1, L_ALL, hw), lambda b, i: (b, 0, i)),
            pl.BlockSpec((1, L_ALL, hw), lambda b, i: (b, 0, i)),
        ],
        out_shape=[
            jax.ShapeDtypeStruct((BATCH, L_ALL, A_HEADS * A_DV), BF16),
            jax.ShapeDtypeStruct((BATCH, L_ALL, B_WIDTH), BF16),
        ],
        scratch_shapes=[
            pltpu.VMEM((L_ALL, hw), F32),
            pltpu.VMEM((L_ALL, hw), F32),
            pltpu.VMEM((2, A_DV, LANES), F32),
            pltpu.VMEM((2, A_DV, LANES), F32),
            pltpu.VMEM((L_ALL + 2 * SUBLANES, hw), F32),
        ],
        compiler_params=_params(("arbitrary", "arbitrary"), VMEM_LIMIT),
        name="gla_conv",
    )(p, p, p, p, p, p, p, lr, gw_pad, gbias, onorm, cw)


def _outproj_kernel(n_in, *refs):
    y_refs = refs[:n_in]
    w_refs = refs[n_in:2 * n_in]
    x_ref, mod_ref, g_ref, rwt_ref, rb_ref, xo_ref, h_ref, te_ref, tg_ref, cnt_ref = refs[2 * n_in:]
    acc = _dot(y_refs[0][0], w_refs[0][...])
    for i in range(1, n_in):
        acc = acc + _dot(y_refs[i][0], w_refs[i][...])
    mod = mod_ref[0, 0]
    xn = x_ref[0] + mod[2:3] * acc
    xo_ref[0] = xn
    h = _norm_mod(xn, g_ref[...], mod[3:4], mod[4:5])
    h_ref[...] = h

    hb = h.astype(BF16)
    hl = (h - hb.astype(F32)).astype(BF16)
    wt = rwt_ref[...]
    wb = wt.astype(BF16)
    wl = (wt - wb.astype(F32)).astype(BF16)
    logits = _dot_nt(wb, hb) + _dot_nt(wb, hl) + _dot_nt(wl, hb) + rb_ref[...]
    eidx = lax.broadcasted_iota(I32, logits.shape, 0)
    cur = logits
    vals, idxs = [], []
    for _ in range(TOP_K):
        m = jnp.max(cur, axis=0, keepdims=True)
        sel = jnp.min(jnp.where(cur == m, eidx, N_EXPERTS), axis=0, keepdims=True)
        vals.append(m)
        idxs.append(sel)
        cur = jnp.where(eidx == sel, -jnp.inf, cur)
    ex = [jnp.exp(v - vals[0]) for v in vals]
    den = ex[0] + ex[1] + ex[2] + ex[3]
    zi = jnp.zeros_like(idxs[0])
    zf = jnp.zeros_like(den)
    te_ref[...] = jnp.concatenate(idxs + [zi] * (8 - TOP_K), axis=0)
    tg_ref[...] = jnp.concatenate([e / den for e in ex] + [zf] * (8 - TOP_K), axis=0)
    onehot = jnp.where(eidx == idxs[0], 1, 0)
    for k in range(1, TOP_K):
        onehot = onehot + jnp.where(eidx == idxs[k], 1, 0)
    cnt_ref[0] = jnp.broadcast_to(jnp.sum(onehot, axis=1, keepdims=True), (N_EXPERTS, LANES))


def _outproj(ys, ws, xres, x_tile_offset, n_tiles, mod, mod_seg, g, rwt, rb):
    n_in = len(ys)
    n_tok = BATCH * n_tiles * TM
    in_specs = []
    for y in ys:
        in_specs.append(pl.BlockSpec((1, TM, y.shape[2]), lambda b, j: (b, j, 0)))
    for w in ws:
        in_specs.append(pl.BlockSpec(w.shape, lambda b, j: (0, 0)))
    in_specs += [
        pl.BlockSpec((1, TM, D_MODEL), lambda b, j: (b, j + x_tile_offset, 0)),
        pl.BlockSpec((1, 1, 8, D_MODEL), lambda b, j: (b, mod_seg(j), 0, 0)),
        pl.BlockSpec((1, D_MODEL), lambda b, j: (0, 0)),
        pl.BlockSpec((N_EXPERTS, D_MODEL), lambda b, j: (0, 0)),
        pl.BlockSpec((N_EXPERTS, 1), lambda b, j: (0, 0)),
    ]
    return pl.pallas_call(
        functools.partial(_outproj_kernel, n_in),
        grid=(BATCH, n_tiles),
        in_specs=in_specs,
        out_specs=[
            pl.BlockSpec((1, TM, D_MODEL), lambda b, j: (b, j, 0)),
            pl.BlockSpec((TM, D_MODEL), lambda b, j: (b * n_tiles + j, 0)),
            pl.BlockSpec((8, TM), lambda b, j: (0, b * n_tiles + j)),
            pl.BlockSpec((8, TM), lambda b, j: (0, b * n_tiles + j)),
            pl.BlockSpec((1, N_EXPERTS, LANES), lambda b, j: (b * n_tiles + j, 0, 0)),
        ],
        out_shape=[
            jax.ShapeDtypeStruct((BATCH, n_tiles * TM, D_MODEL), F32),
            jax.ShapeDtypeStruct((n_tok, D_MODEL), F32),
            jax.ShapeDtypeStruct((8, n_tok), I32),
            jax.ShapeDtypeStruct((8, n_tok), F32),
            jax.ShapeDtypeStruct((BATCH * n_tiles, N_EXPERTS, LANES), I32),
        ],
        compiler_params=_params(("arbitrary", "arbitrary"), VMEM_LIMIT),
        name="outproj_router",
    )(*ys, *ws, xres, mod, g, rwt, rb)


def _moe_rows(n_tok):
    n_assign = n_tok * TOP_K
    n_blocks = -(-(n_assign + N_EXPERTS * (MOE_BM - 1)) // MOE_BM)
    return n_assign, n_blocks


def _route_tables(cnt, n_tok):
    _, n_blocks = _moe_rows(n_tok)
    counts = jnp.sum(cnt, axis=0)
    padded = (counts + MOE_BM - 1) // MOE_BM * MOE_BM
    pad_end = jnp.cumsum(padded)
    block_start = pad_end - padded
    seg_start = block_start[None, :] + jnp.cumsum(cnt, axis=0) - cnt
    blk_row = jnp.arange(n_blocks, dtype=I32) * MOE_BM
    block_e = jnp.minimum(jnp.sum((pad_end[None, :] <= blk_row[:, None]).astype(I32), axis=1), N_EXPERTS - 1)
    n_used = (pad_end[-1] // MOE_BM).reshape(1)
    return dict(block_e=block_e.astype(I32), n_used=n_used.astype(I32),
                seg_start=seg_start.reshape(-1).astype(I32), cnt=cnt.reshape(-1).astype(I32),
                pad_start=(block_start + counts).astype(I32), pad_len=(padded - counts).astype(I32))


def _store_row_slabs(ref, val, n_rows):
    for s in range(ROW_SLABS):
        ref[pl.ds(s, n_rows, stride=ROW_SLABS), :] = val[:, s * LANES:(s + 1) * LANES]


def _load_row_slabs(ref, n_rows):
    return jnp.concatenate(
        [ref[pl.ds(s, n_rows, stride=ROW_SLABS), :] for s in range(ROW_SLABS)], axis=1)


def _slab_rows(start, n):
    return pl.ds(pl.multiple_of(start * ROW_SLABS, ROW_SLABS), n * ROW_SLABS)


def _local_positions(te):
    eidx = lax.broadcasted_iota(I32, (N_EXPERTS, TM), 0)
    hits = [te[k:k + 1] == eidx for k in range(TOP_K)]
    onehot = jnp.where(hits[0], 1.0, 0.0)
    for k in range(1, TOP_K):
        onehot = onehot + jnp.where(hits[k], 1.0, 0.0)
    mb = onehot.astype(BF16)
    trow = lax.broadcasted_iota(I32, (TM, TM), 0)
    tcol = lax.broadcasted_iota(I32, (TM, TM), 1)
    before = _dot(mb, (trow < tcol).astype(BF16))
    totals = _dot(mb, jnp.ones((TM, TM), BF16))
    erow = lax.broadcasted_iota(I32, (N_EXPERTS, N_EXPERTS), 0)
    ecol = lax.broadcasted_iota(I32, (N_EXPERTS, N_EXPERTS), 1)
    first = _dot((ecol < erow).astype(BF16), totals.astype(BF16))
    base = first + before
    return [jnp.sum(jnp.where(hits[k], base, 0.0), axis=0, keepdims=True).astype(I32) for k in range(TOP_K)]


def _dispatch_kernel(n_blocks, ss_ref, cn_ref, ps_ref, pl_ref, nu_ref, h_ref, te_ref, xs_hbm, lpos_ref,
                     sbuf, zbuf, sem):
    j = pl.program_id(0)
    n_tiles = pl.num_programs(0)

    def zero_fill(act):
        def per_expert(e, carry):
            n = pl_ref[e]

            @pl.when(n > 0)
            def _():
                act(pltpu.make_async_copy(zbuf.at[_slab_rows(0, n)], xs_hbm.at[_slab_rows(ps_ref[e], n)],
                                          sem.at[2]))
            return carry
        lax.fori_loop(0, N_EXPERTS, per_expert, 0)

        def per_block(b, carry):
            act(pltpu.make_async_copy(zbuf, xs_hbm.at[_slab_rows(b * MOE_BM, MOE_BM)], sem.at[2]))
            return carry
        lax.fori_loop(nu_ref[0], n_blocks, per_block, 0)

    @pl.when(j == 0)
    def _():
        zbuf[...] = jnp.zeros_like(zbuf)
        zero_fill(lambda cp: cp.start())

    lpos = _local_positions(te_ref[...])
    zi = jnp.zeros_like(lpos[0])
    lpos_ref[...] = jnp.concatenate(lpos + [zi] * (8 - TOP_K), axis=0)
    riota = lax.broadcasted_iota(I32, (TILE_ROWS, TM), 0)
    hit = riota == lpos[0]
    for k in range(1, TOP_K):
        hit = jnp.logical_or(hit, riota == lpos[k])
    perm = jnp.where(hit, 1.0, 0.0).astype(BF16)
    hb = h_ref[...].astype(BF16)
    slot = j % 2
    buf = sbuf.at[slot]

    def rows_copy(sl):
        return pltpu.make_async_copy(sbuf.at[sl], xs_hbm.at[_slab_rows(0, TILE_ROWS)], sem.at[sl])

    @pl.when(j >= 2)
    def _():
        rows_copy(slot).wait()

    for s in range(0, ROW_SLABS, 2):
        xl = _dot(perm, hb[:, s * LANES:(s + 2) * LANES])
        buf[pl.ds(s, TILE_ROWS, stride=ROW_SLABS), :] = xl[:, :LANES]
        buf[pl.ds(s + 1, TILE_ROWS, stride=ROW_SLABS), :] = xl[:, LANES:]

    def segment(e, local):
        n = cn_ref[j * N_EXPERTS + e]

        @pl.when(n > 0)
        def _():
            pltpu.make_async_copy(buf.at[_slab_rows(local, n)],
                                  xs_hbm.at[_slab_rows(ss_ref[j * N_EXPERTS + e], n)], sem.at[slot]).start()
        return local + n
    lax.fori_loop(0, N_EXPERTS, segment, 0)

    @pl.when(j == n_tiles - 1)
    def _():
        rows_copy(slot).wait()
        rows_copy(1 - slot).wait()
        zero_fill(lambda cp: cp.wait())


def _dispatch(h, top_e, tabs, n_tok):
    _, n_blocks = _moe_rows(n_tok)
    n_tiles = n_tok // TM
    grid_spec = pltpu.PrefetchScalarGridSpec(
        num_scalar_prefetch=5,
        grid=(n_tiles,),
        in_specs=[
            pl.BlockSpec((TM, D_MODEL), lambda j, *_: (j, 0)),
            pl.BlockSpec((8, TM), lambda j, *_: (0, j)),
        ],
        out_specs=[
            pl.BlockSpec(memory_space=pl.ANY),
            pl.BlockSpec((8, TM), lambda j, *_: (0, j)),
        ],
        scratch_shapes=[
            pltpu.VMEM((2, TILE_ROWS * ROW_SLABS, LANES), F32),
            pltpu.VMEM((MOE_BM * ROW_SLABS, LANES), F32),
            pltpu.SemaphoreType.DMA((3,)),
        ],
    )
    return pl.pallas_call(
        functools.partial(_dispatch_kernel, n_blocks),
        grid_spec=grid_spec,
        out_shape=[
            jax.ShapeDtypeStruct((n_blocks * MOE_BM * ROW_SLABS, LANES), F32),
            jax.ShapeDtypeStruct((8, n_tok), I32),
        ],
        compiler_params=_params(("arbitrary",), VMEM_LIMIT),
        name="moe_dispatch",
    )(tabs["seg_start"], tabs["cnt"], tabs["pad_start"], tabs["pad_len"], tabs["n_used"], h, top_e)


def _moe_kernel(be_ref, nu_ref, x_ref, w1_ref, b1_ref, w2_ref, b2_ref, y_ref, w1b, w2b):
    i = pl.program_id(0)
    nu = nu_ref[0]

    @pl.when(i < nu)
    def _():
        first = jnp.logical_or(i == 0, be_ref[i] != be_ref[jnp.maximum(i - 1, 0)])

        @pl.when(first)
        def _():
            w1b[...] = w1_ref[0, 0].astype(BF16)
            w2b[...] = w2_ref[0, 0].astype(BF16)

        x = _load_row_slabs(x_ref, MOE_BM).astype(BF16)
        h1 = _dot(x, w1b[...]) + b1_ref[0, 0]
        gate = jnp.minimum(h1[:, :D_MODEL], SWIGLU_LIMIT)
        up = jnp.clip(h1[:, D_MODEL:], -SWIGLU_LIMIT, SWIGLU_LIMIT)
        act = (up + 1.0) * gate * _sigmoid(SWIGLU_ALPHA * gate)
        y = _dot(act.astype(BF16), w2b[...]) + b2_ref[0, 0]
        _store_row_slabs(y_ref, y, MOE_BM)

    @pl.when(i >= nu)
    def _():
        y_ref[...] = jnp.zeros_like(y_ref)


def _moe(xs, tabs, layer, w1, b1, w2, b2, n_tok):
    _, n_blocks = _moe_rows(n_tok)
    d2 = 2 * D_MODEL
    blk = MOE_BM * ROW_SLABS
    grid_spec = pltpu.PrefetchScalarGridSpec(
        num_scalar_prefetch=2,
        grid=(n_blocks,),
        in_specs=[
            pl.BlockSpec((blk, LANES), lambda i, be, nu: (jnp.minimum(i, nu[0] - 1), 0)),
            pl.BlockSpec((1, 1, D_MODEL, d2), lambda i, be, nu: (layer, be[i], 0, 0)),
            pl.BlockSpec((1, 1, 1, d2), lambda i, be, nu: (layer, be[i], 0, 0)),
            pl.BlockSpec((1, 1, D_MODEL, D_MODEL), lambda i, be, nu: (layer, be[i], 0, 0)),
            pl.BlockSpec((1, 1, 1, D_MODEL), lambda i, be, nu: (layer, be[i], 0, 0)),
        ],
        out_specs=pl.BlockSpec((blk, LANES), lambda i, be, nu: (i, 0)),
        scratch_shapes=[
            pltpu.VMEM((D_MODEL, d2), BF16),
            pltpu.VMEM((D_MODEL, D_MODEL), BF16),
        ],
    )
    return pl.pallas_call(
        _moe_kernel,
        grid_spec=grid_spec,
        out_shape=jax.ShapeDtypeStruct((n_blocks * blk, LANES), F32),
        compiler_params=_params(("arbitrary",), VMEM_LIMIT),
        name="moe_experts",
    )(tabs["block_e"], tabs["n_used"], xs, w1, b1.reshape(DEPTH, N_EXPERTS, 1, d2),
      w2, b2.reshape(DEPTH, N_EXPERTS, 1, D_MODEL))


def _combine_kernel(final, n_tiles, ss_ref, cn_ref, lpos_ref, tg_ref, x_ref, mod_ref, g_ref, ys_hbm, o_ref,
                    cbuf, sem):
    t = pl.program_id(0) * n_tiles + pl.program_id(1)
    n_total = pl.num_programs(0) * n_tiles
    slot = t % 2

    def fetch(tile, sl):
        def segment(e, local):
            n = cn_ref[tile * N_EXPERTS + e]

            @pl.when(n > 0)
            def _():
                pltpu.make_async_copy(ys_hbm.at[_slab_rows(ss_ref[tile * N_EXPERTS + e], n)],
                                      cbuf.at[sl, _slab_rows(local, n)], sem.at[sl]).start()
            return local + n
        lax.fori_loop(0, N_EXPERTS, segment, 0)

    @pl.when(t == 0)
    def _():
        fetch(0, 0)

    @pl.when(t + 1 < n_total)
    def _():
        fetch(t + 1, 1 - slot)

    pltpu.make_async_copy(ys_hbm.at[_slab_rows(0, TILE_ROWS)], cbuf.at[slot], sem.at[slot]).wait()
    yb = _load_row_slabs(cbuf.at[slot], TILE_ROWS).astype(BF16)

    lpos = lpos_ref[...]
    tg = tg_ref[...]
    riota = lax.broadcasted_iota(I32, (TILE_ROWS, TM), 0)
    gsel = jnp.where(riota == lpos[0:1], tg[0:1], 0.0)
    for k in range(1, TOP_K):
        gsel = gsel + jnp.where(riota == lpos[k:k + 1], tg[k:k + 1], 0.0)
    g_hi = gsel.astype(BF16)
    g_lo = (gsel - g_hi.astype(F32)).astype(BF16)
    f = _dot_tn(g_hi, yb) + _dot_tn(g_lo, yb)

    mod = mod_ref[0, 0]
    xn = x_ref[0] + mod[5:6] * f
    if final:
        ms = jnp.mean(xn * xn, axis=-1, keepdims=True)
        xn = xn * lax.rsqrt(ms + EPS) * g_ref[...]
    o_ref[0] = xn


def _combine(ys, tabs, lpos, top_g, xres, n_tiles, mod, mod_seg, g, final):
    grid_spec = pltpu.PrefetchScalarGridSpec(
        num_scalar_prefetch=2,
        grid=(BATCH, n_tiles),
        in_specs=[
            pl.BlockSpec((8, TM), lambda b, j, *_: (0, b * n_tiles + j)),
            pl.BlockSpec((8, TM), lambda b, j, *_: (0, b * n_tiles + j)),
            pl.BlockSpec((1, TM, D_MODEL), lambda b, j, *_: (b, j, 0)),
            pl.BlockSpec((1, 1, 8, D_MODEL), lambda b, j, *_: (b, mod_seg(j), 0, 0)),
            pl.BlockSpec((1, D_MODEL), lambda b, j, *_: (0, 0)),
            pl.BlockSpec(memory_space=pl.ANY),
        ],
        out_specs=pl.BlockSpec((1, TM, D_MODEL), lambda b, j, *_: (b, j, 0)),
        scratch_shapes=[
            pltpu.VMEM((2, TILE_ROWS * ROW_SLABS, LANES), F32),
            pltpu.SemaphoreType.DMA((2,)),
        ],
    )
    return pl.pallas_call(
        functools.partial(_combine_kernel, final, n_tiles),
        grid_spec=grid_spec,
        out_shape=jax.ShapeDtypeStruct((BATCH, n_tiles * TM, D_MODEL), F32),
        compiler_params=_params(("arbitrary", "arbitrary"), VMEM_LIMIT),
        name="moe_combine",
    )(tabs["seg_start"], tabs["cnt"], lpos, top_g, xres, mod, g, ys)


def _moe_ffn(h, top_e, top_g, cnt, layer, xres, n_tiles, mod, mod_seg, w1, b1, w2, b2, g, final):
    n_tok = BATCH * n_tiles * TM
    tabs = _route_tables(cnt[:, :, 0], n_tok)
    xs, lpos = _dispatch(h, top_e, tabs, n_tok)
    ys = _moe(xs, tabs, layer, w1, b1, w2, b2, n_tok)
    return _combine(ys, tabs, lpos, top_g, xres, n_tiles, mod, mod_seg, g, final)


def _qkv_kernel(x_ref, mod_ref, g_ref, w_ref, qn_ref, kn_ref, cos_ref, sin_ref, q_ref, k_ref, v_ref):
    mod = mod_ref[0, 0]
    h = _norm_mod(x_ref[0], g_ref[...], mod[0:1], mod[1:2]).astype(BF16)
    qkv = _dot(h, w_ref[...])
    cos = cos_ref[...]
    sin = sin_ref[...]
    lane = lax.broadcasted_iota(I32, (TM, C_HEAD_DIM), 1)
    first_half = (lane % (C_HEAD_DIM // 2)) < (C_HEAD_DIM // 4)
    quarter = C_HEAD_DIM // 4

    def head(xh, gn):
        ms = jnp.mean(xh * xh, axis=-1, keepdims=True)
        y = xh * lax.rsqrt(ms + EPS) * gn
        partner = jnp.where(first_half, pltpu.roll(y, C_HEAD_DIM - quarter, 1), pltpu.roll(y, quarter, 1))
        return y * cos + partner * sin

    qn = qn_ref[...]
    kn = kn_ref[...]
    for hq in range(C_HEADS):
        sl = slice(hq * C_HEAD_DIM, (hq + 1) * C_HEAD_DIM)
        q_ref[0, :, sl] = head(qkv[:, sl], qn).astype(BF16)
    for hk in range(C_KV_HEADS):
        src = slice((C_HEADS + hk) * C_HEAD_DIM, (C_HEADS + hk + 1) * C_HEAD_DIM)
        k_ref[0, :, hk * C_HEAD_DIM:(hk + 1) * C_HEAD_DIM] = head(qkv[:, src], kn).astype(BF16)
    ones = jnp.ones((TM, C_HEAD_DIM), BF16)
    for hk in range(C_KV_HEADS):
        src = slice((C_HEADS + C_KV_HEADS + hk) * C_HEAD_DIM, (C_HEADS + C_KV_HEADS + hk + 1) * C_HEAD_DIM)
        v_ref[0, :, 2 * hk * C_HEAD_DIM:(2 * hk + 1) * C_HEAD_DIM] = qkv[:, src].astype(BF16)
        v_ref[0, :, (2 * hk + 1) * C_HEAD_DIM:(2 * hk + 2) * C_HEAD_DIM] = ones


def _qkv(xall, mod, g, w, qn, kn, cos, sin):
    nj = L_ALL // TM
    kvw = C_KV_HEADS * C_HEAD_DIM
    return pl.pallas_call(
        _qkv_kernel,
        grid=(BATCH, nj),
        in_specs=[
            pl.BlockSpec((1, TM, D_MODEL), lambda b, j: (b, j, 0)),
            pl.BlockSpec((1, 1, 8, D_MODEL), lambda b, j: (b, jnp.minimum(j, 1), 0, 0)),
            pl.BlockSpec((1, D_MODEL), lambda b, j: (0, 0)),
            pl.BlockSpec((D_MODEL, C_QKV), lambda b, j: (0, 0)),
            pl.BlockSpec((1, C_HEAD_DIM), lambda b, j: (0, 0)),
            pl.BlockSpec((1, C_HEAD_DIM), lambda b, j: (0, 0)),
            pl.BlockSpec((TM, C_HEAD_DIM), lambda b, j: (j, 0)),
            pl.BlockSpec((TM, C_HEAD_DIM), lambda b, j: (j, 0)),
        ],
        out_specs=[
            pl.BlockSpec((1, TM, D_MODEL), lambda b, j: (b, j, 0)),
            pl.BlockSpec((1, TM, kvw), lambda b, j: (b, j, 0)),
            pl.BlockSpec((1, TM, 2 * kvw), lambda b, j: (b, j, 0)),
        ],
        out_shape=[
            jax.ShapeDtypeStruct((BATCH, L_ALL, D_MODEL), BF16),
            jax.ShapeDtypeStruct((BATCH, L_ALL, kvw), BF16),
            jax.ShapeDtypeStruct((BATCH, L_ALL, 2 * kvw), BF16),
        ],
        compiler_params=_params(("arbitrary", "arbitrary"), VMEM_LIMIT),
        name="qkv_rope",
    )(xall, mod, g, w, qn, kn, cos, sin)


def _rope_tables():
    rows = SEQ // GRID_W
    row = jnp.repeat(jnp.arange(rows), GRID_W).astype(F32)
    col = jnp.tile(jnp.arange(GRID_W), rows).astype(F32)
    half = C_HEAD_DIM // 2
    inv_freq = ROPE_THETA ** (-jnp.arange(0, half, 2, dtype=F32) / half)
    ar = row[:, None] * inv_freq
    ac = col[:, None] * inv_freq
    cos = jnp.concatenate([jnp.cos(ar), jnp.cos(ar), jnp.cos(ac), jnp.cos(ac)], axis=-1)
    sin = jnp.concatenate([-jnp.sin(ar), jnp.sin(ar), -jnp.sin(ac), jnp.sin(ac)], axis=-1)
    cos = jnp.concatenate([jnp.ones((CTX_LEN, C_HEAD_DIM), F32), cos], axis=0)
    sin = jnp.concatenate([jnp.zeros((CTX_LEN, C_HEAD_DIM), F32), sin], axis=0)
    return cos, sin


def _attn_kernel(q_ref, k_ref, v_ref, o_ref):
    c = (C_HEAD_DIM ** -0.5) * LOG2_E
    for g in range(C_KV_HEADS):
        kg = k_ref[0, :, g * C_HEAD_DIM:(g + 1) * C_HEAD_DIM]
        vg = v_ref[0, :, 2 * g * C_HEAD_DIM:(2 * g + 2) * C_HEAD_DIM]
        for hh in range(C_GROUP):
            sl = slice((g * C_GROUP + hh) * C_HEAD_DIM, (g * C_GROUP + hh + 1) * C_HEAD_DIM)
            s = _dot_nt(q_ref[0, :, sl], kg)
            m = jnp.max(s, axis=-1, keepdims=True)
            p = jnp.exp2((s - m) * c).astype(BF16)
            oe = _dot(p, vg)
            o_ref[0, :, sl] = (oe[:, :C_HEAD_DIM] / oe[:, C_HEAD_DIM:]).astype(BF16)


def _attention(q, k, v):
    kvw = C_KV_HEADS * C_HEAD_DIM
    ctx_tiles = CTX_LEN // TM
    return pl.pallas_call(
        _attn_kernel,
        grid=(BATCH, SEQ // TM),
        in_specs=[
            pl.BlockSpec((1, TM, D_MODEL), lambda b, j: (b, j + ctx_tiles, 0)),
            pl.BlockSpec((1, L_ALL, kvw), lambda b, j: (b, 0, 0)),
            pl.BlockSpec((1, L_ALL, 2 * kvw), lambda b, j: (b, 0, 0)),
        ],
        out_specs=pl.BlockSpec((1, TM, D_MODEL), lambda b, j: (b, j, 0)),
        out_shape=jax.ShapeDtypeStruct((BATCH, SEQ, D_MODEL), BF16),
        compiler_params=_params(("arbitrary", "arbitrary"), VMEM_LIMIT),
        name="attention",
    )(q, k, v)


def _mod_table(ada_layer):
    m = ada_layer.reshape(16, N_MOD, D_MODEL)
    m_lat = m[:BATCH]
    m_ctx = jnp.broadcast_to(m[BATCH], (BATCH, N_MOD, D_MODEL))
    t = jnp.stack([m_ctx, m_lat], axis=1)
    return jnp.pad(t, ((0, 0), (0, 0), (0, 8 - N_MOD), (0, 0)))


def kernel(x, c, ctx, c_ctx, ada_w, ada_b, norm_mix, norm_ffn, ab_w_in, ab_gate_w, ab_gate_b, ab_out_norm,
           ab_conv_w, ab_conv_b, ab_w_out, attn_w_qkv, attn_q_norm, attn_k_norm, attn_w_o, router_w,
           router_b, moe_w1, moe_b1, moe_w2, moe_b2, final_norm):
    cond = jnp.zeros((16, D_MODEL), F32).at[:BATCH].set(c).at[BATCH].set(c_ctx)
    ada = _ada(cond, ada_w, ada_b)
    mod0 = _mod_table(ada[0])
    mod1 = _mod_table(ada[1])
    seg_all = lambda j: jnp.minimum(j, 1)
    seg_lat = lambda j: 1
    n_tiles_all = L_ALL // TM
    n_tiles_lat = SEQ // TM
    fnorm = final_norm.reshape(1, D_MODEL)

    xall = jnp.concatenate([ctx, x], axis=1)
    w_in = ab_w_in[0]
    lr0 = 2 * A_HEADS * A_DK + A_HEADS * A_DV
    lr1 = lr0 + 2 * A_GATE_RANK
    w_main = jnp.concatenate([w_in[:, :lr0], w_in[:, lr1:]], axis=1).astype(BF16)
    w_lr = jnp.pad(w_in[:, lr0:lr1], ((0, 0), (0, LANES - 2 * A_GATE_RANK))).astype(BF16)
    p, lr = _proj_in(xall, mod0, norm_mix[0:1], w_main, w_lr)

    gw = ab_gate_w[0]
    gw_pad = jnp.zeros((2, LANES, A_HEADS * A_DK), F32)
    gw_pad = gw_pad.at[0, :A_GATE_RANK].set(gw[0]).at[1, A_GATE_RANK:2 * A_GATE_RANK].set(gw[1])
    cw = jnp.concatenate([ab_conv_w[0], ab_conv_b[0][None], jnp.zeros((4, B_WIDTH), F32)], axis=0)
    yg, yc = _gla_conv(p, lr, gw_pad.astype(BF16), ab_gate_b[0].reshape(2, 1, A_HEADS * A_DK),
                       ab_out_norm[0:1], cw)

    w_out = ab_w_out[0].astype(BF16)
    ngla = A_HEADS * A_DV
    xmid, h, top_e, top_g, cnt = _outproj(
        [yg, yc], [w_out[:ngla], w_out[ngla:]], xall, 0, n_tiles_all, mod0, seg_all,
        norm_ffn[0:1], router_w[0].T, router_b[0].reshape(N_EXPERTS, 1))
    x1 = _moe_ffn(h, top_e, top_g, cnt, 0, xmid, n_tiles_all, mod0, seg_all,
                  moe_w1, moe_b1, moe_w2, moe_b2, fnorm, False)

    cos, sin = _rope_tables()
    q, k, v = _qkv(x1, mod1, norm_mix[1:2], attn_w_qkv[0].astype(BF16), attn_q_norm[0:1],
                   attn_k_norm[0:1], cos, sin)
    o = _attention(q, k, v)
    xmid, h, top_e, top_g, cnt = _outproj(
        [o], [attn_w_o[0].astype(BF16)], x1, CTX_LEN // TM, n_tiles_lat, mod1, seg_lat,
        norm_ffn[1:2], router_w[1].T, router_b[1].reshape(N_EXPERTS, 1))
    return _moe_ffn(h, top_e, top_g, cnt, 1, xmid, n_tiles_lat, mod1, seg_lat,
                    moe_w1, moe_b1, moe_w2, moe_b2, fnorm, True)
```

```python
import functools

import jax
import jax.numpy as jnp
from jax import lax
from jax.experimental import pallas as pl
from jax.experimental.pallas import tpu as pltpu

F32 = jnp.float32
BF16 = jnp.bfloat16
I32 = jnp.int32

D_MODEL = 1024
BATCH = 8
SEQ = 2048
DEPTH = 2
GRID_W = 64
CTX_LEN = 256
L_ALL = CTX_LEN + SEQ
N_MOD = 6
EPS = 1e-6

A_DV = 128
A_HEADS = 4
A_DK = 64
A_GATE_RANK = 16
A_GATE_TAU = 16.0
GLA_CHUNK = 64
B_WIDTH = 512
AB_MAIN = 3072

C_HEAD_DIM = 128
C_HEADS = 8
C_KV_HEADS = 2
C_GROUP = 4
C_QKV = (C_HEADS + 2 * C_KV_HEADS) * C_HEAD_DIM
ROPE_THETA = 10000.0
LOG2_E = 1.4426950408889634

N_EXPERTS = 32
TOP_K = 4
SWIGLU_LIMIT = 7.0
SWIGLU_ALPHA = 1.702

LANES = 128
SUBLANES = 8
ROW_SLABS = D_MODEL // LANES
TM = 256
TILE_ROWS = TM * TOP_K
MOE_BM = 512
MOE_HALF = MOE_BM // 2
MOE_RING = 3
VMEM_LIMIT = 56 * 1024 * 1024

NT_DIMS = (((1,), (1,)), ((), ()))
TN_DIMS = (((0,), (0,)), ((), ()))


def _dot(a, b):
    return jnp.dot(a, b, preferred_element_type=F32)


def _dot_nt(a, b):
    return lax.dot_general(a, b, NT_DIMS, preferred_element_type=F32)


def _dot_tn(a, b):
    return lax.dot_general(a, b, TN_DIMS, preferred_element_type=F32)


def _params(sem, vmem=None):
    return pltpu.CompilerParams(dimension_semantics=sem, vmem_limit_bytes=vmem)


def _norm_mod(x, g, shift, scale):
    ms = jnp.mean(x * x, axis=-1, keepdims=True)
    y = x * lax.rsqrt(ms + EPS) * g
    return y * (1.0 + scale) + shift


def _sigmoid(x):
    return 1.0 / (1.0 + jnp.exp(-x))


def _ada_kernel(cond_ref, w_ref, b_ref, o_ref):
    c = cond_ref[...]
    s = (c * _sigmoid(c)).astype(BF16)
    o_ref[0] = _dot(s, w_ref[0].astype(BF16)) + b_ref[0]


def _ada(cond, ada_w, ada_b):
    tn = 1536
    n = N_MOD * D_MODEL
    return pl.pallas_call(
        _ada_kernel,
        grid=(DEPTH, n // tn),
        in_specs=[
            pl.BlockSpec((16, D_MODEL), lambda l, j: (0, 0)),
            pl.BlockSpec((1, D_MODEL, tn), lambda l, j: (l, 0, j)),
            pl.BlockSpec((1, 1, tn), lambda l, j: (l, 0, j)),
        ],
        out_specs=pl.BlockSpec((1, 16, tn), lambda l, j: (l, 0, j)),
        out_shape=jax.ShapeDtypeStruct((DEPTH, 16, n), F32),
        compiler_params=_params(("arbitrary", "arbitrary"), VMEM_LIMIT),
        name="ada",
    )(cond, ada_w, ada_b.reshape(DEPTH, 1, n))


def _proj_in_kernel(x_ref, mod_ref, g_ref, w_ref, wlr_ref, p_ref, lr_ref):
    mod = mod_ref[0, 0]
    h = _norm_mod(x_ref[0], g_ref[...], mod[0:1], mod[1:2]).astype(BF16)
    p_ref[0] = _dot(h, w_ref[...])
    lr_ref[0] = _dot(h, wlr_ref[...])


def _proj_in(xall, mod, g, w_main, w_lr):
    nj = L_ALL // TM
    return pl.pallas_call(
        _proj_in_kernel,
        grid=(BATCH, nj),
        in_specs=[
            pl.BlockSpec((1, TM, D_MODEL), lambda b, j: (b, j, 0)),
            pl.BlockSpec((1, 1, 8, D_MODEL), lambda b, j: (b, jnp.minimum(j, 1), 0, 0)),
            pl.BlockSpec((1, D_MODEL), lambda b, j: (0, 0)),
            pl.BlockSpec((D_MODEL, AB_MAIN), lambda b, j: (0, 0)),
            pl.BlockSpec((D_MODEL, LANES), lambda b, j: (0, 0)),
        ],
        out_specs=[
            pl.BlockSpec((1, TM, AB_MAIN), lambda b, j: (b, j, 0)),
            pl.BlockSpec((1, TM, LANES), lambda b, j: (b, j, 0)),
        ],
        out_shape=[
            jax.ShapeDtypeStruct((BATCH, L_ALL, AB_MAIN), F32),
            jax.ShapeDtypeStruct((BATCH, L_ALL, LANES), F32),
        ],
        compiler_params=_params(("arbitrary", "arbitrary"), VMEM_LIMIT),
        name="proj_in",
    )(xall, mod, g, w_main, w_lr)


def _log_sigmoid(z):
    return jnp.minimum(z, 0.0) - jnp.log1p(jnp.exp(-jnp.abs(z)))


def _gla_conv_kernel(q_ref, k_ref, v_ref, r_ref, gb_ref, gc_ref, u_ref, lr_ref, gw_ref, gbias_ref,
                     onorm_ref, cw_ref, yg_ref, yc_ref, of_ref, ob_ref, sf_ref, sb_ref, xs_ref):
    ch = GLA_CHUNK
    grp = TM
    n_grp = L_ALL // grp
    cpg = grp // ch
    row = lax.broadcasted_iota(I32, (grp, grp), 0)
    col = lax.broadcasted_iota(I32, (grp, grp), 1)
    same_chunk = (row // ch) == (col // ch)
    tri = (jnp.logical_and(same_chunk, row >= col), jnp.logical_and(same_chunk, col >= row))
    tri_bf = (tri[0].astype(BF16), tri[1].astype(BF16))
    lane = lax.broadcasted_iota(I32, (grp, LANES), 1)
    head_mask = (lane < A_DK, lane >= A_DK)
    chunk_of_row = lax.broadcasted_iota(I32, (grp, LANES), 0) // ch

    sf_ref[...] = jnp.zeros_like(sf_ref)
    sb_ref[...] = jnp.zeros_like(sb_ref)

    def chunk_rows(x, idx):
        return jnp.concatenate(
            [jnp.broadcast_to(x[c * ch + idx:c * ch + idx + 1], (ch, x.shape[1])) for c in range(cpg)], axis=0)

    def group(g, direction, s_ref, o_ref):
        rows = pl.ds(pl.multiple_of(g * grp, grp), grp)
        q = q_ref[0, rows, :] * (A_DK ** -0.5)
        k = k_ref[0, rows, :]
        v = v_ref[0, rows, :]
        lr = lr_ref[0, rows, :].astype(BF16)
        z = _dot(lr, gw_ref[direction]) + gbias_ref[direction]
        a = _log_sigmoid(z) * (1.0 / A_GATE_TAU)
        a_hi = a.astype(BF16)
        a_lo = (a - a_hi.astype(F32)).astype(BF16)
        cum = _dot(tri_bf[direction], a_hi) + _dot(tri_bf[direction], a_lo)
        if direction == 0:
            ref, last, order = chunk_rows(cum, ch // 2 - 1), chunk_rows(cum, ch - 1), range(cpg)
        else:
            ref, last, order = chunk_rows(cum, ch // 2), chunk_rows(cum, 0), range(cpg - 1, -1, -1)
        qe = q * jnp.exp(cum)
        qt = q * jnp.exp(cum - ref)
        kt = (k * jnp.exp(ref - cum)).astype(BF16)
        kl = k * jnp.exp(last - cum)
        dec = jnp.exp(last)
        for hh in range(2):
            m = head_mask[hh]
            vh = v[:, hh * A_DV:(hh + 1) * A_DV].astype(BF16)
            sc = _dot_nt(jnp.where(m, qt, 0.0).astype(BF16), kt)
            sc = jnp.where(tri[direction], sc, 0.0)
            o_intra = _dot(sc.astype(BF16), vh)
            qx = jnp.concatenate([jnp.where(jnp.logical_and(m, chunk_of_row == c), qe, 0.0) for c in range(cpg)],
                                 axis=1).astype(BF16)
            kx = jnp.concatenate([jnp.where(jnp.logical_and(m, chunk_of_row == c), kl, 0.0) for c in range(cpg)],
                                 axis=1).astype(BF16)
            kv = _dot(vh.T, kx)
            st = s_ref[hh]
            states = [None] * cpg
            for c in order:
                states[c] = st
                st = st * dec[c * ch:c * ch + 1] + kv[:, c * LANES:(c + 1) * LANES]
            s_ref[hh] = st
            o_inter = _dot_nt(qx, jnp.concatenate(states, axis=1).astype(BF16))
            o_ref[rows, hh * A_DV:(hh + 1) * A_DV] = o_intra + o_inter

    def body(i, carry):
        group(i, 0, sf_ref, of_ref)
        group(jnp.where(i == 0, 0, n_grp - i), 1, sb_ref, ob_ref)
        return carry

    lax.fori_loop(0, n_grp, body, 0)

    pad = SUBLANES
    xs_ref[0:pad, :] = jnp.zeros((pad, xs_ref.shape[1]), F32)
    xs_ref[pad + L_ALL:, :] = jnp.zeros((pad, xs_ref.shape[1]), F32)
    xs_ref[pad:pad + L_ALL, :] = gc_ref[0] * u_ref[0]

    cw = cw_ref[...]
    onorm = onorm_ref[...]
    trow = lax.broadcasted_iota(I32, (TM, 1), 0)
    for ti in range(L_ALL // TM):
        s0 = ti * TM
        rows = slice(s0, s0 + TM)
        o = of_ref[rows, :] + ob_ref[rows, :]
        r = r_ref[0, rows, :]
        parts = []
        for hh in range(2):
            oh = o[:, hh * A_DV:(hh + 1) * A_DV]
            ms = jnp.mean(oh * oh, axis=-1, keepdims=True)
            parts.append(oh * lax.rsqrt(ms + EPS) * onorm[:, hh * A_DV:(hh + 1) * A_DV])
        on = jnp.concatenate(parts, axis=1)
        yg_ref[0, rows, :] = (on * (r * _sigmoid(r))).astype(BF16)
        xm1 = xs_ref[pad + s0 - 1:pad + s0 - 1 + TM, :]
        x0 = xs_ref[pad + s0:pad + s0 + TM, :]
        xp1 = xs_ref[pad + s0 + 1:pad + s0 + 1 + TM, :]
        if s0 + TM == CTX_LEN:
            xp1 = jnp.where(trow == TM - 1, 0.0, xp1)
        if s0 == CTX_LEN:
            xm1 = jnp.where(trow == 0, 0.0, xm1)
        conv = cw[0:1] * xm1 + cw[1:2] * x0 + cw[2:3] * xp1 + cw[3:4]
        yc_ref[0, rows, :] = (gb_ref[0, rows, :] * conv).astype(BF16)


def _gla_conv(p, lr, gw_pad, gbias, onorm, cw):
    hw = 2 * A_DV

    def pspec(width, base):
        return pl.BlockSpec((1, L_ALL, width), lambda b, i: (b, 0, base + i))

    return pl.pallas_call(
        _gla_conv_kernel,
        grid=(BATCH, A_HEADS // 2),
        in_specs=[
            pspec(LANES, 0),
            pspec(LANES, 2),
            pspec(hw, 2),
            pspec(hw, 4),
            pspec(hw, 6),
            pspec(hw, 8),
            pspec(hw, 10),
            pl.BlockSpec((1, L_ALL, LANES), lambda b, i: (b, 0, 0)),
            pl.BlockSpec((2, LANES, LANES), lambda b, i: (0, 0, i)),
            pl.BlockSpec((2, 1, LANES), lambda b, i: (0, 0, i)),
            pl.BlockSpec((1, hw), lambda b, i: (0, i)),
            pl.BlockSpec((8, hw), lambda b, i: (0, i)),
        ],
        out_specs=[
            pl.BlockSpec((1, L_ALL, hw), lambda b, i: (b, 0, i)),
            pl.BlockSpec((1, L_ALL, hw), lambda b, i: (b, 0, i)),
        ],
        out_shape=[
            jax.ShapeDtypeStruct((BATCH, L_ALL, A_HEADS * A_DV), BF16),
            jax.ShapeDtypeStruct((BATCH, L_ALL, B_WIDTH), BF16),
        ],
        scratch_shapes=[
            pltpu.VMEM((L_ALL, hw), F32),
            pltpu.VMEM((L_ALL, hw), F32),
            pltpu.VMEM((2, A_DV, LANES), F32),
            pltpu.VMEM((2, A_DV, LANES), F32),
            pltpu.VMEM((L_ALL + 2 * SUBLANES, hw), F32),
        ],
        compiler_params=_params(("arbitrary", "arbitrary"), VMEM_LIMIT),
        name="gla_conv",
    )(p, p, p, p, p, p, p, lr, gw_pad, gbias, onorm, cw)


def _outproj_kernel(n_in, *refs):
    y_refs = refs[:n_in]
    w_refs = refs[n_in:2 * n_in]
    x_ref, mod_ref, g_ref, rwt_ref, rb_ref, xo_ref, h_ref, te_ref, tg_ref, cnt_ref = refs[2 * n_in:]
    mod = mod_ref[0, 0]
    wt = rwt_ref[...]
    wb = wt.astype(BF16)
    wl = (wt - wb.astype(F32)).astype(BF16)
    eidx = lax.broadcasted_iota(I32, (N_EXPERTS, LANES), 0)
    cnt = jnp.zeros((N_EXPERTS, 1), I32)
    for r0 in range(0, TM, LANES):
        rows = slice(r0, r0 + LANES)
        acc = _dot(y_refs[0][0, rows, :], w_refs[0][...])
        for i in range(1, n_in):
            acc = acc + _dot(y_refs[i][0, rows, :], w_refs[i][...])
        xn = x_ref[0, rows, :] + mod[2:3] * acc
        xo_ref[0, rows, :] = xn
        h = _norm_mod(xn, g_ref[...], mod[3:4], mod[4:5])
        h_ref[rows, :] = h

        hb = h.astype(BF16)
        hl = (h - hb.astype(F32)).astype(BF16)
        logits = _dot_nt(wb, hb) + _dot_nt(wb, hl) + _dot_nt(wl, hb) + rb_ref[...]
        cur = logits
        vals, idxs = [], []
        for _ in range(TOP_K):
            m = jnp.max(cur, axis=0, keepdims=True)
            sel = jnp.min(jnp.where(cur == m, eidx, N_EXPERTS), axis=0, keepdims=True)
            vals.append(m)
            idxs.append(sel)
            cur = jnp.where(eidx == sel, -jnp.inf, cur)
        ex = [jnp.exp(v - vals[0]) for v in vals]
        den = ex[0] + ex[1] + ex[2] + ex[3]
        zi = jnp.zeros_like(idxs[0])
        zf = jnp.zeros_like(den)
        te_ref[:, rows] = jnp.concatenate(idxs + [zi] * (8 - TOP_K), axis=0)
        tg_ref[:, rows] = jnp.concatenate([e / den for e in ex] + [zf] * (8 - TOP_K), axis=0)
        onehot = jnp.where(eidx == idxs[0], 1, 0)
        for k in range(1, TOP_K):
            onehot = onehot + jnp.where(eidx == idxs[k], 1, 0)
        cnt = cnt + jnp.sum(onehot, axis=1, keepdims=True)
    cnt_ref[0] = jnp.broadcast_to(cnt, (N_EXPERTS, LANES))


def _outproj(ys, ws, xres, x_tile_offset, n_tiles, mod, mod_seg, g, rwt, rb):
    n_in = len(ys)
    n_tok = BATCH * n_tiles * TM
    in_specs = []
    for y in ys:
        in_specs.append(pl.BlockSpec((1, TM, y.shape[2]), lambda b, j: (b, j, 0)))
    for w in ws:
        in_specs.append(pl.BlockSpec(w.shape, lambda b, j: (0, 0)))
    in_specs += [
        pl.BlockSpec((1, TM, D_MODEL), lambda b, j: (b, j + x_tile_offset, 0)),
        pl.BlockSpec((1, 1, 8, D_MODEL), lambda b, j: (b, mod_seg(j), 0, 0)),
        pl.BlockSpec((1, D_MODEL), lambda b, j: (0, 0)),
        pl.BlockSpec((N_EXPERTS, D_MODEL), lambda b, j: (0, 0)),
        pl.BlockSpec((N_EXPERTS, 1), lambda b, j: (0, 0)),
    ]
    return pl.pallas_call(
        functools.partial(_outproj_kernel, n_in),
        grid=(BATCH, n_tiles),
        in_specs=in_specs,
        out_specs=[
            pl.BlockSpec((1, TM, D_MODEL), lambda b, j: (b, j, 0)),
            pl.BlockSpec((TM, D_MODEL), lambda b, j: (b * n_tiles + j, 0)),
            pl.BlockSpec((8, TM), lambda b, j: (0, b * n_tiles + j)),
            pl.BlockSpec((8, TM), lambda b, j: (0, b * n_tiles + j)),
            pl.BlockSpec((1, N_EXPERTS, LANES), lambda b, j: (b * n_tiles + j, 0, 0)),
        ],
        out_shape=[
            jax.ShapeDtypeStruct((BATCH, n_tiles * TM, D_MODEL), F32),
            jax.ShapeDtypeStruct((n_tok, D_MODEL), F32),
            jax.ShapeDtypeStruct((8, n_tok), I32),
            jax.ShapeDtypeStruct((8, n_tok), F32),
            jax.ShapeDtypeStruct((BATCH * n_tiles, N_EXPERTS, LANES), I32),
        ],
        compiler_params=_params(("arbitrary", "arbitrary"), VMEM_LIMIT),
        name="outproj_router",
    )(*ys, *ws, xres, mod, g, rwt, rb)


def _moe_rows(n_tok):
    n_assign = n_tok * TOP_K
    n_blocks = -(-(n_assign + N_EXPERTS * (MOE_BM - 1)) // MOE_BM)
    return n_assign, n_blocks


def _route_tables(cnt, n_tok):
    _, n_blocks = _moe_rows(n_tok)
    counts = jnp.sum(cnt, axis=0)
    padded = (counts + MOE_BM - 1) // MOE_BM * MOE_BM
    pad_end = jnp.cumsum(padded)
    block_start = pad_end - padded
    seg_start = block_start[None, :] + jnp.cumsum(cnt, axis=0) - cnt
    blk_row = jnp.arange(n_blocks, dtype=I32) * MOE_BM
    block_e = jnp.minimum(jnp.sum((pad_end[None, :] <= blk_row[:, None]).astype(I32), axis=1), N_EXPERTS - 1)
    n_used = (pad_end[-1] // MOE_BM).reshape(1)
    block_valid = jnp.clip((block_start + counts)[block_e] - blk_row, 0, MOE_BM)
    eids = jnp.arange(N_EXPERTS, dtype=I32)
    later = jnp.where(jnp.logical_and(eids[None, :] > eids[:, None], counts[None, :] > 0), eids[None, :], N_EXPERTS)
    nxt = jnp.min(later, axis=1)
    next_e = jnp.where(nxt == N_EXPERTS, -1, nxt)[block_e]
    return dict(block_e=block_e.astype(I32), n_used=n_used.astype(I32),
                block_valid=block_valid.astype(I32), next_e=next_e.astype(I32),
                seg_start=seg_start.reshape(-1).astype(I32), cnt=cnt.reshape(-1).astype(I32),
                pad_start=(block_start + counts).astype(I32), pad_len=(padded - counts).astype(I32))


def _store_row_slabs(ref, val, n_rows):
    for s in range(ROW_SLABS):
        ref[pl.ds(s, n_rows, stride=ROW_SLABS), :] = val[:, s * LANES:(s + 1) * LANES]


def _load_row_slabs(ref, n_rows):
    return jnp.concatenate(
        [ref[pl.ds(s, n_rows, stride=ROW_SLABS), :] for s in range(ROW_SLABS)], axis=1)


def _slab_rows(start, n):
    return pl.ds(pl.multiple_of(start * ROW_SLABS, ROW_SLABS), n * ROW_SLABS)


def _local_positions(te):
    eidx = lax.broadcasted_iota(I32, (N_EXPERTS, TM), 0)
    hits = [te[k:k + 1] == eidx for k in range(TOP_K)]
    onehot = jnp.where(hits[0], 1.0, 0.0)
    for k in range(1, TOP_K):
        onehot = onehot + jnp.where(hits[k], 1.0, 0.0)
    mb = onehot.astype(BF16)
    trow = lax.broadcasted_iota(I32, (TM, TM), 0)
    tcol = lax.broadcasted_iota(I32, (TM, TM), 1)
    before = _dot(mb, (trow < tcol).astype(BF16))
    totals = _dot(mb, jnp.ones((TM, TM), BF16))
    erow = lax.broadcasted_iota(I32, (N_EXPERTS, N_EXPERTS), 0)
    ecol = lax.broadcasted_iota(I32, (N_EXPERTS, N_EXPERTS), 1)
    first = _dot((ecol < erow).astype(BF16), totals.astype(BF16))
    base = first + before
    return [jnp.sum(jnp.where(hits[k], base, 0.0), axis=0, keepdims=True).astype(I32) for k in range(TOP_K)]


def _dispatch_kernel(n_blocks, ss_ref, cn_ref, ps_ref, pl_ref, nu_ref, h_ref, te_ref, xs_hbm, lpos_ref,
                     sbuf, zbuf, sem):
    j = pl.program_id(0)
    n_tiles = pl.num_programs(0)

    def zero_fill(act):
        def per_expert(e, carry):
            n = pl_ref[e]

            @pl.when(n > 0)
            def _():
                act(pltpu.make_async_copy(zbuf.at[_slab_rows(0, n)], xs_hbm.at[_slab_rows(ps_ref[e], n)],
                                          sem.at[MOE_RING]))
            return carry
        lax.fori_loop(0, N_EXPERTS, per_expert, 0)

        def per_block(b, carry):
            act(pltpu.make_async_copy(zbuf, xs_hbm.at[_slab_rows(b * MOE_BM, MOE_BM)], sem.at[MOE_RING]))
            return carry
        lax.fori_loop(nu_ref[0], n_blocks, per_block, 0)

    @pl.when(j == 0)
    def _():
        zbuf[...] = jnp.zeros_like(zbuf)
        zero_fill(lambda cp: cp.start())

    lpos = _local_positions(te_ref[...])
    zi = jnp.zeros_like(lpos[0])
    lpos_ref[...] = jnp.concatenate(lpos + [zi] * (8 - TOP_K), axis=0)
    riota = lax.broadcasted_iota(I32, (TILE_ROWS, TM), 0)
    hit = riota == lpos[0]
    for k in range(1, TOP_K):
        hit = jnp.logical_or(hit, riota == lpos[k])
    perm = jnp.where(hit, 1.0, 0.0).astype(BF16)
    hb = h_ref[...].astype(BF16)
    slot = lax.rem(j, MOE_RING)
    buf = sbuf.at[slot]

    def rows_copy(sl):
        return pltpu.make_async_copy(sbuf.at[sl], xs_hbm.at[_slab_rows(0, TILE_ROWS)], sem.at[sl])

    @pl.when(j >= MOE_RING)
    def _():
        rows_copy(slot).wait()

    for s in range(0, ROW_SLABS, 2):
        xl = _dot(perm, hb[:, s * LANES:(s + 2) * LANES])
        buf[pl.ds(s, TILE_ROWS, stride=ROW_SLABS), :] = xl[:, :LANES]
        buf[pl.ds(s + 1, TILE_ROWS, stride=ROW_SLABS), :] = xl[:, LANES:]

    def segment(e, local):
        n = cn_ref[j * N_EXPERTS + e]

        @pl.when(n > 0)
        def _():
            pltpu.make_async_copy(buf.at[_slab_rows(local, n)],
                                  xs_hbm.at[_slab_rows(ss_ref[j * N_EXPERTS + e], n)], sem.at[slot]).start()
        return local + n
    lax.fori_loop(0, N_EXPERTS, segment, 0)

    @pl.when(j == n_tiles - 1)
    def _():
        for back in range(MOE_RING):
            rows_copy(lax.rem(j + MOE_RING - back, MOE_RING)).wait()
        zero_fill(lambda cp: cp.wait())


def _dispatch(h, top_e, tabs, n_tok):
    _, n_blocks = _moe_rows(n_tok)
    n_tiles = n_tok // TM
    grid_spec = pltpu.PrefetchScalarGridSpec(
        num_scalar_prefetch=5,
        grid=(n_tiles,),
        in_specs=[
            pl.BlockSpec((TM, D_MODEL), lambda j, *_: (j, 0)),
            pl.BlockSpec((8, TM), lambda j, *_: (0, j)),
        ],
        out_specs=[
            pl.BlockSpec(memory_space=pl.ANY),
            pl.BlockSpec((8, TM), lambda j, *_: (0, j)),
        ],
        scratch_shapes=[
            pltpu.VMEM((MOE_RING, TILE_ROWS * ROW_SLABS, LANES), F32),
            pltpu.VMEM((MOE_BM * ROW_SLABS, LANES), F32),
            pltpu.SemaphoreType.DMA((MOE_RING + 1,)),
        ],
    )
    return pl.pallas_call(
        functools.partial(_dispatch_kernel, n_blocks),
        grid_spec=grid_spec,
        out_shape=[
            jax.ShapeDtypeStruct((n_blocks * MOE_BM * ROW_SLABS, LANES), F32),
            jax.ShapeDtypeStruct((8, n_tok), I32),
        ],
        compiler_params=_params(("arbitrary",), VMEM_LIMIT),
        name="moe_dispatch",
    )(tabs["seg_start"], tabs["cnt"], tabs["pad_start"], tabs["pad_len"], tabs["n_used"], h, top_e)


def _moe_kernel(layer, be_ref, nu_ref, bv_ref, ne_ref, x_ref, b1_ref, b2_ref, w1_hbm, w2_hbm, y_ref,
                w1s, w2s, w1b, w2b, sem):
    i = pl.program_id(0)
    nu = nu_ref[0]

    def weight_copies(e):
        return (pltpu.make_async_copy(w1_hbm.at[layer, e], w1s, sem.at[0]),
                pltpu.make_async_copy(w2_hbm.at[layer, e], w2s, sem.at[1]))

    @pl.when(i < nu)
    def _():
        e = be_ref[i]

        @pl.when(i == 0)
        def _():
            for cp in weight_copies(e):
                cp.start()

        @pl.when(jnp.logical_or(i == 0, e != be_ref[jnp.maximum(i - 1, 0)]))
        def _():
            for cp in weight_copies(e):
                cp.wait()
            w1b[...] = w1s[...].astype(BF16)
            w2b[...] = w2s[...].astype(BF16)
            nxt = ne_ref[i]

            @pl.when(nxt >= 0)
            def _():
                for cp in weight_copies(nxt):
                    cp.start()

        def run(n_rows):
            slab_rows = pl.ds(0, n_rows * ROW_SLABS)
            x = _load_row_slabs(x_ref.at[slab_rows], n_rows).astype(BF16)
            h1 = _dot(x, w1b[...]) + b1_ref[0, 0]
            gate = jnp.minimum(h1[:, :D_MODEL], SWIGLU_LIMIT)
            up = jnp.clip(h1[:, D_MODEL:], -SWIGLU_LIMIT, SWIGLU_LIMIT)
            act = (up + 1.0) * gate * _sigmoid(SWIGLU_ALPHA * gate)
            y = _dot(act.astype(BF16), w2b[...]) + b2_ref[0, 0]
            _store_row_slabs(y_ref.at[slab_rows], y, n_rows)

        n_valid = bv_ref[i]

        @pl.when(n_valid > MOE_HALF)
        def _():
            run(MOE_BM)

        @pl.when(n_valid <= MOE_HALF)
        def _():
            run(MOE_HALF)
            y_ref[pl.ds(MOE_HALF * ROW_SLABS, MOE_HALF * ROW_SLABS), :] = jnp.zeros(
                (MOE_HALF * ROW_SLABS, LANES), F32)

    @pl.when(i >= nu)
    def _():
        y_ref[...] = jnp.zeros_like(y_ref)


def _moe(xs, tabs, layer, w1, b1, w2, b2, n_tok):
    _, n_blocks = _moe_rows(n_tok)
    d2 = 2 * D_MODEL
    blk = MOE_BM * ROW_SLABS
    grid_spec = pltpu.PrefetchScalarGridSpec(
        num_scalar_prefetch=4,
        grid=(n_blocks,),
        in_specs=[
            pl.BlockSpec((blk, LANES), lambda i, be, nu, *_: (jnp.minimum(i, nu[0] - 1), 0)),
            pl.BlockSpec((1, 1, 1, d2), lambda i, be, *_: (layer, be[i], 0, 0)),
            pl.BlockSpec((1, 1, 1, D_MODEL), lambda i, be, *_: (layer, be[i], 0, 0)),
            pl.BlockSpec(memory_space=pl.ANY),
            pl.BlockSpec(memory_space=pl.ANY),
        ],
        out_specs=pl.BlockSpec((blk, LANES), lambda i, *_: (i, 0)),
        scratch_shapes=[
            pltpu.VMEM((D_MODEL, d2), F32),
            pltpu.VMEM((D_MODEL, D_MODEL), F32),
            pltpu.VMEM((D_MODEL, d2), BF16),
            pltpu.VMEM((D_MODEL, D_MODEL), BF16),
            pltpu.SemaphoreType.DMA((2,)),
        ],
    )
    return pl.pallas_call(
        functools.partial(_moe_kernel, layer),
        grid_spec=grid_spec,
        out_shape=jax.ShapeDtypeStruct((n_blocks * blk, LANES), F32),
        compiler_params=_params(("arbitrary",), VMEM_LIMIT),
        name="moe_experts",
    )(tabs["block_e"], tabs["n_used"], tabs["block_valid"], tabs["next_e"], xs,
      b1.reshape(DEPTH, N_EXPERTS, 1, d2), b2.reshape(DEPTH, N_EXPERTS, 1, D_MODEL), w1, w2)


def _combine_kernel(final, n_tiles, ss_ref, cn_ref, lpos_ref, tg_ref, x_ref, mod_ref, g_ref, ys_hbm, o_ref,
                    cbuf, sem):
    t = pl.program_id(0) * n_tiles + pl.program_id(1)
    n_total = pl.num_programs(0) * n_tiles
    slot = lax.rem(t, MOE_RING)
    ahead = MOE_RING - 1

    def fetch(tile, sl):
        def segment(e, local):
            n = cn_ref[tile * N_EXPERTS + e]

            @pl.when(n > 0)
            def _():
                pltpu.make_async_copy(ys_hbm.at[_slab_rows(ss_ref[tile * N_EXPERTS + e], n)],
                                      cbuf.at[sl, _slab_rows(local, n)], sem.at[sl]).start()
            return local + n
        lax.fori_loop(0, N_EXPERTS, segment, 0)

    @pl.when(t == 0)
    def _():
        for first in range(ahead):
            fetch(first, first)

    @pl.when(t + ahead < n_total)
    def _():
        fetch(t + ahead, lax.rem(t + ahead, MOE_RING))

    pltpu.make_async_copy(ys_hbm.at[_slab_rows(0, TILE_ROWS)], cbuf.at[slot], sem.at[slot]).wait()
    yb = _load_row_slabs(cbuf.at[slot], TILE_ROWS).astype(BF16)

    lpos = lpos_ref[...]
    tg = tg_ref[...]
    riota = lax.broadcasted_iota(I32, (TILE_ROWS, TM), 0)
    gsel = jnp.where(riota == lpos[0:1], tg[0:1], 0.0)
    for k in range(1, TOP_K):
        gsel = gsel + jnp.where(riota == lpos[k:k + 1], tg[k:k + 1], 0.0)
    g_hi = gsel.astype(BF16)
    g_lo = (gsel - g_hi.astype(F32)).astype(BF16)
    f = _dot_tn(g_hi, yb) + _dot_tn(g_lo, yb)

    mod = mod_ref[0, 0]
    xn = x_ref[0] + mod[5:6] * f
    if final:
        ms = jnp.mean(xn * xn, axis=-1, keepdims=True)
        xn = xn * lax.rsqrt(ms + EPS) * g_ref[...]
    o_ref[0] = xn


def _combine(ys, tabs, lpos, top_g, xres, n_tiles, mod, mod_seg, g, final):
    grid_spec = pltpu.PrefetchScalarGridSpec(
        num_scalar_prefetch=2,
        grid=(BATCH, n_tiles),
        in_specs=[
            pl.BlockSpec((8, TM), lambda b, j, *_: (0, b * n_tiles + j)),
            pl.BlockSpec((8, TM), lambda b, j, *_: (0, b * n_tiles + j)),
            pl.BlockSpec((1, TM, D_MODEL), lambda b, j, *_: (b, j, 0)),
            pl.BlockSpec((1, 1, 8, D_MODEL), lambda b, j, *_: (b, mod_seg(j), 0, 0)),
            pl.BlockSpec((1, D_MODEL), lambda b, j, *_: (0, 0)),
            pl.BlockSpec(memory_space=pl.ANY),
        ],
        out_specs=pl.BlockSpec((1, TM, D_MODEL), lambda b, j, *_: (b, j, 0)),
        scratch_shapes=[
            pltpu.VMEM((MOE_RING, TILE_ROWS * ROW_SLABS, LANES), F32),
            pltpu.SemaphoreType.DMA((MOE_RING,)),
        ],
    )
    return pl.pallas_call(
        functools.partial(_combine_kernel, final, n_tiles),
        grid_spec=grid_spec,
        out_shape=jax.ShapeDtypeStruct((BATCH, n_tiles * TM, D_MODEL), F32),
        compiler_params=_params(("arbitrary", "arbitrary"), VMEM_LIMIT),
        name="moe_combine",
    )(tabs["seg_start"], tabs["cnt"], lpos, top_g, xres, mod, g, ys)


def _moe_ffn(h, top_e, top_g, cnt, layer, xres, n_tiles, mod, mod_seg, w1, b1, w2, b2, g, final):
    n_tok = BATCH * n_tiles * TM
    tabs = _route_tables(cnt[:, :, 0], n_tok)
    xs, lpos = _dispatch(h, top_e, tabs, n_tok)
    ys = _moe(xs, tabs, layer, w1, b1, w2, b2, n_tok)
    return _combine(ys, tabs, lpos, top_g, xres, n_tiles, mod, mod_seg, g, final)


def _qkv_kernel(x_ref, mod_ref, g_ref, w_ref, qn_ref, kn_ref, cos_ref, sin_ref, q_ref, k_ref, v_ref):
    mod = mod_ref[0, 0]
    h = _norm_mod(x_ref[0], g_ref[...], mod[0:1], mod[1:2]).astype(BF16)
    qkv = _dot(h, w_ref[...])
    cos = cos_ref[...]
    sin = sin_ref[...]
    lane = lax.broadcasted_iota(I32, (TM, C_HEAD_DIM), 1)
    first_half = (lane % (C_HEAD_DIM // 2)) < (C_HEAD_DIM // 4)
    quarter = C_HEAD_DIM // 4

    def head(xh, gn):
        ms = jnp.mean(xh * xh, axis=-1, keepdims=True)
        y = xh * lax.rsqrt(ms + EPS) * gn
        partner = jnp.where(first_half, pltpu.roll(y, C_HEAD_DIM - quarter, 1), pltpu.roll(y, quarter, 1))
        return y * cos + partner * sin

    qn = qn_ref[...]
    kn = kn_ref[...]
    for hq in range(C_HEADS):
        sl = slice(hq * C_HEAD_DIM, (hq + 1) * C_HEAD_DIM)
        q_ref[0, :, sl] = head(qkv[:, sl], qn).astype(BF16)
    for hk in range(C_KV_HEADS):
        src = slice((C_HEADS + hk) * C_HEAD_DIM, (C_HEADS + hk + 1) * C_HEAD_DIM)
        k_ref[0, :, hk * C_HEAD_DIM:(hk + 1) * C_HEAD_DIM] = head(qkv[:, src], kn).astype(BF16)
    ones = jnp.ones((TM, C_HEAD_DIM), BF16)
    for hk in range(C_KV_HEADS):
        src = slice((C_HEADS + C_KV_HEADS + hk) * C_HEAD_DIM, (C_HEADS + C_KV_HEADS + hk + 1) * C_HEAD_DIM)
        v_ref[0, :, 2 * hk * C_HEAD_DIM:(2 * hk + 1) * C_HEAD_DIM] = qkv[:, src].astype(BF16)
        v_ref[0, :, (2 * hk + 1) * C_HEAD_DIM:(2 * hk + 2) * C_HEAD_DIM] = ones


def _qkv(xall, mod, g, w, qn, kn, cos, sin):
    nj = L_ALL // TM
    kvw = C_KV_HEADS * C_HEAD_DIM
    return pl.pallas_call(
        _qkv_kernel,
        grid=(BATCH, nj),
        in_specs=[
            pl.BlockSpec((1, TM, D_MODEL), lambda b, j: (b, j, 0)),
            pl.BlockSpec((1, 1, 8, D_MODEL), lambda b, j: (b, jnp.minimum(j, 1), 0, 0)),
            pl.BlockSpec((1, D_MODEL), lambda b, j: (0, 0)),
            pl.BlockSpec((D_MODEL, C_QKV), lambda b, j: (0, 0)),
            pl.BlockSpec((1, C_HEAD_DIM), lambda b, j: (0, 0)),
            pl.BlockSpec((1, C_HEAD_DIM), lambda b, j: (0, 0)),
            pl.BlockSpec((TM, C_HEAD_DIM), lambda b, j: (j, 0)),
            pl.BlockSpec((TM, C_HEAD_DIM), lambda b, j: (j, 0)),
        ],
        out_specs=[
            pl.BlockSpec((1, TM, D_MODEL), lambda b, j: (b, j, 0)),
            pl.BlockSpec((1, TM, kvw), lambda b, j: (b, j, 0)),
            pl.BlockSpec((1, TM, 2 * kvw), lambda b, j: (b, j, 0)),
        ],
        out_shape=[
            jax.ShapeDtypeStruct((BATCH, L_ALL, D_MODEL), BF16),
            jax.ShapeDtypeStruct((BATCH, L_ALL, kvw), BF16),
            jax.ShapeDtypeStruct((BATCH, L_ALL, 2 * kvw), BF16),
        ],
        compiler_params=_params(("arbitrary", "arbitrary"), VMEM_LIMIT),
        name="qkv_rope",
    )(xall, mod, g, w, qn, kn, cos, sin)


def _rope_tables():
    rows = SEQ // GRID_W
    row = jnp.repeat(jnp.arange(rows), GRID_W).astype(F32)
    col = jnp.tile(jnp.arange(GRID_W), rows).astype(F32)
    half = C_HEAD_DIM // 2
    inv_freq = ROPE_THETA ** (-jnp.arange(0, half, 2, dtype=F32) / half)
    ar = row[:, None] * inv_freq
    ac = col[:, None] * inv_freq
    cos = jnp.concatenate([jnp.cos(ar), jnp.cos(ar), jnp.cos(ac), jnp.cos(ac)], axis=-1)
    sin = jnp.concatenate([-jnp.sin(ar), jnp.sin(ar), -jnp.sin(ac), jnp.sin(ac)], axis=-1)
    cos = jnp.concatenate([jnp.ones((CTX_LEN, C_HEAD_DIM), F32), cos], axis=0)
    sin = jnp.concatenate([jnp.zeros((CTX_LEN, C_HEAD_DIM), F32), sin], axis=0)
    return cos, sin


def _attn_kernel(q_ref, k_ref, v_ref, o_ref):
    c = (C_HEAD_DIM ** -0.5) * LOG2_E
    for g in range(C_KV_HEADS):
        kg = k_ref[0, :, g * C_HEAD_DIM:(g + 1) * C_HEAD_DIM]
        vg = v_ref[0, :, 2 * g * C_HEAD_DIM:(2 * g + 2) * C_HEAD_DIM]
        for hh in range(C_GROUP):
            sl = slice((g * C_GROUP + hh) * C_HEAD_DIM, (g * C_GROUP + hh + 1) * C_HEAD_DIM)
            s = _dot_nt(q_ref[0, :, sl], kg)
            m = jnp.max(s, axis=-1, keepdims=True)
            p = jnp.exp2((s - m) * c).astype(BF16)
            oe = _dot(p, vg)
            o_ref[0, :, sl] = (oe[:, :C_HEAD_DIM] / oe[:, C_HEAD_DIM:]).astype(BF16)


def _attention(q, k, v):
    kvw = C_KV_HEADS * C_HEAD_DIM
    ctx_tiles = CTX_LEN // TM
    return pl.pallas_call(
        _attn_kernel,
        grid=(BATCH, SEQ // TM),
        in_specs=[
            pl.BlockSpec((1, TM, D_MODEL), lambda b, j: (b, j + ctx_tiles, 0)),
            pl.BlockSpec((1, L_ALL, kvw), lambda b, j: (b, 0, 0)),
            pl.BlockSpec((1, L_ALL, 2 * kvw), lambda b, j: (b, 0, 0)),
        ],
        out_specs=pl.BlockSpec((1, TM, D_MODEL), lambda b, j: (b, j, 0)),
        out_shape=jax.ShapeDtypeStruct((BATCH, SEQ, D_MODEL), BF16),
        compiler_params=_params(("arbitrary", "arbitrary"), VMEM_LIMIT),
        name="attention",
    )(q, k, v)


def _mod_table(ada_layer):
    m = ada_layer.reshape(16, N_MOD, D_MODEL)
    m_lat = m[:BATCH]
    m_ctx = jnp.broadcast_to(m[BATCH], (BATCH, N_MOD, D_MODEL))
    t = jnp.stack([m_ctx, m_lat], axis=1)
    return jnp.pad(t, ((0, 0), (0, 0), (0, 8 - N_MOD), (0, 0)))


def kernel(x, c, ctx, c_ctx, ada_w, ada_b, norm_mix, norm_ffn, ab_w_in, ab_gate_w, ab_gate_b, ab_out_norm,
           ab_conv_w, ab_conv_b, ab_w_out, attn_w_qkv, attn_q_norm, attn_k_norm, attn_w_o, router_w,
           router_b, moe_w1, moe_b1, moe_w2, moe_b2, final_norm):
    cond = jnp.zeros((16, D_MODEL), F32).at[:BATCH].set(c).at[BATCH].set(c_ctx)
    ada = _ada(cond, ada_w, ada_b)
    mod0 = _mod_table(ada[0])
    mod1 = _mod_table(ada[1])
    seg_all = lambda j: jnp.minimum(j, 1)
    seg_lat = lambda j: 1
    n_tiles_all = L_ALL // TM
    n_tiles_lat = SEQ // TM
    fnorm = final_norm.reshape(1, D_MODEL)

    xall = jnp.concatenate([ctx, x], axis=1)
    w_in = ab_w_in[0]
    lr0 = 2 * A_HEADS * A_DK + A_HEADS * A_DV
    lr1 = lr0 + 2 * A_GATE_RANK
    w_main = jnp.concatenate([w_in[:, :lr0], w_in[:, lr1:]], axis=1).astype(BF16)
    w_lr = jnp.pad(w_in[:, lr0:lr1], ((0, 0), (0, LANES - 2 * A_GATE_RANK))).astype(BF16)
    p, lr = _proj_in(xall, mod0, norm_mix[0:1], w_main, w_lr)

    gw = ab_gate_w[0]
    gw_pad = jnp.zeros((2, LANES, A_HEADS * A_DK), F32)
    gw_pad = gw_pad.at[0, :A_GATE_RANK].set(gw[0]).at[1, A_GATE_RANK:2 * A_GATE_RANK].set(gw[1])
    cw = jnp.concatenate([ab_conv_w[0], ab_conv_b[0][None], jnp.zeros((4, B_WIDTH), F32)], axis=0)
    yg, yc = _gla_conv(p, lr, gw_pad.astype(BF16), ab_gate_b[0].reshape(2, 1, A_HEADS * A_DK),
                       ab_out_norm[0:1], cw)

    w_out = ab_w_out[0].astype(BF16)
    ngla = A_HEADS * A_DV
    xmid, h, top_e, top_g, cnt = _outproj(
        [yg, yc], [w_out[:ngla], w_out[ngla:]], xall, 0, n_tiles_all, mod0, seg_all,
        norm_ffn[0:1], router_w[0].T, router_b[0].reshape(N_EXPERTS, 1))
    x1 = _moe_ffn(h, top_e, top_g, cnt, 0, xmid, n_tiles_all, mod0, seg_all,
                  moe_w1, moe_b1, moe_w2, moe_b2, fnorm, False)

    cos, sin = _rope_tables()
    q, k, v = _qkv(x1, mod1, norm_mix[1:2], attn_w_qkv[0].astype(BF16), attn_q_norm[0:1],
                   attn_k_norm[0:1], cos, sin)
    o = _attention(q, k, v)
    xmid, h, top_e, top_g, cnt = _outproj(
        [o], [attn_w_o[0].astype(BF16)], x1, CTX_LEN // TM, n_tiles_lat, mod1, seg_lat,
        norm_ffn[1:2], router_w[1].T, router_b[1].reshape(N_EXPERTS, 1))
    return _moe_ffn(h, top_e, top_g, cnt, 1, xmid, n_tiles_lat, mod1, seg_lat,
                    moe_w1, moe_b1, moe_w2, moe_b2, fnorm, True)
```

```python
import functools

import jax
import jax.numpy as jnp
from jax import lax
from jax.experimental import pallas as pl
from jax.experimental.pallas import tpu as pltpu

F32 = jnp.float32
BF16 = jnp.bfloat16
I32 = jnp.int32

D_MODEL = 1024
BATCH = 8
SEQ = 2048
DEPTH = 2
GRID_W = 64
CTX_LEN = 256
L_ALL = CTX_LEN + SEQ
N_MOD = 6
EPS = 1e-6

A_DV = 128
A_HEADS = 4
A_DK = 64
A_GATE_RANK = 16
A_GATE_TAU = 16.0
GLA_CHUNK = 64
B_WIDTH = 512
AB_MAIN = 3072

C_HEAD_DIM = 128
C_HEADS = 8
C_KV_HEADS = 2
C_GROUP = 4
C_QKV = (C_HEADS + 2 * C_KV_HEADS) * C_HEAD_DIM
ROPE_THETA = 10000.0
LOG2_E = 1.4426950408889634

N_EXPERTS = 32
TOP_K = 4
SWIGLU_LIMIT = 7.0
SWIGLU_ALPHA = 1.702

LANES = 128
SUBLANES = 8
ROW_SLABS = D_MODEL // LANES
TM = 256
TILE_ROWS = TM * TOP_K
MOE_BM = 512
MOE_HALF = MOE_BM // 2
MOE_RING = 3
VMEM_LIMIT = 56 * 1024 * 1024

NT_DIMS = (((1,), (1,)), ((), ()))
TN_DIMS = (((0,), (0,)), ((), ()))


def _dot(a, b):
    return jnp.dot(a, b, preferred_element_type=F32)


def _dot_nt(a, b):
    return lax.dot_general(a, b, NT_DIMS, preferred_element_type=F32)


def _dot_tn(a, b):
    return lax.dot_general(a, b, TN_DIMS, preferred_element_type=F32)


def _params(sem, vmem=None):
    return pltpu.CompilerParams(dimension_semantics=sem, vmem_limit_bytes=vmem)


def _norm_mod(x, g, shift, scale):
    ms = jnp.mean(x * x, axis=-1, keepdims=True)
    y = x * lax.rsqrt(ms + EPS) * g
    return y * (1.0 + scale) + shift


def _sigmoid(x):
    return 1.0 / (1.0 + jnp.exp(-x))


def _ada_kernel(cond_ref, w_ref, b_ref, o_ref):
    c = cond_ref[...]
    s = (c * _sigmoid(c)).astype(BF16)
    o_ref[0] = _dot(s, w_ref[0].astype(BF16)) + b_ref[0]


def _ada(cond, ada_w, ada_b):
    tn = 1536
    n = N_MOD * D_MODEL
    return pl.pallas_call(
        _ada_kernel,
        grid=(DEPTH, n // tn),
        in_specs=[
            pl.BlockSpec((16, D_MODEL), lambda l, j: (0, 0)),
            pl.BlockSpec((1, D_MODEL, tn), lambda l, j: (l, 0, j)),
            pl.BlockSpec((1, 1, tn), lambda l, j: (l, 0, j)),
        ],
        out_specs=pl.BlockSpec((1, 16, tn), lambda l, j: (l, 0, j)),
        out_shape=jax.ShapeDtypeStruct((DEPTH, 16, n), F32),
        compiler_params=_params(("arbitrary", "arbitrary"), VMEM_LIMIT),
        name="ada",
    )(cond, ada_w, ada_b.reshape(DEPTH, 1, n))


def _ctx_or_latent(ctx_ref, x_ref):
    return jnp.where(pl.program_id(1) == 0, ctx_ref[0], x_ref[0])


def _ctx_latent_specs():
    return [pl.BlockSpec((1, TM, D_MODEL), lambda b, j: (b, 0, 0)),
            pl.BlockSpec((1, TM, D_MODEL), lambda b, j: (b, jnp.maximum(j - 1, 0), 0))]


def _proj_in_kernel(ctx_ref, x_ref, mod_ref, g_ref, w_ref, wlr_ref, p_ref, lr_ref):
    mod = mod_ref[0, 0]
    h = _norm_mod(_ctx_or_latent(ctx_ref, x_ref), g_ref[...], mod[0:1], mod[1:2]).astype(BF16)
    p_ref[0] = _dot(h, w_ref[...]).astype(BF16)
    lr_ref[0] = _dot(h, wlr_ref[...]).astype(BF16)


def _proj_in(ctx, x, mod, g, w_main, w_lr):
    nj = L_ALL // TM
    return pl.pallas_call(
        _proj_in_kernel,
        grid=(BATCH, nj),
        in_specs=_ctx_latent_specs() + [
            pl.BlockSpec((1, 1, 8, D_MODEL), lambda b, j: (b, jnp.minimum(j, 1), 0, 0)),
            pl.BlockSpec((1, D_MODEL), lambda b, j: (0, 0)),
            pl.BlockSpec((D_MODEL, AB_MAIN), lambda b, j: (0, 0)),
            pl.BlockSpec((D_MODEL, LANES), lambda b, j: (0, 0)),
        ],
        out_specs=[
            pl.BlockSpec((1, TM, AB_MAIN), lambda b, j: (b, j, 0)),
            pl.BlockSpec((1, TM, LANES), lambda b, j: (b, j, 0)),
        ],
        out_shape=[
            jax.ShapeDtypeStruct((BATCH, L_ALL, AB_MAIN), BF16),
            jax.ShapeDtypeStruct((BATCH, L_ALL, LANES), BF16),
        ],
        compiler_params=_params(("arbitrary", "arbitrary"), VMEM_LIMIT),
        name="proj_in",
    )(ctx, x, mod, g, w_main, w_lr)


def _log_sigmoid(z):
    return jnp.minimum(z, 0.0) - jnp.log1p(jnp.exp(-jnp.abs(z)))


def _gla_conv_kernel(q_ref, k_ref, v_ref, r_ref, gb_ref, gc_ref, u_ref, lr_ref, gw_ref, gbias_ref,
                     onorm_ref, cw_ref, yg_ref, yc_ref, of_ref, ob_ref, sf_ref, sb_ref, xs_ref):
    ch = GLA_CHUNK
    grp = TM
    n_grp = L_ALL // grp
    cpg = grp // ch
    row = lax.broadcasted_iota(I32, (grp, grp), 0)
    col = lax.broadcasted_iota(I32, (grp, grp), 1)
    same_chunk = (row // ch) == (col // ch)
    tri = (jnp.logical_and(same_chunk, row >= col), jnp.logical_and(same_chunk, col >= row))
    tri_bf = (tri[0].astype(BF16), tri[1].astype(BF16))
    lane = lax.broadcasted_iota(I32, (grp, LANES), 1)
    head_mask = (lane < A_DK, lane >= A_DK)
    chunk_of_row = lax.broadcasted_iota(I32, (grp, LANES), 0) // ch

    sf_ref[...] = jnp.zeros_like(sf_ref)
    sb_ref[...] = jnp.zeros_like(sb_ref)

    def chunk_rows(x, idx):
        return jnp.concatenate(
            [jnp.broadcast_to(x[c * ch + idx:c * ch + idx + 1], (ch, x.shape[1])) for c in range(cpg)], axis=0)

    def group(g, direction, s_ref, o_ref):
        rows = pl.ds(pl.multiple_of(g * grp, grp), grp)
        q = q_ref[0, rows, :].astype(F32) * (A_DK ** -0.5)
        k = k_ref[0, rows, :].astype(F32)
        v = v_ref[0, rows, :]
        lr = lr_ref[0, rows, :].astype(BF16)
        z = _dot(lr, gw_ref[direction]) + gbias_ref[direction]
        a = _log_sigmoid(z) * (1.0 / A_GATE_TAU)
        a_hi = a.astype(BF16)
        a_lo = (a - a_hi.astype(F32)).astype(BF16)
        cum = _dot(tri_bf[direction], a_hi) + _dot(tri_bf[direction], a_lo)
        if direction == 0:
            ref, last, order = chunk_rows(cum, ch // 2 - 1), chunk_rows(cum, ch - 1), range(cpg)
        else:
            ref, last, order = chunk_rows(cum, ch // 2), chunk_rows(cum, 0), range(cpg - 1, -1, -1)
        qe = q * jnp.exp(cum)
        qt = q * jnp.exp(cum - ref)
        kt = (k * jnp.exp(ref - cum)).astype(BF16)
        kl = k * jnp.exp(last - cum)
        dec = jnp.exp(last)
        for hh in range(2):
            m = head_mask[hh]
            vh = v[:, hh * A_DV:(hh + 1) * A_DV].astype(BF16)
            sc = _dot_nt(jnp.where(m, qt, 0.0).astype(BF16), kt)
            sc = jnp.where(tri[direction], sc, 0.0)
            o_intra = _dot(sc.astype(BF16), vh)
            qx = jnp.concatenate([jnp.where(jnp.logical_and(m, chunk_of_row == c), qe, 0.0) for c in range(cpg)],
                                 axis=1).astype(BF16)
            kx = jnp.concatenate([jnp.where(jnp.logical_and(m, chunk_of_row == c), kl, 0.0) for c in range(cpg)],
                                 axis=1).astype(BF16)
            kv = _dot(vh.T, kx)
            st = s_ref[hh]
            states = [None] * cpg
            for c in order:
                states[c] = st
                st = st * dec[c * ch:c * ch + 1] + kv[:, c * LANES:(c + 1) * LANES]
            s_ref[hh] = st
            o_inter = _dot_nt(qx, jnp.concatenate(states, axis=1).astype(BF16))
            o_ref[rows, hh * A_DV:(hh + 1) * A_DV] = o_intra + o_inter

    def body(i, carry):
        group(i, 0, sf_ref, of_ref)
        group(jnp.where(i == 0, 0, n_grp - i), 1, sb_ref, ob_ref)
        return carry

    lax.fori_loop(0, n_grp, body, 0)

    pad = SUBLANES
    xs_ref[0:pad, :] = jnp.zeros((pad, xs_ref.shape[1]), F32)
    xs_ref[pad + L_ALL:, :] = jnp.zeros((pad, xs_ref.shape[1]), F32)
    xs_ref[pad:pad + L_ALL, :] = gc_ref[0].astype(F32) * u_ref[0].astype(F32)

    cw = cw_ref[...]
    onorm = onorm_ref[...]
    trow = lax.broadcasted_iota(I32, (TM, 1), 0)
    for ti in range(L_ALL // TM):
        s0 = ti * TM
        rows = slice(s0, s0 + TM)
        o = of_ref[rows, :] + ob_ref[rows, :]
        r = r_ref[0, rows, :].astype(F32)
        parts = []
        for hh in range(2):
            oh = o[:, hh * A_DV:(hh + 1) * A_DV]
            ms = jnp.mean(oh * oh, axis=-1, keepdims=True)
            parts.append(oh * lax.rsqrt(ms + EPS) * onorm[:, hh * A_DV:(hh + 1) * A_DV])
        on = jnp.concatenate(parts, axis=1)
        yg_ref[0, rows, :] = (on * (r * _sigmoid(r))).astype(BF16)
        xm1 = xs_ref[pad + s0 - 1:pad + s0 - 1 + TM, :]
        x0 = xs_ref[pad + s0:pad + s0 + TM, :]
        xp1 = xs_ref[pad + s0 + 1:pad + s0 + 1 + TM, :]
        if s0 + TM == CTX_LEN:
            xp1 = jnp.where(trow == TM - 1, 0.0, xp1)
        if s0 == CTX_LEN:
            xm1 = jnp.where(trow == 0, 0.0, xm1)
        conv = cw[0:1] * xm1 + cw[1:2] * x0 + cw[2:3] * xp1 + cw[3:4]
        yc_ref[0, rows, :] = (gb_ref[0, rows, :].astype(F32) * conv).astype(BF16)


def _gla_conv(p, lr, gw_pad, gbias, onorm, cw):
    hw = 2 * A_DV

    def pspec(width, base):
        return pl.BlockSpec((1, L_ALL, width), lambda b, i: (b, 0, base + i))

    return pl.pallas_call(
        _gla_conv_kernel,
        grid=(BATCH, A_HEADS // 2),
        in_specs=[
            pspec(LANES, 0),
            pspec(LANES, 2),
            pspec(hw, 2),
            pspec(hw, 4),
            pspec(hw, 6),
            pspec(hw, 8),
            pspec(hw, 10),
            pl.BlockSpec((1, L_ALL, LANES), lambda b, i: (b, 0, 0)),
            pl.BlockSpec((2, LANES, LANES), lambda b, i: (0, 0, i)),
            pl.BlockSpec((2, 1, LANES), lambda b, i: (0, 0, i)),
            pl.BlockSpec((1, hw), lambda b, i: (0, i)),
            pl.BlockSpec((8, hw), lambda b, i: (0, i)),
        ],
        out_specs=[
            pl.BlockSpec((1, L_ALL, hw), lambda b, i: (b, 0, i)),
            pl.BlockSpec((1, L_ALL, hw), lambda b, i: (b, 0, i)),
        ],
        out_shape=[
            jax.ShapeDtypeStruct((BATCH, L_ALL, A_HEADS * A_DV), BF16),
            jax.ShapeDtypeStruct((BATCH, L_ALL, B_WIDTH), BF16),
        ],
        scratch_shapes=[
            pltpu.VMEM((L_ALL, hw), F32),
            pltpu.VMEM((L_ALL, hw), F32),
            pltpu.VMEM((2, A_DV, LANES), F32),
            pltpu.VMEM((2, A_DV, LANES), F32),
            pltpu.VMEM((L_ALL + 2 * SUBLANES, hw), F32),
        ],
        compiler_params=_params(("arbitrary", "arbitrary"), VMEM_LIMIT),
        name="gla_conv",
    )(p, p, p, p, p, p, p, lr, gw_pad, gbias, onorm, cw)


def _outproj_kernel(n_in, n_res, *refs):
    y_refs = refs[:n_in]
    w_refs = refs[n_in:2 * n_in]
    res_refs = refs[2 * n_in:2 * n_in + n_res]
    mod_ref, g_ref, rwt_ref, rb_ref, xo_ref, h_ref, te_ref, tg_ref, cnt_ref = refs[2 * n_in + n_res:]
    is_ctx = pl.program_id(1) == 0
    mod = mod_ref[0, 0]
    wt = rwt_ref[...]
    wb = wt.astype(BF16)
    wl = (wt - wb.astype(F32)).astype(BF16)
    eidx = lax.broadcasted_iota(I32, (N_EXPERTS, LANES), 0)
    cnt = jnp.zeros((N_EXPERTS, 1), I32)
    for r0 in range(0, TM, LANES):
        rows = slice(r0, r0 + LANES)
        acc = _dot(y_refs[0][0, rows, :], w_refs[0][...])
        for i in range(1, n_in):
            acc = acc + _dot(y_refs[i][0, rows, :], w_refs[i][...])
        xres = res_refs[0][0, rows, :]
        if n_res == 2:
            xres = jnp.where(is_ctx, xres, res_refs[1][0, rows, :])
        xn = xres + mod[2:3] * acc
        xo_ref[0, rows, :] = xn
        h = _norm_mod(xn, g_ref[...], mod[3:4], mod[4:5])
        h_ref[rows, :] = h.astype(BF16)

        hb = h.astype(BF16)
        hl = (h - hb.astype(F32)).astype(BF16)
        logits = _dot_nt(wb, hb) + _dot_nt(wb, hl) + _dot_nt(wl, hb) + rb_ref[...]
        cur = logits
        vals, idxs = [], []
        for _ in range(TOP_K):
            m = jnp.max(cur, axis=0, keepdims=True)
            sel = jnp.min(jnp.where(cur == m, eidx, N_EXPERTS), axis=0, keepdims=True)
            vals.append(m)
            idxs.append(sel)
            cur = jnp.where(eidx == sel, -jnp.inf, cur)
        ex = [jnp.exp(v - vals[0]) for v in vals]
        den = ex[0] + ex[1] + ex[2] + ex[3]
        zi = jnp.zeros_like(idxs[0])
        zf = jnp.zeros_like(den)
        te_ref[:, rows] = jnp.concatenate(idxs + [zi] * (8 - TOP_K), axis=0)
        tg_ref[:, rows] = jnp.concatenate([e / den for e in ex] + [zf] * (8 - TOP_K), axis=0)
        onehot = jnp.where(eidx == idxs[0], 1, 0)
        for k in range(1, TOP_K):
            onehot = onehot + jnp.where(eidx == idxs[k], 1, 0)
        cnt = cnt + jnp.sum(onehot, axis=1, keepdims=True)
    cnt_ref[0] = jnp.broadcast_to(cnt, (N_EXPERTS, LANES))


def _outproj(ys, ws, res, res_specs, n_tiles, mod, mod_seg, g, rwt, rb):
    n_in = len(ys)
    n_tok = BATCH * n_tiles * TM
    in_specs = []
    for y in ys:
        in_specs.append(pl.BlockSpec((1, TM, y.shape[2]), lambda b, j: (b, j, 0)))
    for w in ws:
        in_specs.append(pl.BlockSpec(w.shape, lambda b, j: (0, 0)))
    in_specs += list(res_specs) + [
        pl.BlockSpec((1, 1, 8, D_MODEL), lambda b, j: (b, mod_seg(j), 0, 0)),
        pl.BlockSpec((1, D_MODEL), lambda b, j: (0, 0)),
        pl.BlockSpec((N_EXPERTS, D_MODEL), lambda b, j: (0, 0)),
        pl.BlockSpec((N_EXPERTS, 1), lambda b, j: (0, 0)),
    ]
    return pl.pallas_call(
        functools.partial(_outproj_kernel, n_in, len(res)),
        grid=(BATCH, n_tiles),
        in_specs=in_specs,
        out_specs=[
            pl.BlockSpec((1, TM, D_MODEL), lambda b, j: (b, j, 0)),
            pl.BlockSpec((TM, D_MODEL), lambda b, j: (b * n_tiles + j, 0)),
            pl.BlockSpec((8, TM), lambda b, j: (0, b * n_tiles + j)),
            pl.BlockSpec((8, TM), lambda b, j: (0, b * n_tiles + j)),
            pl.BlockSpec((1, N_EXPERTS, LANES), lambda b, j: (b * n_tiles + j, 0, 0)),
        ],
        out_shape=[
            jax.ShapeDtypeStruct((BATCH, n_tiles * TM, D_MODEL), F32),
            jax.ShapeDtypeStruct((n_tok, D_MODEL), BF16),
            jax.ShapeDtypeStruct((8, n_tok), I32),
            jax.ShapeDtypeStruct((8, n_tok), F32),
            jax.ShapeDtypeStruct((BATCH * n_tiles, N_EXPERTS, LANES), I32),
        ],
        compiler_params=_params(("arbitrary", "arbitrary"), VMEM_LIMIT),
        name="outproj_router",
    )(*ys, *ws, *res, mod, g, rwt, rb)


def _moe_rows(n_tok):
    n_assign = n_tok * TOP_K
    n_blocks = -(-(n_assign + N_EXPERTS * (MOE_BM - 1)) // MOE_BM)
    return n_assign, n_blocks


def _route_tables(cnt, n_tok):
    _, n_blocks = _moe_rows(n_tok)
    counts = jnp.sum(cnt, axis=0)
    padded = (counts + MOE_BM - 1) // MOE_BM * MOE_BM
    pad_end = jnp.cumsum(padded)
    block_start = pad_end - padded
    seg_start = block_start[None, :] + jnp.cumsum(cnt, axis=0) - cnt
    blk_row = jnp.arange(n_blocks, dtype=I32) * MOE_BM
    block_e = jnp.minimum(jnp.sum((pad_end[None, :] <= blk_row[:, None]).astype(I32), axis=1), N_EXPERTS - 1)
    n_used = (pad_end[-1] // MOE_BM).reshape(1)
    block_valid = jnp.clip((block_start + counts)[block_e] - blk_row, 0, MOE_BM)
    eids = jnp.arange(N_EXPERTS, dtype=I32)
    later = jnp.where(jnp.logical_and(eids[None, :] > eids[:, None], counts[None, :] > 0), eids[None, :], N_EXPERTS)
    nxt = jnp.min(later, axis=1)
    next_e = jnp.where(nxt == N_EXPERTS, -1, nxt)[block_e]
    return dict(block_e=block_e.astype(I32), n_used=n_used.astype(I32),
                block_valid=block_valid.astype(I32), next_e=next_e.astype(I32),
                seg_start=seg_start.reshape(-1).astype(I32), cnt=cnt.reshape(-1).astype(I32),
                pad_start=(block_start + counts).astype(I32), pad_len=(padded - counts).astype(I32))


def _store_row_slabs(ref, val, n_rows):
    for s in range(ROW_SLABS):
        ref[pl.ds(s, n_rows, stride=ROW_SLABS), :] = val[:, s * LANES:(s + 1) * LANES]


def _load_row_slabs(ref, n_rows):
    return jnp.concatenate(
        [ref[pl.ds(s, n_rows, stride=ROW_SLABS), :] for s in range(ROW_SLABS)], axis=1)


def _slab_rows(start, n):
    return pl.ds(pl.multiple_of(start * ROW_SLABS, ROW_SLABS), n * ROW_SLABS)


def _local_positions(te):
    eidx = lax.broadcasted_iota(I32, (N_EXPERTS, TM), 0)
    hits = [te[k:k + 1] == eidx for k in range(TOP_K)]
    onehot = jnp.where(hits[0], 1.0, 0.0)
    for k in range(1, TOP_K):
        onehot = onehot + jnp.where(hits[k], 1.0, 0.0)
    mb = onehot.astype(BF16)
    trow = lax.broadcasted_iota(I32, (TM, TM), 0)
    tcol = lax.broadcasted_iota(I32, (TM, TM), 1)
    before = _dot(mb, (trow < tcol).astype(BF16))
    totals = _dot(mb, jnp.ones((TM, TM), BF16))
    erow = lax.broadcasted_iota(I32, (N_EXPERTS, N_EXPERTS), 0)
    ecol = lax.broadcasted_iota(I32, (N_EXPERTS, N_EXPERTS), 1)
    first = _dot((ecol < erow).astype(BF16), totals.astype(BF16))
    base = first + before
    return [jnp.sum(jnp.where(hits[k], base, 0.0), axis=0, keepdims=True).astype(I32) for k in range(TOP_K)]


def _dispatch_kernel(n_blocks, ss_ref, cn_ref, ps_ref, pl_ref, nu_ref, h_ref, te_ref, xs_hbm, lpos_ref,
                     sbuf, zbuf, sem):
    j = pl.program_id(0)
    n_tiles = pl.num_programs(0)

    def zero_fill(act):
        def per_expert(e, carry):
            n = pl_ref[e]

            @pl.when(n > 0)
            def _():
                act(pltpu.make_async_copy(zbuf.at[_slab_rows(0, n)], xs_hbm.at[_slab_rows(ps_ref[e], n)],
                                          sem.at[MOE_RING]))
            return carry
        lax.fori_loop(0, N_EXPERTS, per_expert, 0)

        def per_block(b, carry):
            act(pltpu.make_async_copy(zbuf, xs_hbm.at[_slab_rows(b * MOE_BM, MOE_BM)], sem.at[MOE_RING]))
            return carry
        lax.fori_loop(nu_ref[0], n_blocks, per_block, 0)

    @pl.when(j == 0)
    def _():
        zbuf[...] = jnp.zeros_like(zbuf)
        zero_fill(lambda cp: cp.start())

    lpos = _local_positions(te_ref[...])
    zi = jnp.zeros_like(lpos[0])
    lpos_ref[...] = jnp.concatenate(lpos + [zi] * (8 - TOP_K), axis=0)
    riota = lax.broadcasted_iota(I32, (TILE_ROWS, TM), 0)
    hit = riota == lpos[0]
    for k in range(1, TOP_K):
        hit = jnp.logical_or(hit, riota == lpos[k])
    perm = jnp.where(hit, 1.0, 0.0).astype(BF16)
    hb = h_ref[...].astype(BF16)
    slot = lax.rem(j, MOE_RING)
    buf = sbuf.at[slot]

    def rows_copy(sl):
        return pltpu.make_async_copy(sbuf.at[sl], xs_hbm.at[_slab_rows(0, TILE_ROWS)], sem.at[sl])

    @pl.when(j >= MOE_RING)
    def _():
        rows_copy(slot).wait()

    for s in range(0, ROW_SLABS, 2):
        xl = _dot(perm, hb[:, s * LANES:(s + 2) * LANES])
        buf[pl.ds(s, TILE_ROWS, stride=ROW_SLABS), :] = xl[:, :LANES]
        buf[pl.ds(s + 1, TILE_ROWS, stride=ROW_SLABS), :] = xl[:, LANES:]

    def segment(e, local):
        n = cn_ref[j * N_EXPERTS + e]

        @pl.when(n > 0)
        def _():
            pltpu.make_async_copy(buf.at[_slab_rows(local, n)],
                                  xs_hbm.at[_slab_rows(ss_ref[j * N_EXPERTS + e], n)], sem.at[slot]).start()
        return local + n
    lax.fori_loop(0, N_EXPERTS, segment, 0)

    @pl.when(j == n_tiles - 1)
    def _():
        for back in range(MOE_RING):
            rows_copy(lax.rem(j + MOE_RING - back, MOE_RING)).wait()
        zero_fill(lambda cp: cp.wait())


def _dispatch(h, top_e, tabs, n_tok):
    _, n_blocks = _moe_rows(n_tok)
    n_tiles = n_tok // TM
    grid_spec = pltpu.PrefetchScalarGridSpec(
        num_scalar_prefetch=5,
        grid=(n_tiles,),
        in_specs=[
            pl.BlockSpec((TM, D_MODEL), lambda j, *_: (j, 0)),
            pl.BlockSpec((8, TM), lambda j, *_: (0, j)),
        ],
        out_specs=[
            pl.BlockSpec(memory_space=pl.ANY),
            pl.BlockSpec((8, TM), lambda j, *_: (0, j)),
        ],
        scratch_shapes=[
            pltpu.VMEM((MOE_RING, TILE_ROWS * ROW_SLABS, LANES), F32),
            pltpu.VMEM((MOE_BM * ROW_SLABS, LANES), F32),
            pltpu.SemaphoreType.DMA((MOE_RING + 1,)),
        ],
    )
    return pl.pallas_call(
        functools.partial(_dispatch_kernel, n_blocks),
        grid_spec=grid_spec,
        out_shape=[
            jax.ShapeDtypeStruct((n_blocks * MOE_BM * ROW_SLABS, LANES), F32),
            jax.ShapeDtypeStruct((8, n_tok), I32),
        ],
        compiler_params=_params(("arbitrary",), VMEM_LIMIT),
        name="moe_dispatch",
    )(tabs["seg_start"], tabs["cnt"], tabs["pad_start"], tabs["pad_len"], tabs["n_used"], h, top_e)


def _moe_kernel(layer, be_ref, nu_ref, bv_ref, ne_ref, x_ref, b1_ref, b2_ref, w1_hbm, w2_hbm, y_ref,
                w1s, w2s, w1b, w2b, sem):
    i = pl.program_id(0)
    nu = nu_ref[0]

    def weight_copies(e):
        return (pltpu.make_async_copy(w1_hbm.at[layer, e], w1s, sem.at[0]),
                pltpu.make_async_copy(w2_hbm.at[layer, e], w2s, sem.at[1]))

    @pl.when(i < nu)
    def _():
        e = be_ref[i]

        @pl.when(i == 0)
        def _():
            for cp in weight_copies(e):
                cp.start()

        @pl.when(jnp.logical_or(i == 0, e != be_ref[jnp.maximum(i - 1, 0)]))
        def _():
            for cp in weight_copies(e):
                cp.wait()
            w1b[...] = w1s[...].astype(BF16)
            w2b[...] = w2s[...].astype(BF16)
            nxt = ne_ref[i]

            @pl.when(nxt >= 0)
            def _():
                for cp in weight_copies(nxt):
                    cp.start()

        def run(n_rows):
            slab_rows = pl.ds(0, n_rows * ROW_SLABS)
            x = _load_row_slabs(x_ref.at[slab_rows], n_rows).astype(BF16)
            h1 = _dot(x, w1b[...]) + b1_ref[0, 0]
            gate = jnp.minimum(h1[:, :D_MODEL], SWIGLU_LIMIT)
            up = jnp.clip(h1[:, D_MODEL:], -SWIGLU_LIMIT, SWIGLU_LIMIT)
            act = (up + 1.0) * gate * _sigmoid(SWIGLU_ALPHA * gate)
            y = _dot(act.astype(BF16), w2b[...]) + b2_ref[0, 0]
            _store_row_slabs(y_ref.at[slab_rows], y, n_rows)

        n_valid = bv_ref[i]

        @pl.when(n_valid > MOE_HALF)
        def _():
            run(MOE_BM)

        @pl.when(n_valid <= MOE_HALF)
        def _():
            run(MOE_HALF)
            y_ref[pl.ds(MOE_HALF * ROW_SLABS, MOE_HALF * ROW_SLABS), :] = jnp.zeros(
                (MOE_HALF * ROW_SLABS, LANES), F32)

    @pl.when(i >= nu)
    def _():
        y_ref[...] = jnp.zeros_like(y_ref)


def _moe(xs, tabs, layer, w1, b1, w2, b2, n_tok):
    _, n_blocks = _moe_rows(n_tok)
    d2 = 2 * D_MODEL
    blk = MOE_BM * ROW_SLABS
    grid_spec = pltpu.PrefetchScalarGridSpec(
        num_scalar_prefetch=4,
        grid=(n_blocks,),
        in_specs=[
            pl.BlockSpec((blk, LANES), lambda i, be, nu, *_: (jnp.minimum(i, nu[0] - 1), 0)),
            pl.BlockSpec((1, 1, 1, d2), lambda i, be, *_: (layer, be[i], 0, 0)),
            pl.BlockSpec((1, 1, 1, D_MODEL), lambda i, be, *_: (layer, be[i], 0, 0)),
            pl.BlockSpec(memory_space=pl.ANY),
            pl.BlockSpec(memory_space=pl.ANY),
        ],
        out_specs=pl.BlockSpec((blk, LANES), lambda i, *_: (i, 0)),
        scratch_shapes=[
            pltpu.VMEM((D_MODEL, d2), F32),
            pltpu.VMEM((D_MODEL, D_MODEL), F32),
            pltpu.VMEM((D_MODEL, d2), BF16),
            pltpu.VMEM((D_MODEL, D_MODEL), BF16),
            pltpu.SemaphoreType.DMA((2,)),
        ],
    )
    return pl.pallas_call(
        functools.partial(_moe_kernel, layer),
        grid_spec=grid_spec,
        out_shape=jax.ShapeDtypeStruct((n_blocks * blk, LANES), F32),
        compiler_params=_params(("arbitrary",), VMEM_LIMIT),
        name="moe_experts",
    )(tabs["block_e"], tabs["n_used"], tabs["block_valid"], tabs["next_e"], xs,
      b1.reshape(DEPTH, N_EXPERTS, 1, d2), b2.reshape(DEPTH, N_EXPERTS, 1, D_MODEL), w1, w2)


def _combine_kernel(final, n_tiles, ss_ref, cn_ref, lpos_ref, tg_ref, x_ref, mod_ref, g_ref, ys_hbm, o_ref,
                    cbuf, sem):
    t = pl.program_id(0) * n_tiles + pl.program_id(1)
    n_total = pl.num_programs(0) * n_tiles
    slot = lax.rem(t, MOE_RING)
    ahead = MOE_RING - 1

    def fetch(tile, sl):
        def segment(e, local):
            n = cn_ref[tile * N_EXPERTS + e]

            @pl.when(n > 0)
            def _():
                pltpu.make_async_copy(ys_hbm.at[_slab_rows(ss_ref[tile * N_EXPERTS + e], n)],
                                      cbuf.at[sl, _slab_rows(local, n)], sem.at[sl]).start()
            return local + n
        lax.fori_loop(0, N_EXPERTS, segment, 0)

    @pl.when(t == 0)
    def _():
        for first in range(ahead):
            fetch(first, first)

    @pl.when(t + ahead < n_total)
    def _():
        fetch(t + ahead, lax.rem(t + ahead, MOE_RING))

    pltpu.make_async_copy(ys_hbm.at[_slab_rows(0, TILE_ROWS)], cbuf.at[slot], sem.at[slot]).wait()
    yb = _load_row_slabs(cbuf.at[slot], TILE_ROWS).astype(BF16)

    lpos = lpos_ref[...]
    tg = tg_ref[...]
    riota = lax.broadcasted_iota(I32, (TILE_ROWS, TM), 0)
    gsel = jnp.where(riota == lpos[0:1], tg[0:1], 0.0)
    for k in range(1, TOP_K):
        gsel = gsel + jnp.where(riota == lpos[k:k + 1], tg[k:k + 1], 0.0)
    g_hi = gsel.astype(BF16)
    g_lo = (gsel - g_hi.astype(F32)).astype(BF16)
    f = _dot_tn(g_hi, yb) + _dot_tn(g_lo, yb)

    mod = mod_ref[0, 0]
    xn = x_ref[0] + mod[5:6] * f
    if final:
        ms = jnp.mean(xn * xn, axis=-1, keepdims=True)
        xn = xn * lax.rsqrt(ms + EPS) * g_ref[...]
    o_ref[0] = xn


def _combine(ys, tabs, lpos, top_g, xres, n_tiles, mod, mod_seg, g, final):
    grid_spec = pltpu.PrefetchScalarGridSpec(
        num_scalar_prefetch=2,
        grid=(BATCH, n_tiles),
        in_specs=[
            pl.BlockSpec((8, TM), lambda b, j, *_: (0, b * n_tiles + j)),
            pl.BlockSpec((8, TM), lambda b, j, *_: (0, b * n_tiles + j)),
            pl.BlockSpec((1, TM, D_MODEL), lambda b, j, *_: (b, j, 0)),
            pl.BlockSpec((1, 1, 8, D_MODEL), lambda b, j, *_: (b, mod_seg(j), 0, 0)),
            pl.BlockSpec((1, D_MODEL), lambda b, j, *_: (0, 0)),
            pl.BlockSpec(memory_space=pl.ANY),
        ],
        out_specs=pl.BlockSpec((1, TM, D_MODEL), lambda b, j, *_: (b, j, 0)),
        scratch_shapes=[
            pltpu.VMEM((MOE_RING, TILE_ROWS * ROW_SLABS, LANES), F32),
            pltpu.SemaphoreType.DMA((MOE_RING,)),
        ],
    )
    return pl.pallas_call(
        functools.partial(_combine_kernel, final, n_tiles),
        grid_spec=grid_spec,
        out_shape=jax.ShapeDtypeStruct((BATCH, n_tiles * TM, D_MODEL), F32),
        compiler_params=_params(("arbitrary", "arbitrary"), VMEM_LIMIT),
        name="moe_combine",
    )(tabs["seg_start"], tabs["cnt"], lpos, top_g, xres, mod, g, ys)


def _moe_ffn(h, top_e, top_g, cnt, layer, xres, n_tiles, mod, mod_seg, w1, b1, w2, b2, g, final):
    n_tok = BATCH * n_tiles * TM
    tabs = _route_tables(cnt[:, :, 0], n_tok)
    xs, lpos = _dispatch(h, top_e, tabs, n_tok)
    ys = _moe(xs, tabs, layer, w1, b1, w2, b2, n_tok)
    return _combine(ys, tabs, lpos, top_g, xres, n_tiles, mod, mod_seg, g, final)


def _qkv_kernel(x_ref, mod_ref, g_ref, w_ref, qn_ref, kn_ref, cos_ref, sin_ref, q_ref, k_ref, v_ref):
    mod = mod_ref[0, 0]
    h = _norm_mod(x_ref[0], g_ref[...], mod[0:1], mod[1:2]).astype(BF16)
    qkv = _dot(h, w_ref[...])
    cos = cos_ref[...]
    sin = sin_ref[...]
    lane = lax.broadcasted_iota(I32, (TM, C_HEAD_DIM), 1)
    first_half = (lane % (C_HEAD_DIM // 2)) < (C_HEAD_DIM // 4)
    quarter = C_HEAD_DIM // 4

    def head(xh, gn):
        ms = jnp.mean(xh * xh, axis=-1, keepdims=True)
        y = xh * lax.rsqrt(ms + EPS) * gn
        partner = jnp.where(first_half, pltpu.roll(y, C_HEAD_DIM - quarter, 1), pltpu.roll(y, quarter, 1))
        return y * cos + partner * sin

    qn = qn_ref[...]
    kn = kn_ref[...]
    for hq in range(C_HEADS):
        sl = slice(hq * C_HEAD_DIM, (hq + 1) * C_HEAD_DIM)
        q_ref[0, :, sl] = head(qkv[:, sl], qn).astype(BF16)
    for hk in range(C_KV_HEADS):
        src = slice((C_HEADS + hk) * C_HEAD_DIM, (C_HEADS + hk + 1) * C_HEAD_DIM)
        k_ref[0, :, hk * C_HEAD_DIM:(hk + 1) * C_HEAD_DIM] = head(qkv[:, src], kn).astype(BF16)
    ones = jnp.ones((TM, C_HEAD_DIM), BF16)
    for hk in range(C_KV_HEADS):
        src = slice((C_HEADS + C_KV_HEADS + hk) * C_HEAD_DIM, (C_HEADS + C_KV_HEADS + hk + 1) * C_HEAD_DIM)
        v_ref[0, :, 2 * hk * C_HEAD_DIM:(2 * hk + 1) * C_HEAD_DIM] = qkv[:, src].astype(BF16)
        v_ref[0, :, (2 * hk + 1) * C_HEAD_DIM:(2 * hk + 2) * C_HEAD_DIM] = ones


def _qkv(xall, mod, g, w, qn, kn, cos, sin):
    nj = L_ALL // TM
    kvw = C_KV_HEADS * C_HEAD_DIM
    return pl.pallas_call(
        _qkv_kernel,
        grid=(BATCH, nj),
        in_specs=[
            pl.BlockSpec((1, TM, D_MODEL), lambda b, j: (b, j, 0)),
            pl.BlockSpec((1, 1, 8, D_MODEL), lambda b, j: (b, jnp.minimum(j, 1), 0, 0)),
            pl.BlockSpec((1, D_MODEL), lambda b, j: (0, 0)),
            pl.BlockSpec((D_MODEL, C_QKV), lambda b, j: (0, 0)),
            pl.BlockSpec((1, C_HEAD_DIM), lambda b, j: (0, 0)),
            pl.BlockSpec((1, C_HEAD_DIM), lambda b, j: (0, 0)),
            pl.BlockSpec((TM, C_HEAD_DIM), lambda b, j: (j, 0)),
            pl.BlockSpec((TM, C_HEAD_DIM), lambda b, j: (j, 0)),
        ],
        out_specs=[
            pl.BlockSpec((1, TM, D_MODEL), lambda b, j: (b, j, 0)),
            pl.BlockSpec((1, TM, kvw), lambda b, j: (b, j, 0)),
            pl.BlockSpec((1, TM, 2 * kvw), lambda b, j: (b, j, 0)),
        ],
        out_shape=[
            jax.ShapeDtypeStruct((BATCH, L_ALL, D_MODEL), BF16),
            jax.ShapeDtypeStruct((BATCH, L_ALL, kvw), BF16),
            jax.ShapeDtypeStruct((BATCH, L_ALL, 2 * kvw), BF16),
        ],
        compiler_params=_params(("arbitrary", "arbitrary"), VMEM_LIMIT),
        name="qkv_rope",
    )(xall, mod, g, w, qn, kn, cos, sin)


def _rope_tables():
    rows = SEQ // GRID_W
    row = jnp.repeat(jnp.arange(rows), GRID_W).astype(F32)
    col = jnp.tile(jnp.arange(GRID_W), rows).astype(F32)
    half = C_HEAD_DIM // 2
    inv_freq = ROPE_THETA ** (-jnp.arange(0, half, 2, dtype=F32) / half)
    ar = row[:, None] * inv_freq
    ac = col[:, None] * inv_freq
    cos = jnp.concatenate([jnp.cos(ar), jnp.cos(ar), jnp.cos(ac), jnp.cos(ac)], axis=-1)
    sin = jnp.concatenate([-jnp.sin(ar), jnp.sin(ar), -jnp.sin(ac), jnp.sin(ac)], axis=-1)
    cos = jnp.concatenate([jnp.ones((CTX_LEN, C_HEAD_DIM), F32), cos], axis=0)
    sin = jnp.concatenate([jnp.zeros((CTX_LEN, C_HEAD_DIM), F32), sin], axis=0)
    return cos, sin


def _attn_kernel(q_ref, k_ref, v_ref, o_ref):
    c = (C_HEAD_DIM ** -0.5) * LOG2_E
    for g in range(C_KV_HEADS):
        kg = k_ref[0, :, g * C_HEAD_DIM:(g + 1) * C_HEAD_DIM]
        vg = v_ref[0, :, 2 * g * C_HEAD_DIM:(2 * g + 2) * C_HEAD_DIM]
        for hh in range(C_GROUP):
            sl = slice((g * C_GROUP + hh) * C_HEAD_DIM, (g * C_GROUP + hh + 1) * C_HEAD_DIM)
            s = _dot_nt(q_ref[0, :, sl], kg)
            m = jnp.max(s, axis=-1, keepdims=True)
            p = jnp.exp2((s - m) * c).astype(BF16)
            oe = _dot(p, vg)
            o_ref[0, :, sl] = (oe[:, :C_HEAD_DIM] / oe[:, C_HEAD_DIM:]).astype(BF16)


def _attention(q, k, v):
    kvw = C_KV_HEADS * C_HEAD_DIM
    ctx_tiles = CTX_LEN // TM
    return pl.pallas_call(
        _attn_kernel,
        grid=(BATCH, SEQ // TM),
        in_specs=[
            pl.BlockSpec((1, TM, D_MODEL), lambda b, j: (b, j + ctx_tiles, 0)),
            pl.BlockSpec((1, L_ALL, kvw), lambda b, j: (b, 0, 0)),
            pl.BlockSpec((1, L_ALL, 2 * kvw), lambda b, j: (b, 0, 0)),
        ],
        out_specs=pl.BlockSpec((1, TM, D_MODEL), lambda b, j: (b, j, 0)),
        out_shape=jax.ShapeDtypeStruct((BATCH, SEQ, D_MODEL), BF16),
        compiler_params=_params(("arbitrary", "arbitrary"), VMEM_LIMIT),
        name="attention",
    )(q, k, v)


def _mod_table(ada_layer):
    m = ada_layer.reshape(16, N_MOD, D_MODEL)
    m_lat = m[:BATCH]
    m_ctx = jnp.broadcast_to(m[BATCH], (BATCH, N_MOD, D_MODEL))
    t = jnp.stack([m_ctx, m_lat], axis=1)
    return jnp.pad(t, ((0, 0), (0, 0), (0, 8 - N_MOD), (0, 0)))


def kernel(x, c, ctx, c_ctx, ada_w, ada_b, norm_mix, norm_ffn, ab_w_in, ab_gate_w, ab_gate_b, ab_out_norm,
           ab_conv_w, ab_conv_b, ab_w_out, attn_w_qkv, attn_q_norm, attn_k_norm, attn_w_o, router_w,
           router_b, moe_w1, moe_b1, moe_w2, moe_b2, final_norm):
    cond = jnp.zeros((16, D_MODEL), F32).at[:BATCH].set(c).at[BATCH].set(c_ctx)
    ada = _ada(cond, ada_w, ada_b)
    mod0 = _mod_table(ada[0])
    mod1 = _mod_table(ada[1])
    seg_all = lambda j: jnp.minimum(j, 1)
    seg_lat = lambda j: 1
    n_tiles_all = L_ALL // TM
    n_tiles_lat = SEQ // TM
    fnorm = final_norm.reshape(1, D_MODEL)

    w_in = ab_w_in[0]
    lr0 = 2 * A_HEADS * A_DK + A_HEADS * A_DV
    lr1 = lr0 + 2 * A_GATE_RANK
    w_main = jnp.concatenate([w_in[:, :lr0], w_in[:, lr1:]], axis=1).astype(BF16)
    w_lr = jnp.pad(w_in[:, lr0:lr1], ((0, 0), (0, LANES - 2 * A_GATE_RANK))).astype(BF16)
    p, lr = _proj_in(ctx, x, mod0, norm_mix[0:1], w_main, w_lr)

    gw = ab_gate_w[0]
    gw_pad = jnp.zeros((2, LANES, A_HEADS * A_DK), F32)
    gw_pad = gw_pad.at[0, :A_GATE_RANK].set(gw[0]).at[1, A_GATE_RANK:2 * A_GATE_RANK].set(gw[1])
    cw = jnp.concatenate([ab_conv_w[0], ab_conv_b[0][None], jnp.zeros((4, B_WIDTH), F32)], axis=0)
    yg, yc = _gla_conv(p, lr, gw_pad.astype(BF16), ab_gate_b[0].reshape(2, 1, A_HEADS * A_DK),
                       ab_out_norm[0:1], cw)

    w_out = ab_w_out[0].astype(BF16)
    ngla = A_HEADS * A_DV
    xmid, h, top_e, top_g, cnt = _outproj(
        [yg, yc], [w_out[:ngla], w_out[ngla:]], [ctx, x], _ctx_latent_specs(), n_tiles_all, mod0, seg_all,
        norm_ffn[0:1], router_w[0].T, router_b[0].reshape(N_EXPERTS, 1))
    x1 = _moe_ffn(h, top_e, top_g, cnt, 0, xmid, n_tiles_all, mod0, seg_all,
                  moe_w1, moe_b1, moe_w2, moe_b2, fnorm, False)

    cos, sin = _rope_tables()
    q, k, v = _qkv(x1, mod1, norm_mix[1:2], attn_w_qkv[0].astype(BF16), attn_q_norm[0:1],
                   attn_k_norm[0:1], cos, sin)
    o = _attention(q, k, v)
    xmid, h, top_e, top_g, cnt = _outproj(
        [o], [attn_w_o[0].astype(BF16)], [x1],
        [pl.BlockSpec((1, TM, D_MODEL), lambda b, j: (b, j + CTX_LEN // TM, 0))], n_tiles_lat, mod1, seg_lat,
        norm_ffn[1:2], router_w[1].T, router_b[1].reshape(N_EXPERTS, 1))
    return _moe_ffn(h, top_e, top_g, cnt, 1, xmid, n_tiles_lat, mod1, seg_lat,
                    moe_w1, moe_b1, moe_w2, moe_b2, fnorm, True)
```

```python
import functools

import jax
import jax.numpy as jnp
from jax import lax
from jax.experimental import pallas as pl
from jax.experimental.pallas import tpu as pltpu

F32 = jnp.float32
BF16 = jnp.bfloat16
I32 = jnp.int32

D_MODEL = 1024
BATCH = 8
SEQ = 2048
DEPTH = 2
GRID_W = 64
CTX_LEN = 256
L_ALL = CTX_LEN + SEQ
N_MOD = 6
EPS = 1e-6

A_DV = 128
A_HEADS = 4
A_DK = 64
A_GATE_RANK = 16
A_GATE_TAU = 16.0
GLA_CHUNK = 64
B_WIDTH = 512
AB_MAIN = 3072

C_HEAD_DIM = 128
C_HEADS = 8
C_KV_HEADS = 2
C_GROUP = 4
C_QKV = (C_HEADS + 2 * C_KV_HEADS) * C_HEAD_DIM
ROPE_THETA = 10000.0
LOG2_E = 1.4426950408889634

N_EXPERTS = 32
TOP_K = 4
SWIGLU_LIMIT = 7.0
SWIGLU_ALPHA = 1.702

LANES = 128
SUBLANES = 8
ROW_SLABS = D_MODEL // LANES
TM = 256
TILE_ROWS = TM * TOP_K
MOE_BM = 512
MOE_HALF = MOE_BM // 2
MOE_RING = 3
VMEM_LIMIT = 56 * 1024 * 1024

NT_DIMS = (((1,), (1,)), ((), ()))
TN_DIMS = (((0,), (0,)), ((), ()))


def _dot(a, b):
    return jnp.dot(a, b, preferred_element_type=F32)


def _dot_nt(a, b):
    return lax.dot_general(a, b, NT_DIMS, preferred_element_type=F32)


def _dot_tn(a, b):
    return lax.dot_general(a, b, TN_DIMS, preferred_element_type=F32)


def _params(sem, vmem=None):
    return pltpu.CompilerParams(dimension_semantics=sem, vmem_limit_bytes=vmem)


def _norm_mod(x, g, shift, scale):
    ms = jnp.mean(x * x, axis=-1, keepdims=True)
    y = x * lax.rsqrt(ms + EPS) * g
    return y * (1.0 + scale) + shift


def _sigmoid(x):
    return 1.0 / (1.0 + jnp.exp(-x))


def _ada_kernel(cond_ref, w_ref, b_ref, o_ref):
    c = cond_ref[...]
    s = (c * _sigmoid(c)).astype(BF16)
    o_ref[0] = _dot(s, w_ref[0].astype(BF16)) + b_ref[0]


def _ada(cond, ada_w, ada_b):
    tn = 1536
    n = N_MOD * D_MODEL
    return pl.pallas_call(
        _ada_kernel,
        grid=(DEPTH, n // tn),
        in_specs=[
            pl.BlockSpec((16, D_MODEL), lambda l, j: (0, 0)),
            pl.BlockSpec((1, D_MODEL, tn), lambda l, j: (l, 0, j)),
            pl.BlockSpec((1, 1, tn), lambda l, j: (l, 0, j)),
        ],
        out_specs=pl.BlockSpec((1, 16, tn), lambda l, j: (l, 0, j)),
        out_shape=jax.ShapeDtypeStruct((DEPTH, 16, n), F32),
        compiler_params=_params(("arbitrary", "arbitrary"), VMEM_LIMIT),
        name="ada",
    )(cond, ada_w, ada_b.reshape(DEPTH, 1, n))


def _ctx_or_latent(ctx_ref, x_ref):
    return jnp.where(pl.program_id(1) == 0, ctx_ref[0], x_ref[0])


def _ctx_latent_specs():
    return [pl.BlockSpec((1, TM, D_MODEL), lambda b, j: (b, 0, 0)),
            pl.BlockSpec((1, TM, D_MODEL), lambda b, j: (b, jnp.maximum(j - 1, 0), 0))]


def _proj_in_kernel(ctx_ref, x_ref, mod_ref, g_ref, w_ref, wlr_ref, p_ref, lr_ref):
    mod = mod_ref[0, 0]
    h = _norm_mod(_ctx_or_latent(ctx_ref, x_ref), g_ref[...], mod[0:1], mod[1:2]).astype(BF16)
    p_ref[0] = _dot(h, w_ref[...]).astype(BF16)
    lr_ref[0] = _dot(h, wlr_ref[...]).astype(BF16)


def _proj_in(ctx, x, mod, g, w_main, w_lr):
    nj = L_ALL // TM
    return pl.pallas_call(
        _proj_in_kernel,
        grid=(BATCH, nj),
        in_specs=_ctx_latent_specs() + [
            pl.BlockSpec((1, 1, 8, D_MODEL), lambda b, j: (b, jnp.minimum(j, 1), 0, 0)),
            pl.BlockSpec((1, D_MODEL), lambda b, j: (0, 0)),
            pl.BlockSpec((D_MODEL, AB_MAIN), lambda b, j: (0, 0)),
            pl.BlockSpec((D_MODEL, LANES), lambda b, j: (0, 0)),
        ],
        out_specs=[
            pl.BlockSpec((1, TM, AB_MAIN), lambda b, j: (b, j, 0)),
            pl.BlockSpec((1, TM, LANES), lambda b, j: (b, j, 0)),
        ],
        out_shape=[
            jax.ShapeDtypeStruct((BATCH, L_ALL, AB_MAIN), BF16),
            jax.ShapeDtypeStruct((BATCH, L_ALL, LANES), BF16),
        ],
        compiler_params=_params(("arbitrary", "arbitrary"), VMEM_LIMIT),
        name="proj_in",
    )(ctx, x, mod, g, w_main, w_lr)


def _log_sigmoid(z):
    return jnp.minimum(z, 0.0) - jnp.log1p(jnp.exp(-jnp.abs(z)))


def _gla_conv_kernel(q_ref, k_ref, v_ref, r_ref, gb_ref, gc_ref, u_ref, lr_ref, gw_ref, gbias_ref,
                     onorm_ref, cw_ref, yg_ref, yc_ref, of_ref, ob_ref, sf_ref, sb_ref, xs_ref):
    ch = GLA_CHUNK
    grp = TM
    n_grp = L_ALL // grp
    cpg = grp // ch
    row = lax.broadcasted_iota(I32, (grp, grp), 0)
    col = lax.broadcasted_iota(I32, (grp, grp), 1)
    same_chunk = (row // ch) == (col // ch)
    tri = (jnp.logical_and(same_chunk, row >= col), jnp.logical_and(same_chunk, col >= row))
    tri_bf = (tri[0].astype(BF16), tri[1].astype(BF16))
    lane = lax.broadcasted_iota(I32, (grp, LANES), 1)
    head_mask = (lane < A_DK, lane >= A_DK)
    chunk_of_row = lax.broadcasted_iota(I32, (grp, LANES), 0) // ch

    sf_ref[...] = jnp.zeros_like(sf_ref)
    sb_ref[...] = jnp.zeros_like(sb_ref)

    def chunk_rows(x, idx):
        return jnp.concatenate(
            [jnp.broadcast_to(x[c * ch + idx:c * ch + idx + 1], (ch, x.shape[1])) for c in range(cpg)], axis=0)

    s_refs = (sf_ref, sb_ref)
    o_refs = (of_ref, ob_ref)
    dirs = (0, 1)
    pairs = [(d, hh) for d in dirs for hh in range(2)]

    def by_chunk_lanes(m, x):
        return jnp.concatenate([jnp.where(jnp.logical_and(m, chunk_of_row == c), x, 0.0) for c in range(cpg)],
                               axis=1).astype(BF16)

    def body(i, carry):
        gidx = (i, jnp.where(i == 0, 0, n_grp - i))
        rows = [pl.ds(pl.multiple_of(g * grp, grp), grp) for g in gidx]
        q = [q_ref[0, r, :].astype(F32) * (A_DK ** -0.5) for r in rows]
        k = [k_ref[0, r, :].astype(F32) for r in rows]
        v = [v_ref[0, r, :] for r in rows]
        z = [_dot(lr_ref[0, rows[d], :].astype(BF16), gw_ref[d]) + gbias_ref[d] for d in dirs]
        a = [_log_sigmoid(z[d]) * (1.0 / A_GATE_TAU) for d in dirs]
        a_hi = [a[d].astype(BF16) for d in dirs]
        a_lo = [(a[d] - a_hi[d].astype(F32)).astype(BF16) for d in dirs]
        cum = [_dot(tri_bf[d], a_hi[d]) + _dot(tri_bf[d], a_lo[d]) for d in dirs]
        ref = (chunk_rows(cum[0], ch // 2 - 1), chunk_rows(cum[1], ch // 2))
        last = (chunk_rows(cum[0], ch - 1), chunk_rows(cum[1], 0))
        order = (range(cpg), range(cpg - 1, -1, -1))
        qe = [q[d] * jnp.exp(cum[d]) for d in dirs]
        qt = [q[d] * jnp.exp(cum[d] - ref[d]) for d in dirs]
        kt = [(k[d] * jnp.exp(ref[d] - cum[d])).astype(BF16) for d in dirs]
        kl = [k[d] * jnp.exp(last[d] - cum[d]) for d in dirs]
        dec = [jnp.exp(last[d]) for d in dirs]
        vh = {(d, hh): v[d][:, hh * A_DV:(hh + 1) * A_DV].astype(BF16) for d, hh in pairs}
        sc = {(d, hh): _dot_nt(jnp.where(head_mask[hh], qt[d], 0.0).astype(BF16), kt[d]) for d, hh in pairs}
        kv = {(d, hh): _dot(vh[d, hh].T, by_chunk_lanes(head_mask[hh], kl[d])) for d, hh in pairs}
        o_intra = {(d, hh): _dot(jnp.where(tri[d], sc[d, hh], 0.0).astype(BF16), vh[d, hh]) for d, hh in pairs}
        for d, hh in pairs:
            st = s_refs[d][hh]
            states = [None] * cpg
            for c in order[d]:
                states[c] = st
                st = st * dec[d][c * ch:c * ch + 1] + kv[d, hh][:, c * LANES:(c + 1) * LANES]
            s_refs[d][hh] = st
            o_inter = _dot_nt(by_chunk_lanes(head_mask[hh], qe[d]), jnp.concatenate(states, axis=1).astype(BF16))
            o_refs[d][rows[d], hh * A_DV:(hh + 1) * A_DV] = o_intra[d, hh] + o_inter
        return carry

    lax.fori_loop(0, n_grp, body, 0)

    pad = SUBLANES
    xs_ref[0:pad, :] = jnp.zeros((pad, xs_ref.shape[1]), F32)
    xs_ref[pad + L_ALL:, :] = jnp.zeros((pad, xs_ref.shape[1]), F32)
    xs_ref[pad:pad + L_ALL, :] = gc_ref[0].astype(F32) * u_ref[0].astype(F32)

    cw = cw_ref[...]
    onorm = onorm_ref[...]
    trow = lax.broadcasted_iota(I32, (TM, 1), 0)
    for ti in range(L_ALL // TM):
        s0 = ti * TM
        rows = slice(s0, s0 + TM)
        o = of_ref[rows, :] + ob_ref[rows, :]
        r = r_ref[0, rows, :].astype(F32)
        parts = []
        for hh in range(2):
            oh = o[:, hh * A_DV:(hh + 1) * A_DV]
            ms = jnp.mean(oh * oh, axis=-1, keepdims=True)
            parts.append(oh * lax.rsqrt(ms + EPS) * onorm[:, hh * A_DV:(hh + 1) * A_DV])
        on = jnp.concatenate(parts, axis=1)
        yg_ref[0, rows, :] = (on * (r * _sigmoid(r))).astype(BF16)
        xm1 = xs_ref[pad + s0 - 1:pad + s0 - 1 + TM, :]
        x0 = xs_ref[pad + s0:pad + s0 + TM, :]
        xp1 = xs_ref[pad + s0 + 1:pad + s0 + 1 + TM, :]
        if s0 + TM == CTX_LEN:
            xp1 = jnp.where(trow == TM - 1, 0.0, xp1)
        if s0 == CTX_LEN:
            xm1 = jnp.where(trow == 0, 0.0, xm1)
        conv = cw[0:1] * xm1 + cw[1:2] * x0 + cw[2:3] * xp1 + cw[3:4]
        yc_ref[0, rows, :] = (gb_ref[0, rows, :].astype(F32) * conv).astype(BF16)


def _gla_conv(p, lr, gw_pad, gbias, onorm, cw):
    hw = 2 * A_DV

    def pspec(width, base):
        return pl.BlockSpec((1, L_ALL, width), lambda b, i: (b, 0, base + i))

    return pl.pallas_call(
        _gla_conv_kernel,
        grid=(BATCH, A_HEADS // 2),
        in_specs=[
            pspec(LANES, 0),
            pspec(LANES, 2),
            pspec(hw, 2),
            pspec(hw, 4),
            pspec(hw, 6),
            pspec(hw, 8),
            pspec(hw, 10),
            pl.BlockSpec((1, L_ALL, LANES), lambda b, i: (b, 0, 0)),
            pl.BlockSpec((2, LANES, LANES), lambda b, i: (0, 0, i)),
            pl.BlockSpec((2, 1, LANES), lambda b, i: (0, 0, i)),
            pl.BlockSpec((1, hw), lambda b, i: (0, i)),
            pl.BlockSpec((8, hw), lambda b, i: (0, i)),
        ],
        out_specs=[
            pl.BlockSpec((1, L_ALL, hw), lambda b, i: (b, 0, i)),
            pl.BlockSpec((1, L_ALL, hw), lambda b, i: (b, 0, i)),
        ],
        out_shape=[
            jax.ShapeDtypeStruct((BATCH, L_ALL, A_HEADS * A_DV), BF16),
            jax.ShapeDtypeStruct((BATCH, L_ALL, B_WIDTH), BF16),
        ],
        scratch_shapes=[
            pltpu.VMEM((L_ALL, hw), F32),
            pltpu.VMEM((L_ALL, hw), F32),
            pltpu.VMEM((2, A_DV, LANES), F32),
            pltpu.VMEM((2, A_DV, LANES), F32),
            pltpu.VMEM((L_ALL + 2 * SUBLANES, hw), F32),
        ],
        compiler_params=_params(("arbitrary", "arbitrary"), VMEM_LIMIT),
        name="gla_conv",
    )(p, p, p, p, p, p, p, lr, gw_pad, gbias, onorm, cw)


def _outproj_kernel(n_in, n_res, *refs):
    y_refs = refs[:n_in]
    w_refs = refs[n_in:2 * n_in]
    res_refs = refs[2 * n_in:2 * n_in + n_res]
    mod_ref, g_ref, rwt_ref, rb_ref, xo_ref, h_ref, te_ref, tg_ref, cnt_ref = refs[2 * n_in + n_res:]
    is_ctx = pl.program_id(1) == 0
    mod = mod_ref[0, 0]
    wt = rwt_ref[...]
    wb = wt.astype(BF16)
    wl = (wt - wb.astype(F32)).astype(BF16)
    eidx = lax.broadcasted_iota(I32, (N_EXPERTS, LANES), 0)
    cnt = jnp.zeros((N_EXPERTS, 1), I32)
    groups = [slice(r0, r0 + LANES) for r0 in range(0, TM, LANES)]
    accs = []
    for rows in groups:
        acc = _dot(y_refs[0][0, rows, :], w_refs[0][...])
        for i in range(1, n_in):
            acc = acc + _dot(y_refs[i][0, rows, :], w_refs[i][...])
        accs.append(acc)
    hs = []
    for rows, acc in zip(groups, accs):
        xres = res_refs[0][0, rows, :]
        if n_res == 2:
            xres = jnp.where(is_ctx, xres, res_refs[1][0, rows, :])
        xn = xres + mod[2:3] * acc
        xo_ref[0, rows, :] = xn
        h = _norm_mod(xn, g_ref[...], mod[3:4], mod[4:5])
        h_ref[rows, :] = h.astype(BF16)
        hs.append(h)
    all_logits = []
    for h in hs:
        hb = h.astype(BF16)
        hl = (h - hb.astype(F32)).astype(BF16)
        all_logits.append(_dot_nt(wb, hb) + _dot_nt(wb, hl) + _dot_nt(wl, hb) + rb_ref[...])
    for rows, logits in zip(groups, all_logits):
        cur = logits
        vals, idxs = [], []
        for _ in range(TOP_K):
            m = jnp.max(cur, axis=0, keepdims=True)
            sel = jnp.min(jnp.where(cur == m, eidx, N_EXPERTS), axis=0, keepdims=True)
            vals.append(m)
            idxs.append(sel)
            cur = jnp.where(eidx == sel, -jnp.inf, cur)
        ex = [jnp.exp(v - vals[0]) for v in vals]
        den = ex[0] + ex[1] + ex[2] + ex[3]
        zi = jnp.zeros_like(idxs[0])
        zf = jnp.zeros_like(den)
        te_ref[:, rows] = jnp.concatenate(idxs + [zi] * (8 - TOP_K), axis=0)
        tg_ref[:, rows] = jnp.concatenate([e / den for e in ex] + [zf] * (8 - TOP_K), axis=0)
        onehot = jnp.where(eidx == idxs[0], 1, 0)
        for k in range(1, TOP_K):
            onehot = onehot + jnp.where(eidx == idxs[k], 1, 0)
        cnt = cnt + jnp.sum(onehot, axis=1, keepdims=True)
    cnt_ref[0] = jnp.broadcast_to(cnt, (N_EXPERTS, LANES))


def _outproj(ys, ws, res, res_specs, n_tiles, mod, mod_seg, g, rwt, rb):
    n_in = len(ys)
    n_tok = BATCH * n_tiles * TM
    in_specs = []
    for y in ys:
        in_specs.append(pl.BlockSpec((1, TM, y.shape[2]), lambda b, j: (b, j, 0)))
    for w in ws:
        in_specs.append(pl.BlockSpec(w.shape, lambda b, j: (0, 0)))
    in_specs += list(res_specs) + [
        pl.BlockSpec((1, 1, 8, D_MODEL), lambda b, j: (b, mod_seg(j), 0, 0)),
        pl.BlockSpec((1, D_MODEL), lambda b, j: (0, 0)),
        pl.BlockSpec((N_EXPERTS, D_MODEL), lambda b, j: (0, 0)),
        pl.BlockSpec((N_EXPERTS, 1), lambda b, j: (0, 0)),
    ]
    return pl.pallas_call(
        functools.partial(_outproj_kernel, n_in, len(res)),
        grid=(BATCH, n_tiles),
        in_specs=in_specs,
        out_specs=[
            pl.BlockSpec((1, TM, D_MODEL), lambda b, j: (b, j, 0)),
            pl.BlockSpec((TM, D_MODEL), lambda b, j: (b * n_tiles + j, 0)),
            pl.BlockSpec((8, TM), lambda b, j: (0, b * n_tiles + j)),
            pl.BlockSpec((8, TM), lambda b, j: (0, b * n_tiles + j)),
            pl.BlockSpec((1, N_EXPERTS, LANES), lambda b, j: (b * n_tiles + j, 0, 0)),
        ],
        out_shape=[
            jax.ShapeDtypeStruct((BATCH, n_tiles * TM, D_MODEL), F32),
            jax.ShapeDtypeStruct((n_tok, D_MODEL), BF16),
            jax.ShapeDtypeStruct((8, n_tok), I32),
            jax.ShapeDtypeStruct((8, n_tok), F32),
            jax.ShapeDtypeStruct((BATCH * n_tiles, N_EXPERTS, LANES), I32),
        ],
        compiler_params=_params(("arbitrary", "arbitrary"), VMEM_LIMIT),
        name="outproj_router",
    )(*ys, *ws, *res, mod, g, rwt, rb)


def _moe_rows(n_tok):
    n_assign = n_tok * TOP_K
    n_blocks = -(-(n_assign + N_EXPERTS * (MOE_BM - 1)) // MOE_BM)
    return n_assign, n_blocks


def _route_tables(cnt, n_tok):
    _, n_blocks = _moe_rows(n_tok)
    counts = jnp.sum(cnt, axis=0)
    padded = (counts + MOE_BM - 1) // MOE_BM * MOE_BM
    pad_end = jnp.cumsum(padded)
    block_start = pad_end - padded
    seg_start = block_start[None, :] + jnp.cumsum(cnt, axis=0) - cnt
    blk_row = jnp.arange(n_blocks, dtype=I32) * MOE_BM
    block_e = jnp.minimum(jnp.sum((pad_end[None, :] <= blk_row[:, None]).astype(I32), axis=1), N_EXPERTS - 1)
    n_used = (pad_end[-1] // MOE_BM).reshape(1)
    block_valid = jnp.clip((block_start + counts)[block_e] - blk_row, 0, MOE_BM)
    eids = jnp.arange(N_EXPERTS, dtype=I32)
    later = jnp.where(jnp.logical_and(eids[None, :] > eids[:, None], counts[None, :] > 0), eids[None, :], N_EXPERTS)
    nxt = jnp.min(later, axis=1)
    next_e = jnp.where(nxt == N_EXPERTS, -1, nxt)[block_e]
    return dict(block_e=block_e.astype(I32), n_used=n_used.astype(I32),
                block_valid=block_valid.astype(I32), next_e=next_e.astype(I32),
                seg_start=seg_start.reshape(-1).astype(I32), cnt=cnt.reshape(-1).astype(I32),
                pad_start=(block_start + counts).astype(I32), pad_len=(padded - counts).astype(I32))


def _store_row_slabs(ref, val, n_rows):
    for s in range(ROW_SLABS):
        ref[pl.ds(s, n_rows, stride=ROW_SLABS), :] = val[:, s * LANES:(s + 1) * LANES]


def _load_row_slabs(ref, n_rows):
    return jnp.concatenate(
        [ref[pl.ds(s, n_rows, stride=ROW_SLABS), :] for s in range(ROW_SLABS)], axis=1)


def _slab_rows(start, n):
    return pl.ds(pl.multiple_of(start * ROW_SLABS, ROW_SLABS), n * ROW_SLABS)


def _local_positions(te):
    eidx = lax.broadcasted_iota(I32, (N_EXPERTS, TM), 0)
    hits = [te[k:k + 1] == eidx for k in range(TOP_K)]
    onehot = jnp.where(hits[0], 1.0, 0.0)
    for k in range(1, TOP_K):
        onehot = onehot + jnp.where(hits[k], 1.0, 0.0)
    mb = onehot.astype(BF16)
    trow = lax.broadcasted_iota(I32, (TM, TM), 0)
    tcol = lax.broadcasted_iota(I32, (TM, TM), 1)
    before = _dot(mb, (trow < tcol).astype(BF16))
    totals = _dot(mb, jnp.ones((TM, TM), BF16))
    erow = lax.broadcasted_iota(I32, (N_EXPERTS, N_EXPERTS), 0)
    ecol = lax.broadcasted_iota(I32, (N_EXPERTS, N_EXPERTS), 1)
    first = _dot((ecol < erow).astype(BF16), totals.astype(BF16))
    base = first + before
    return [jnp.sum(jnp.where(hits[k], base, 0.0), axis=0, keepdims=True).astype(I32) for k in range(TOP_K)]


def _dispatch_kernel(n_blocks, ss_ref, cn_ref, ps_ref, pl_ref, nu_ref, h_ref, te_ref, xs_hbm, lpos_ref,
                     sbuf, zbuf, sem):
    j = pl.program_id(0)
    n_tiles = pl.num_programs(0)

    def zero_fill(act):
        def per_expert(e, carry):
            n = pl_ref[e]

            @pl.when(n > 0)
            def _():
                act(pltpu.make_async_copy(zbuf.at[_slab_rows(0, n)], xs_hbm.at[_slab_rows(ps_ref[e], n)],
                                          sem.at[MOE_RING]))
            return carry
        lax.fori_loop(0, N_EXPERTS, per_expert, 0)

        def per_block(b, carry):
            act(pltpu.make_async_copy(zbuf, xs_hbm.at[_slab_rows(b * MOE_BM, MOE_BM)], sem.at[MOE_RING]))
            return carry
        lax.fori_loop(nu_ref[0], n_blocks, per_block, 0)

    @pl.when(j == 0)
    def _():
        zbuf[...] = jnp.zeros_like(zbuf)
        zero_fill(lambda cp: cp.start())

    lpos = _local_positions(te_ref[...])
    zi = jnp.zeros_like(lpos[0])
    lpos_ref[...] = jnp.concatenate(lpos + [zi] * (8 - TOP_K), axis=0)
    riota = lax.broadcasted_iota(I32, (TILE_ROWS, TM), 0)
    hit = riota == lpos[0]
    for k in range(1, TOP_K):
        hit = jnp.logical_or(hit, riota == lpos[k])
    perm = jnp.where(hit, 1.0, 0.0).astype(BF16)
    hb = h_ref[...].astype(BF16)
    slot = lax.rem(j, MOE_RING)
    buf = sbuf.at[slot]

    def rows_copy(sl):
        return pltpu.make_async_copy(sbuf.at[sl], xs_hbm.at[_slab_rows(0, TILE_ROWS)], sem.at[sl])

    @pl.when(j >= MOE_RING)
    def _():
        rows_copy(slot).wait()

    for s in range(0, ROW_SLABS, 2):
        xl = _dot(perm, hb[:, s * LANES:(s + 2) * LANES])
        buf[pl.ds(s, TILE_ROWS, stride=ROW_SLABS), :] = xl[:, :LANES]
        buf[pl.ds(s + 1, TILE_ROWS, stride=ROW_SLABS), :] = xl[:, LANES:]

    def segment(e, local):
        n = cn_ref[j * N_EXPERTS + e]

        @pl.when(n > 0)
        def _():
            pltpu.make_async_copy(buf.at[_slab_rows(local, n)],
                                  xs_hbm.at[_slab_rows(ss_ref[j * N_EXPERTS + e], n)], sem.at[slot]).start()
        return local + n
    lax.fori_loop(0, N_EXPERTS, segment, 0)

    @pl.when(j == n_tiles - 1)
    def _():
        for back in range(MOE_RING):
            rows_copy(lax.rem(j + MOE_RING - back, MOE_RING)).wait()
        zero_fill(lambda cp: cp.wait())


def _dispatch(h, top_e, tabs, n_tok):
    _, n_blocks = _moe_rows(n_tok)
    n_tiles = n_tok // TM
    grid_spec = pltpu.PrefetchScalarGridSpec(
        num_scalar_prefetch=5,
        grid=(n_tiles,),
        in_specs=[
            pl.BlockSpec((TM, D_MODEL), lambda j, *_: (j, 0)),
            pl.BlockSpec((8, TM), lambda j, *_: (0, j)),
        ],
        out_specs=[
            pl.BlockSpec(memory_space=pl.ANY),
            pl.BlockSpec((8, TM), lambda j, *_: (0, j)),
        ],
        scratch_shapes=[
            pltpu.VMEM((MOE_RING, TILE_ROWS * ROW_SLABS, LANES), F32),
            pltpu.VMEM((MOE_BM * ROW_SLABS, LANES), F32),
            pltpu.SemaphoreType.DMA((MOE_RING + 1,)),
        ],
    )
    return pl.pallas_call(
        functools.partial(_dispatch_kernel, n_blocks),
        grid_spec=grid_spec,
        out_shape=[
            jax.ShapeDtypeStruct((n_blocks * MOE_BM * ROW_SLABS, LANES), F32),
            jax.ShapeDtypeStruct((8, n_tok), I32),
        ],
        compiler_params=_params(("arbitrary",), VMEM_LIMIT),
        name="moe_dispatch",
    )(tabs["seg_start"], tabs["cnt"], tabs["pad_start"], tabs["pad_len"], tabs["n_used"], h, top_e)


def _moe_kernel(layer, be_ref, nu_ref, bv_ref, ne_ref, x_ref, b1_ref, b2_ref, w1_hbm, w2_hbm, y_ref,
                w1s, w2s, w1b, w2b, sem):
    i = pl.program_id(0)
    nu = nu_ref[0]

    def weight_copies(e):
        return (pltpu.make_async_copy(w1_hbm.at[layer, e], w1s, sem.at[0]),
                pltpu.make_async_copy(w2_hbm.at[layer, e], w2s, sem.at[1]))

    @pl.when(i < nu)
    def _():
        e = be_ref[i]

        @pl.when(i == 0)
        def _():
            for cp in weight_copies(e):
                cp.start()

        @pl.when(jnp.logical_or(i == 0, e != be_ref[jnp.maximum(i - 1, 0)]))
        def _():
            for cp in weight_copies(e):
                cp.wait()
            w1b[...] = w1s[...].astype(BF16)
            w2b[...] = w2s[...].astype(BF16)
            nxt = ne_ref[i]

            @pl.when(nxt >= 0)
            def _():
                for cp in weight_copies(nxt):
                    cp.start()

        def run(n_rows):
            slab_rows = pl.ds(0, n_rows * ROW_SLABS)
            x = _load_row_slabs(x_ref.at[slab_rows], n_rows).astype(BF16)
            h1 = _dot(x, w1b[...]) + b1_ref[0, 0]
            gate = jnp.minimum(h1[:, :D_MODEL], SWIGLU_LIMIT)
            up = jnp.clip(h1[:, D_MODEL:], -SWIGLU_LIMIT, SWIGLU_LIMIT)
            act = (up + 1.0) * gate * _sigmoid(SWIGLU_ALPHA * gate)
            y = _dot(act.astype(BF16), w2b[...]) + b2_ref[0, 0]
            _store_row_slabs(y_ref.at[slab_rows], y, n_rows)

        n_valid = bv_ref[i]

        @pl.when(n_valid > MOE_HALF)
        def _():
            run(MOE_BM)

        @pl.when(n_valid <= MOE_HALF)
        def _():
            run(MOE_HALF)
            y_ref[pl.ds(MOE_HALF * ROW_SLABS, MOE_HALF * ROW_SLABS), :] = jnp.zeros(
                (MOE_HALF * ROW_SLABS, LANES), F32)

    @pl.when(i >= nu)
    def _():
        y_ref[...] = jnp.zeros_like(y_ref)


def _moe(xs, tabs, layer, w1, b1, w2, b2, n_tok):
    _, n_blocks = _moe_rows(n_tok)
    d2 = 2 * D_MODEL
    blk = MOE_BM * ROW_SLABS
    grid_spec = pltpu.PrefetchScalarGridSpec(
        num_scalar_prefetch=4,
        grid=(n_blocks,),
        in_specs=[
            pl.BlockSpec((blk, LANES), lambda i, be, nu, *_: (jnp.minimum(i, nu[0] - 1), 0)),
            pl.BlockSpec((1, 1, 1, d2), lambda i, be, *_: (layer, be[i], 0, 0)),
            pl.BlockSpec((1, 1, 1, D_MODEL), lambda i, be, *_: (layer, be[i], 0, 0)),
            pl.BlockSpec(memory_space=pl.ANY),
            pl.BlockSpec(memory_space=pl.ANY),
        ],
        out_specs=pl.BlockSpec((blk, LANES), lambda i, *_: (i, 0)),
        scratch_shapes=[
            pltpu.VMEM((D_MODEL, d2), F32),
            pltpu.VMEM((D_MODEL, D_MODEL), F32),
            pltpu.VMEM((D_MODEL, d2), BF16),
            pltpu.VMEM((D_MODEL, D_MODEL), BF16),
            pltpu.SemaphoreType.DMA((2,)),
        ],
    )
    return pl.pallas_call(
        functools.partial(_moe_kernel, layer),
        grid_spec=grid_spec,
        out_shape=jax.ShapeDtypeStruct((n_blocks * blk, LANES), F32),
        compiler_params=_params(("arbitrary",), VMEM_LIMIT),
        name="moe_experts",
    )(tabs["block_e"], tabs["n_used"], tabs["block_valid"], tabs["next_e"], xs,
      b1.reshape(DEPTH, N_EXPERTS, 1, d2), b2.reshape(DEPTH, N_EXPERTS, 1, D_MODEL), w1, w2)


def _combine_kernel(final, n_tiles, ss_ref, cn_ref, lpos_ref, tg_ref, x_ref, mod_ref, g_ref, ys_hbm, o_ref,
                    cbuf, sem):
    t = pl.program_id(0) * n_tiles + pl.program_id(1)
    n_total = pl.num_programs(0) * n_tiles
    slot = lax.rem(t, MOE_RING)
    ahead = MOE_RING - 1

    def fetch(tile, sl):
        def segment(e, local):
            n = cn_ref[tile * N_EXPERTS + e]

            @pl.when(n > 0)
            def _():
                pltpu.make_async_copy(ys_hbm.at[_slab_rows(ss_ref[tile * N_EXPERTS + e], n)],
                                      cbuf.at[sl, _slab_rows(local, n)], sem.at[sl]).start()
            return local + n
        lax.fori_loop(0, N_EXPERTS, segment, 0)

    @pl.when(t == 0)
    def _():
        for first in range(ahead):
            fetch(first, first)

    @pl.when(t + ahead < n_total)
    def _():
        fetch(t + ahead, lax.rem(t + ahead, MOE_RING))

    pltpu.make_async_copy(ys_hbm.at[_slab_rows(0, TILE_ROWS)], cbuf.at[slot], sem.at[slot]).wait()
    yb = _load_row_slabs(cbuf.at[slot], TILE_ROWS).astype(BF16)

    lpos = lpos_ref[...]
    tg = tg_ref[...]
    riota = lax.broadcasted_iota(I32, (TILE_ROWS, TM), 0)
    gsel = jnp.where(riota == lpos[0:1], tg[0:1], 0.0)
    for k in range(1, TOP_K):
        gsel = gsel + jnp.where(riota == lpos[k:k + 1], tg[k:k + 1], 0.0)
    g_hi = gsel.astype(BF16)
    g_lo = (gsel - g_hi.astype(F32)).astype(BF16)
    f = _dot_tn(g_hi, yb) + _dot_tn(g_lo, yb)

    mod = mod_ref[0, 0]
    xn = x_ref[0] + mod[5:6] * f
    if final:
        ms = jnp.mean(xn * xn, axis=-1, keepdims=True)
        xn = xn * lax.rsqrt(ms + EPS) * g_ref[...]
    o_ref[0] = xn


def _combine(ys, tabs, lpos, top_g, xres, n_tiles, mod, mod_seg, g, final):
    grid_spec = pltpu.PrefetchScalarGridSpec(
        num_scalar_prefetch=2,
        grid=(BATCH, n_tiles),
        in_specs=[
            pl.BlockSpec((8, TM), lambda b, j, *_: (0, b * n_tiles + j)),
            pl.BlockSpec((8, TM), lambda b, j, *_: (0, b * n_tiles + j)),
            pl.BlockSpec((1, TM, D_MODEL), lambda b, j, *_: (b, j, 0)),
            pl.BlockSpec((1, 1, 8, D_MODEL), lambda b, j, *_: (b, mod_seg(j), 0, 0)),
            pl.BlockSpec((1, D_MODEL), lambda b, j, *_: (0, 0)),
            pl.BlockSpec(memory_space=pl.ANY),
        ],
        out_specs=pl.BlockSpec((1, TM, D_MODEL), lambda b, j, *_: (b, j, 0)),
        scratch_shapes=[
            pltpu.VMEM((MOE_RING, TILE_ROWS * ROW_SLABS, LANES), F32),
            pltpu.SemaphoreType.DMA((MOE_RING,)),
        ],
    )
    return pl.pallas_call(
        functools.partial(_combine_kernel, final, n_tiles),
        grid_spec=grid_spec,
        out_shape=jax.ShapeDtypeStruct((BATCH, n_tiles * TM, D_MODEL), F32),
        compiler_params=_params(("arbitrary", "arbitrary"), VMEM_LIMIT),
        name="moe_combine",
    )(tabs["seg_start"], tabs["cnt"], lpos, top_g, xres, mod, g, ys)


def _moe_ffn(h, top_e, top_g, cnt, layer, xres, n_tiles, mod, mod_seg, w1, b1, w2, b2, g, final):
    n_tok = BATCH * n_tiles * TM
    tabs = _route_tables(cnt[:, :, 0], n_tok)
    xs, lpos = _dispatch(h, top_e, tabs, n_tok)
    ys = _moe(xs, tabs, layer, w1, b1, w2, b2, n_tok)
    return _combine(ys, tabs, lpos, top_g, xres, n_tiles, mod, mod_seg, g, final)


def _qkv_kernel(x_ref, mod_ref, g_ref, w_ref, qn_ref, kn_ref, cos_ref, sin_ref, q_ref, k_ref, v_ref):
    mod = mod_ref[0, 0]
    h = _norm_mod(x_ref[0], g_ref[...], mod[0:1], mod[1:2]).astype(BF16)
    qkv = _dot(h, w_ref[...])
    cos = cos_ref[...]
    sin = sin_ref[...]
    lane = lax.broadcasted_iota(I32, (TM, C_HEAD_DIM), 1)
    first_half = (lane % (C_HEAD_DIM // 2)) < (C_HEAD_DIM // 4)
    quarter = C_HEAD_DIM // 4

    def head(xh, gn):
        ms = jnp.mean(xh * xh, axis=-1, keepdims=True)
        y = xh * lax.rsqrt(ms + EPS) * gn
        partner = jnp.where(first_half, pltpu.roll(y, C_HEAD_DIM - quarter, 1), pltpu.roll(y, quarter, 1))
        return y * cos + partner * sin

    qn = qn_ref[...]
    kn = kn_ref[...]
    for hq in range(C_HEADS):
        sl = slice(hq * C_HEAD_DIM, (hq + 1) * C_HEAD_DIM)
        q_ref[0, :, sl] = head(qkv[:, sl], qn).astype(BF16)
    for hk in range(C_KV_HEADS):
        src = slice((C_HEADS + hk) * C_HEAD_DIM, (C_HEADS + hk + 1) * C_HEAD_DIM)
        k_ref[0, :, hk * C_HEAD_DIM:(hk + 1) * C_HEAD_DIM] = head(qkv[:, src], kn).astype(BF16)
    ones = jnp.ones((TM, C_HEAD_DIM), BF16)
    for hk in range(C_KV_HEADS):
        src = slice((C_HEADS + C_KV_HEADS + hk) * C_HEAD_DIM, (C_HEADS + C_KV_HEADS + hk + 1) * C_HEAD_DIM)
        v_ref[0, :, 2 * hk * C_HEAD_DIM:(2 * hk + 1) * C_HEAD_DIM] = qkv[:, src].astype(BF16)
        v_ref[0, :, (2 * hk + 1) * C_HEAD_DIM:(2 * hk + 2) * C_HEAD_DIM] = ones


def _qkv(xall, mod, g, w, qn, kn, cos, sin):
    nj = L_ALL // TM
    kvw = C_KV_HEADS * C_HEAD_DIM
    return pl.pallas_call(
        _qkv_kernel,
        grid=(BATCH, nj),
        in_specs=[
            pl.BlockSpec((1, TM, D_MODEL), lambda b, j: (b, j, 0)),
            pl.BlockSpec((1, 1, 8, D_MODEL), lambda b, j: (b, jnp.minimum(j, 1), 0, 0)),
            pl.BlockSpec((1, D_MODEL), lambda b, j: (0, 0)),
            pl.BlockSpec((D_MODEL, C_QKV), lambda b, j: (0, 0)),
            pl.BlockSpec((1, C_HEAD_DIM), lambda b, j: (0, 0)),
            pl.BlockSpec((1, C_HEAD_DIM), lambda b, j: (0, 0)),
            pl.BlockSpec((TM, C_HEAD_DIM), lambda b, j: (j, 0)),
            pl.BlockSpec((TM, C_HEAD_DIM), lambda b, j: (j, 0)),
        ],
        out_specs=[
            pl.BlockSpec((1, TM, D_MODEL), lambda b, j: (b, j, 0)),
            pl.BlockSpec((1, TM, kvw), lambda b, j: (b, j, 0)),
            pl.BlockSpec((1, TM, 2 * kvw), lambda b, j: (b, j, 0)),
        ],
        out_shape=[
            jax.ShapeDtypeStruct((BATCH, L_ALL, D_MODEL), BF16),
            jax.ShapeDtypeStruct((BATCH, L_ALL, kvw), BF16),
            jax.ShapeDtypeStruct((BATCH, L_ALL, 2 * kvw), BF16),
        ],
        compiler_params=_params(("arbitrary", "arbitrary"), VMEM_LIMIT),
        name="qkv_rope",
    )(xall, mod, g, w, qn, kn, cos, sin)


def _rope_tables():
    rows = SEQ // GRID_W
    row = jnp.repeat(jnp.arange(rows), GRID_W).astype(F32)
    col = jnp.tile(jnp.arange(GRID_W), rows).astype(F32)
    half = C_HEAD_DIM // 2
    inv_freq = ROPE_THETA ** (-jnp.arange(0, half, 2, dtype=F32) / half)
    ar = row[:, None] * inv_freq
    ac = col[:, None] * inv_freq
    cos = jnp.concatenate([jnp.cos(ar), jnp.cos(ar), jnp.cos(ac), jnp.cos(ac)], axis=-1)
    sin = jnp.concatenate([-jnp.sin(ar), jnp.sin(ar), -jnp.sin(ac), jnp.sin(ac)], axis=-1)
    cos = jnp.concatenate([jnp.ones((CTX_LEN, C_HEAD_DIM), F32), cos], axis=0)
    sin = jnp.concatenate([jnp.zeros((CTX_LEN, C_HEAD_DIM), F32), sin], axis=0)
    return cos, sin


def _attn_kernel(q_ref, k_ref, v_ref, o_ref):
    c = (C_HEAD_DIM ** -0.5) * LOG2_E
    def head_cols(h):
        return slice(h * C_HEAD_DIM, (h + 1) * C_HEAD_DIM)

    def scores(h):
        return _dot_nt(q_ref[0, :, head_cols(h)], k_ref[0, :, head_cols(h // C_GROUP)])

    s_next = scores(0)
    for h in range(C_HEADS):
        s = s_next
        if h + 1 < C_HEADS:
            s_next = scores(h + 1)
        g = h // C_GROUP
        m = jnp.max(s, axis=-1, keepdims=True)
        p = jnp.exp2((s - m) * c).astype(BF16)
        oe = _dot(p, v_ref[0, :, 2 * g * C_HEAD_DIM:(2 * g + 2) * C_HEAD_DIM])
        o_ref[0, :, head_cols(h)] = (oe[:, :C_HEAD_DIM] / oe[:, C_HEAD_DIM:]).astype(BF16)


def _attention(q, k, v):
    kvw = C_KV_HEADS * C_HEAD_DIM
    ctx_tiles = CTX_LEN // TM
    return pl.pallas_call(
        _attn_kernel,
        grid=(BATCH, SEQ // TM),
        in_specs=[
            pl.BlockSpec((1, TM, D_MODEL), lambda b, j: (b, j + ctx_tiles, 0)),
            pl.BlockSpec((1, L_ALL, kvw), lambda b, j: (b, 0, 0)),
            pl.BlockSpec((1, L_ALL, 2 * kvw), lambda b, j: (b, 0, 0)),
        ],
        out_specs=pl.BlockSpec((1, TM, D_MODEL), lambda b, j: (b, j, 0)),
        out_shape=jax.ShapeDtypeStruct((BATCH, SEQ, D_MODEL), BF16),
        compiler_params=_params(("arbitrary", "arbitrary"), VMEM_LIMIT),
        name="attention",
    )(q, k, v)


def _mod_table(ada_layer):
    m = ada_layer.reshape(16, N_MOD, D_MODEL)
    m_lat = m[:BATCH]
    m_ctx = jnp.broadcast_to(m[BATCH], (BATCH, N_MOD, D_MODEL))
    t = jnp.stack([m_ctx, m_lat], axis=1)
    return jnp.pad(t, ((0, 0), (0, 0), (0, 8 - N_MOD), (0, 0)))


def kernel(x, c, ctx, c_ctx, ada_w, ada_b, norm_mix, norm_ffn, ab_w_in, ab_gate_w, ab_gate_b, ab_out_norm,
           ab_conv_w, ab_conv_b, ab_w_out, attn_w_qkv, attn_q_norm, attn_k_norm, attn_w_o, router_w,
           router_b, moe_w1, moe_b1, moe_w2, moe_b2, final_norm):
    cond = jnp.zeros((16, D_MODEL), F32).at[:BATCH].set(c).at[BATCH].set(c_ctx)
    ada = _ada(cond, ada_w, ada_b)
    mod0 = _mod_table(ada[0])
    mod1 = _mod_table(ada[1])
    seg_all = lambda j: jnp.minimum(j, 1)
    seg_lat = lambda j: 1
    n_tiles_all = L_ALL // TM
    n_tiles_lat = SEQ // TM
    fnorm = final_norm.reshape(1, D_MODEL)

    w_in = ab_w_in[0]
    lr0 = 2 * A_HEADS * A_DK + A_HEADS * A_DV
    lr1 = lr0 + 2 * A_GATE_RANK
    w_main = jnp.concatenate([w_in[:, :lr0], w_in[:, lr1:]], axis=1).astype(BF16)
    w_lr = jnp.pad(w_in[:, lr0:lr1], ((0, 0), (0, LANES - 2 * A_GATE_RANK))).astype(BF16)
    p, lr = _proj_in(ctx, x, mod0, norm_mix[0:1], w_main, w_lr)

    gw = ab_gate_w[0]
    gw_pad = jnp.zeros((2, LANES, A_HEADS * A_DK), F32)
    gw_pad = gw_pad.at[0, :A_GATE_RANK].set(gw[0]).at[1, A_GATE_RANK:2 * A_GATE_RANK].set(gw[1])
    cw = jnp.concatenate([ab_conv_w[0], ab_conv_b[0][None], jnp.zeros((4, B_WIDTH), F32)], axis=0)
    yg, yc = _gla_conv(p, lr, gw_pad.astype(BF16), ab_gate_b[0].reshape(2, 1, A_HEADS * A_DK),
                       ab_out_norm[0:1], cw)

    w_out = ab_w_out[0].astype(BF16)
    ngla = A_HEADS * A_DV
    xmid, h, top_e, top_g, cnt = _outproj(
        [yg, yc], [w_out[:ngla], w_out[ngla:]], [ctx, x], _ctx_latent_specs(), n_tiles_all, mod0, seg_all,
        norm_ffn[0:1], router_w[0].T, router_b[0].reshape(N_EXPERTS, 1))
    x1 = _moe_ffn(h, top_e, top_g, cnt, 0, xmid, n_tiles_all, mod0, seg_all,
                  moe_w1, moe_b1, moe_w2, moe_b2, fnorm, False)

    cos, sin = _rope_tables()
    q, k, v = _qkv(x1, mod1, norm_mix[1:2], attn_w_qkv[0].astype(BF16), attn_q_norm[0:1],
                   attn_k_norm[0:1], cos, sin)
    o = _attention(q, k, v)
    xmid, h, top_e, top_g, cnt = _outproj(
        [o], [attn_w_o[0].astype(BF16)], [x1],
        [pl.BlockSpec((1, TM, D_MODEL), lambda b, j: (b, j + CTX_LEN // TM, 0))], n_tiles_lat, mod1, seg_lat,
        norm_ffn[1:2], router_w[1].T, router_b[1].reshape(N_EXPERTS, 1))
    return _moe_ffn(h, top_e, top_g, cnt, 1, xmid, n_tiles_lat, mod1, seg_lat,
                    moe_w1, moe_b1, moe_w2, moe_b2, fnorm, True)
```

```python
import functools

import jax
import jax.numpy as jnp
from jax import lax
from jax.experimental import pallas as pl
from jax.experimental.pallas import tpu as pltpu

F32 = jnp.float32
BF16 = jnp.bfloat16
I32 = jnp.int32

D_MODEL = 1024
BATCH = 8
SEQ = 2048
DEPTH = 2
GRID_W = 64
CTX_LEN = 256
L_ALL = CTX_LEN + SEQ
N_MOD = 6
EPS = 1e-6

A_DV = 128
A_HEADS = 4
A_DK = 64
A_GATE_RANK = 16
A_GATE_TAU = 16.0
GLA_CHUNK = 64
B_WIDTH = 512
AB_MAIN = 3072

C_HEAD_DIM = 128
C_HEADS = 8
C_KV_HEADS = 2
C_GROUP = 4
C_QKV = (C_HEADS + 2 * C_KV_HEADS) * C_HEAD_DIM
ROPE_THETA = 10000.0
LOG2_E = 1.4426950408889634

N_EXPERTS = 32
TOP_K = 4
SWIGLU_LIMIT = 7.0
SWIGLU_ALPHA = 1.702

LANES = 128
SUBLANES = 8
ROW_SLABS = D_MODEL // LANES
TM = 256
ATT_TQ = 512
TILE_ROWS = TM * TOP_K
MOE_BM = 512
MOE_HALF = MOE_BM // 2
MOE_RING = 3
VMEM_LIMIT = 56 * 1024 * 1024

NT_DIMS = (((1,), (1,)), ((), ()))
TN_DIMS = (((0,), (0,)), ((), ()))


def _dot(a, b):
    return jnp.dot(a, b, preferred_element_type=F32)


def _dot_nt(a, b):
    return lax.dot_general(a, b, NT_DIMS, preferred_element_type=F32)


def _dot_tn(a, b):
    return lax.dot_general(a, b, TN_DIMS, preferred_element_type=F32)


def _params(sem, vmem=None):
    return pltpu.CompilerParams(dimension_semantics=sem, vmem_limit_bytes=vmem)


def _norm_mod(x, g, shift, scale):
    ms = jnp.mean(x * x, axis=-1, keepdims=True)
    y = x * lax.rsqrt(ms + EPS) * g
    return y * (1.0 + scale) + shift


def _sigmoid(x):
    return 1.0 / (1.0 + jnp.exp(-x))


def _ada_kernel(cond_ref, w_ref, b_ref, o_ref):
    c = cond_ref[...]
    s = (c * _sigmoid(c)).astype(BF16)
    o_ref[0] = _dot(s, w_ref[0].astype(BF16)) + b_ref[0]


def _ada(cond, ada_w, ada_b):
    tn = 1536
    n = N_MOD * D_MODEL
    return pl.pallas_call(
        _ada_kernel,
        grid=(DEPTH, n // tn),
        in_specs=[
            pl.BlockSpec((16, D_MODEL), lambda l, j: (0, 0)),
            pl.BlockSpec((1, D_MODEL, tn), lambda l, j: (l, 0, j)),
            pl.BlockSpec((1, 1, tn), lambda l, j: (l, 0, j)),
        ],
        out_specs=pl.BlockSpec((1, 16, tn), lambda l, j: (l, 0, j)),
        out_shape=jax.ShapeDtypeStruct((DEPTH, 16, n), F32),
        compiler_params=_params(("arbitrary", "arbitrary"), VMEM_LIMIT),
        name="ada",
    )(cond, ada_w, ada_b.reshape(DEPTH, 1, n))


def _ctx_or_latent(ctx_ref, x_ref):
    return jnp.where(pl.program_id(1) == 0, ctx_ref[0], x_ref[0])


def _ctx_latent_specs():
    return [pl.BlockSpec((1, TM, D_MODEL), lambda b, j: (b, 0, 0)),
            pl.BlockSpec((1, TM, D_MODEL), lambda b, j: (b, jnp.maximum(j - 1, 0), 0))]


def _proj_in_kernel(ctx_ref, x_ref, mod_ref, g_ref, w_ref, wlr_ref, p_ref, lr_ref):
    mod = mod_ref[0, 0]
    h = _norm_mod(_ctx_or_latent(ctx_ref, x_ref), g_ref[...], mod[0:1], mod[1:2]).astype(BF16)
    p_ref[0] = _dot(h, w_ref[...]).astype(BF16)
    lr_ref[0] = _dot(h, wlr_ref[...]).astype(BF16)


def _proj_in(ctx, x, mod, g, w_main, w_lr):
    nj = L_ALL // TM
    return pl.pallas_call(
        _proj_in_kernel,
        grid=(BATCH, nj),
        in_specs=_ctx_latent_specs() + [
            pl.BlockSpec((1, 1, 8, D_MODEL), lambda b, j: (b, jnp.minimum(j, 1), 0, 0)),
            pl.BlockSpec((1, D_MODEL), lambda b, j: (0, 0)),
            pl.BlockSpec((D_MODEL, AB_MAIN), lambda b, j: (0, 0)),
            pl.BlockSpec((D_MODEL, LANES), lambda b, j: (0, 0)),
        ],
        out_specs=[
            pl.BlockSpec((1, TM, AB_MAIN), lambda b, j: (b, j, 0)),
            pl.BlockSpec((1, TM, LANES), lambda b, j: (b, j, 0)),
        ],
        out_shape=[
            jax.ShapeDtypeStruct((BATCH, L_ALL, AB_MAIN), BF16),
            jax.ShapeDtypeStruct((BATCH, L_ALL, LANES), BF16),
        ],
        compiler_params=_params(("arbitrary", "arbitrary"), VMEM_LIMIT),
        name="proj_in",
    )(ctx, x, mod, g, w_main, w_lr)


def _log_sigmoid(z):
    return jnp.minimum(z, 0.0) - jnp.log1p(jnp.exp(-jnp.abs(z)))


def _gla_conv_kernel(q_ref, k_ref, v_ref, r_ref, gb_ref, gc_ref, u_ref, lr_ref, gw_ref, gbias_ref,
                     onorm_ref, cw_ref, yg_ref, yc_ref, of_ref, ob_ref, sf_ref, sb_ref, xs_ref):
    ch = GLA_CHUNK
    grp = TM
    n_grp = L_ALL // grp
    cpg = grp // ch
    row = lax.broadcasted_iota(I32, (grp, grp), 0)
    col = lax.broadcasted_iota(I32, (grp, grp), 1)
    same_chunk = (row // ch) == (col // ch)
    tri = (jnp.logical_and(same_chunk, row >= col), jnp.logical_and(same_chunk, col >= row))
    tri_bf = (tri[0].astype(BF16), tri[1].astype(BF16))
    lane = lax.broadcasted_iota(I32, (grp, LANES), 1)
    head_mask = (lane < A_DK, lane >= A_DK)
    chunk_of_row = lax.broadcasted_iota(I32, (grp, LANES), 0) // ch

    sf_ref[...] = jnp.zeros_like(sf_ref)
    sb_ref[...] = jnp.zeros_like(sb_ref)

    def chunk_rows(x, idx):
        return jnp.concatenate(
            [jnp.broadcast_to(x[c * ch + idx:c * ch + idx + 1], (ch, x.shape[1])) for c in range(cpg)], axis=0)

    s_refs = (sf_ref, sb_ref)
    o_refs = (of_ref, ob_ref)
    dirs = (0, 1)
    pairs = [(d, hh) for d in dirs for hh in range(2)]

    def by_chunk_lanes(m, x):
        return jnp.concatenate([jnp.where(jnp.logical_and(m, chunk_of_row == c), x, 0.0) for c in range(cpg)],
                               axis=1).astype(BF16)

    def body(i, carry):
        gidx = (i, jnp.where(i == 0, 0, n_grp - i))
        rows = [pl.ds(pl.multiple_of(g * grp, grp), grp) for g in gidx]
        q = [q_ref[0, r, :].astype(F32) * (A_DK ** -0.5) for r in rows]
        k = [k_ref[0, r, :].astype(F32) for r in rows]
        v = [v_ref[0, r, :] for r in rows]
        z = [_dot(lr_ref[0, rows[d], :].astype(BF16), gw_ref[d]) + gbias_ref[d] for d in dirs]
        a = [_log_sigmoid(z[d]) * (1.0 / A_GATE_TAU) for d in dirs]
        a_hi = [a[d].astype(BF16) for d in dirs]
        a_lo = [(a[d] - a_hi[d].astype(F32)).astype(BF16) for d in dirs]
        cum = [_dot(tri_bf[d], a_hi[d]) + _dot(tri_bf[d], a_lo[d]) for d in dirs]
        ref = (chunk_rows(cum[0], ch // 2 - 1), chunk_rows(cum[1], ch // 2))
        last = (chunk_rows(cum[0], ch - 1), chunk_rows(cum[1], 0))
        order = (range(cpg), range(cpg - 1, -1, -1))
        qe = [q[d] * jnp.exp(cum[d]) for d in dirs]
        qt = [q[d] * jnp.exp(cum[d] - ref[d]) for d in dirs]
        kt = [(k[d] * jnp.exp(ref[d] - cum[d])).astype(BF16) for d in dirs]
        kl = [k[d] * jnp.exp(last[d] - cum[d]) for d in dirs]
        dec = [jnp.exp(last[d]) for d in dirs]
        vh = {(d, hh): v[d][:, hh * A_DV:(hh + 1) * A_DV].astype(BF16) for d, hh in pairs}
        sc = {(d, hh): _dot_nt(jnp.where(head_mask[hh], qt[d], 0.0).astype(BF16), kt[d]) for d, hh in pairs}
        kv = {(d, hh): _dot(vh[d, hh].T, by_chunk_lanes(head_mask[hh], kl[d])) for d, hh in pairs}
        o_intra = {(d, hh): _dot(jnp.where(tri[d], sc[d, hh], 0.0).astype(BF16), vh[d, hh]) for d, hh in pairs}
        for d, hh in pairs:
            st = s_refs[d][hh]
            states = [None] * cpg
            for c in order[d]:
                states[c] = st
                st = st * dec[d][c * ch:c * ch + 1] + kv[d, hh][:, c * LANES:(c + 1) * LANES]
            s_refs[d][hh] = st
            o_inter = _dot_nt(by_chunk_lanes(head_mask[hh], qe[d]), jnp.concatenate(states, axis=1).astype(BF16))
            o_refs[d][rows[d], hh * A_DV:(hh + 1) * A_DV] = o_intra[d, hh] + o_inter
        return carry

    lax.fori_loop(0, n_grp, body, 0)

    pad = SUBLANES
    xs_ref[0:pad, :] = jnp.zeros((pad, xs_ref.shape[1]), F32)
    xs_ref[pad + L_ALL:, :] = jnp.zeros((pad, xs_ref.shape[1]), F32)
    xs_ref[pad:pad + L_ALL, :] = gc_ref[0].astype(F32) * u_ref[0].astype(F32)

    cw = cw_ref[...]
    onorm = onorm_ref[...]
    trow = lax.broadcasted_iota(I32, (TM, 1), 0)
    for ti in range(L_ALL // TM):
        s0 = ti * TM
        rows = slice(s0, s0 + TM)
        o = of_ref[rows, :] + ob_ref[rows, :]
        r = r_ref[0, rows, :].astype(F32)
        parts = []
        for hh in range(2):
            oh = o[:, hh * A_DV:(hh + 1) * A_DV]
            ms = jnp.mean(oh * oh, axis=-1, keepdims=True)
            parts.append(oh * lax.rsqrt(ms + EPS) * onorm[:, hh * A_DV:(hh + 1) * A_DV])
        on = jnp.concatenate(parts, axis=1)
        yg_ref[0, rows, :] = (on * (r * _sigmoid(r))).astype(BF16)
        xm1 = xs_ref[pad + s0 - 1:pad + s0 - 1 + TM, :]
        x0 = xs_ref[pad + s0:pad + s0 + TM, :]
        xp1 = xs_ref[pad + s0 + 1:pad + s0 + 1 + TM, :]
        if s0 + TM == CTX_LEN:
            xp1 = jnp.where(trow == TM - 1, 0.0, xp1)
        if s0 == CTX_LEN:
            xm1 = jnp.where(trow == 0, 0.0, xm1)
        conv = cw[0:1] * xm1 + cw[1:2] * x0 + cw[2:3] * xp1 + cw[3:4]
        yc_ref[0, rows, :] = (gb_ref[0, rows, :].astype(F32) * conv).astype(BF16)


def _gla_conv(p, lr, gw_pad, gbias, onorm, cw):
    hw = 2 * A_DV

    def pspec(width, base):
        return pl.BlockSpec((1, L_ALL, width), lambda b, i: (b, 0, base + i))

    return pl.pallas_call(
        _gla_conv_kernel,
        grid=(BATCH, A_HEADS // 2),
        in_specs=[
            pspec(LANES, 0),
            pspec(LANES, 2),
            pspec(hw, 2),
            pspec(hw, 4),
            pspec(hw, 6),
            pspec(hw, 8),
            pspec(hw, 10),
            pl.BlockSpec((1, L_ALL, LANES), lambda b, i: (b, 0, 0)),
            pl.BlockSpec((2, LANES, LANES), lambda b, i: (0, 0, i)),
            pl.BlockSpec((2, 1, LANES), lambda b, i: (0, 0, i)),
            pl.BlockSpec((1, hw), lambda b, i: (0, i)),
            pl.BlockSpec((8, hw), lambda b, i: (0, i)),
        ],
        out_specs=[
            pl.BlockSpec((1, L_ALL, hw), lambda b, i: (b, 0, i)),
            pl.BlockSpec((1, L_ALL, hw), lambda b, i: (b, 0, i)),
        ],
        out_shape=[
            jax.ShapeDtypeStruct((BATCH, L_ALL, A_HEADS * A_DV), BF16),
            jax.ShapeDtypeStruct((BATCH, L_ALL, B_WIDTH), BF16),
        ],
        scratch_shapes=[
            pltpu.VMEM((L_ALL, hw), F32),
            pltpu.VMEM((L_ALL, hw), F32),
            pltpu.VMEM((2, A_DV, LANES), F32),
            pltpu.VMEM((2, A_DV, LANES), F32),
            pltpu.VMEM((L_ALL + 2 * SUBLANES, hw), F32),
        ],
        compiler_params=_params(("arbitrary", "arbitrary"), VMEM_LIMIT),
        name="gla_conv",
    )(p, p, p, p, p, p, p, lr, gw_pad, gbias, onorm, cw)


def _outproj_kernel(n_in, n_res, *refs):
    y_refs = refs[:n_in]
    w_refs = refs[n_in:2 * n_in]
    res_refs = refs[2 * n_in:2 * n_in + n_res]
    mod_ref, g_ref, rwt_ref, rb_ref, xo_ref, h_ref, te_ref, tg_ref, cnt_ref = refs[2 * n_in + n_res:]
    is_ctx = pl.program_id(1) == 0
    mod = mod_ref[0, 0]
    wt = rwt_ref[...]
    wb = wt.astype(BF16)
    wl = (wt - wb.astype(F32)).astype(BF16)
    eidx = lax.broadcasted_iota(I32, (N_EXPERTS, LANES), 0)
    cnt = jnp.zeros((N_EXPERTS, 1), I32)
    groups = [slice(r0, r0 + LANES) for r0 in range(0, TM, LANES)]
    accs = []
    for rows in groups:
        acc = _dot(y_refs[0][0, rows, :], w_refs[0][...])
        for i in range(1, n_in):
            acc = acc + _dot(y_refs[i][0, rows, :], w_refs[i][...])
        accs.append(acc)
    hs = []
    for rows, acc in zip(groups, accs):
        xres = res_refs[0][0, rows, :]
        if n_res == 2:
            xres = jnp.where(is_ctx, xres, res_refs[1][0, rows, :])
        xn = xres + mod[2:3] * acc
        xo_ref[0, rows, :] = xn
        h = _norm_mod(xn, g_ref[...], mod[3:4], mod[4:5])
        h_ref[rows, :] = h.astype(BF16)
        hs.append(h)
    all_logits = []
    for h in hs:
        hb = h.astype(BF16)
        hl = (h - hb.astype(F32)).astype(BF16)
        all_logits.append(_dot_nt(wb, hb) + _dot_nt(wb, hl) + _dot_nt(wl, hb) + rb_ref[...])
    for rows, logits in zip(groups, all_logits):
        cur = logits
        vals, idxs = [], []
        for _ in range(TOP_K):
            m = jnp.max(cur, axis=0, keepdims=True)
            sel = jnp.min(jnp.where(cur == m, eidx, N_EXPERTS), axis=0, keepdims=True)
            vals.append(m)
            idxs.append(sel)
            cur = jnp.where(eidx == sel, -jnp.inf, cur)
        ex = [jnp.exp(v - vals[0]) for v in vals]
        den = ex[0] + ex[1] + ex[2] + ex[3]
        zi = jnp.zeros_like(idxs[0])
        zf = jnp.zeros_like(den)
        te_ref[:, rows] = jnp.concatenate(idxs + [zi] * (8 - TOP_K), axis=0)
        tg_ref[:, rows] = jnp.concatenate([e / den for e in ex] + [zf] * (8 - TOP_K), axis=0)
        onehot = jnp.where(eidx == idxs[0], 1, 0)
        for k in range(1, TOP_K):
            onehot = onehot + jnp.where(eidx == idxs[k], 1, 0)
        cnt = cnt + jnp.sum(onehot, axis=1, keepdims=True)
    cnt_ref[0] = jnp.broadcast_to(cnt, (N_EXPERTS, LANES))


def _outproj(ys, ws, res, res_specs, n_tiles, mod, mod_seg, g, rwt, rb):
    n_in = len(ys)
    n_tok = BATCH * n_tiles * TM
    in_specs = []
    for y in ys:
        in_specs.append(pl.BlockSpec((1, TM, y.shape[2]), lambda b, j: (b, j, 0)))
    for w in ws:
        in_specs.append(pl.BlockSpec(w.shape, lambda b, j: (0, 0)))
    in_specs += list(res_specs) + [
        pl.BlockSpec((1, 1, 8, D_MODEL), lambda b, j: (b, mod_seg(j), 0, 0)),
        pl.BlockSpec((1, D_MODEL), lambda b, j: (0, 0)),
        pl.BlockSpec((N_EXPERTS, D_MODEL), lambda b, j: (0, 0)),
        pl.BlockSpec((N_EXPERTS, 1), lambda b, j: (0, 0)),
    ]
    return pl.pallas_call(
        functools.partial(_outproj_kernel, n_in, len(res)),
        grid=(BATCH, n_tiles),
        in_specs=in_specs,
        out_specs=[
            pl.BlockSpec((1, TM, D_MODEL), lambda b, j: (b, j, 0)),
            pl.BlockSpec((TM, D_MODEL), lambda b, j: (b * n_tiles + j, 0)),
            pl.BlockSpec((8, TM), lambda b, j: (0, b * n_tiles + j)),
            pl.BlockSpec((8, TM), lambda b, j: (0, b * n_tiles + j)),
            pl.BlockSpec((1, N_EXPERTS, LANES), lambda b, j: (b * n_tiles + j, 0, 0)),
        ],
        out_shape=[
            jax.ShapeDtypeStruct((BATCH, n_tiles * TM, D_MODEL), F32),
            jax.ShapeDtypeStruct((n_tok, D_MODEL), BF16),
            jax.ShapeDtypeStruct((8, n_tok), I32),
            jax.ShapeDtypeStruct((8, n_tok), F32),
            jax.ShapeDtypeStruct((BATCH * n_tiles, N_EXPERTS, LANES), I32),
        ],
        compiler_params=_params(("arbitrary", "arbitrary"), VMEM_LIMIT),
        name="outproj_router",
    )(*ys, *ws, *res, mod, g, rwt, rb)


def _moe_rows(n_tok):
    n_assign = n_tok * TOP_K
    n_blocks = -(-(n_assign + N_EXPERTS * (MOE_BM - 1)) // MOE_BM)
    return n_assign, n_blocks


def _route_tables(cnt, n_tok):
    _, n_blocks = _moe_rows(n_tok)
    counts = jnp.sum(cnt, axis=0)
    padded = (counts + MOE_BM - 1) // MOE_BM * MOE_BM
    pad_end = jnp.cumsum(padded)
    block_start = pad_end - padded
    seg_start = block_start[None, :] + jnp.cumsum(cnt, axis=0) - cnt
    blk_row = jnp.arange(n_blocks, dtype=I32) * MOE_BM
    block_e = jnp.minimum(jnp.sum((pad_end[None, :] <= blk_row[:, None]).astype(I32), axis=1), N_EXPERTS - 1)
    n_used = (pad_end[-1] // MOE_BM).reshape(1)
    eids = jnp.arange(N_EXPERTS, dtype=I32)
    later = jnp.where(jnp.logical_and(eids[None, :] > eids[:, None], counts[None, :] > 0), eids[None, :], N_EXPERTS)
    nxt = jnp.min(later, axis=1)
    next_e = jnp.where(nxt == N_EXPERTS, -1, nxt)
    return dict(block_e=block_e.astype(I32), n_used=n_used.astype(I32),
                row_end=(block_start + counts).astype(I32), next_e=next_e.astype(I32),
                seg_start=seg_start.reshape(-1).astype(I32), cnt=cnt.reshape(-1).astype(I32),
                pad_start=(block_start + counts).astype(I32), pad_len=(padded - counts).astype(I32))


def _store_row_slabs(ref, val, n_rows):
    for s in range(ROW_SLABS):
        ref[pl.ds(s, n_rows, stride=ROW_SLABS), :] = val[:, s * LANES:(s + 1) * LANES]


def _load_row_slabs(ref, n_rows):
    return jnp.concatenate(
        [ref[pl.ds(s, n_rows, stride=ROW_SLABS), :] for s in range(ROW_SLABS)], axis=1)


def _slab_rows(start, n):
    return pl.ds(pl.multiple_of(start * ROW_SLABS, ROW_SLABS), n * ROW_SLABS)


def _local_positions(te):
    eidx = lax.broadcasted_iota(I32, (N_EXPERTS, TM), 0)
    hits = [te[k:k + 1] == eidx for k in range(TOP_K)]
    onehot = jnp.where(hits[0], 1.0, 0.0)
    for k in range(1, TOP_K):
        onehot = onehot + jnp.where(hits[k], 1.0, 0.0)
    mb = onehot.astype(BF16)
    trow = lax.broadcasted_iota(I32, (TM, TM), 0)
    tcol = lax.broadcasted_iota(I32, (TM, TM), 1)
    before = _dot(mb, (trow < tcol).astype(BF16))
    totals = _dot(mb, jnp.ones((TM, TM), BF16))
    erow = lax.broadcasted_iota(I32, (N_EXPERTS, N_EXPERTS), 0)
    ecol = lax.broadcasted_iota(I32, (N_EXPERTS, N_EXPERTS), 1)
    first = _dot((ecol < erow).astype(BF16), totals.astype(BF16))
    base = first + before
    return [jnp.sum(jnp.where(hits[k], base, 0.0), axis=0, keepdims=True).astype(I32) for k in range(TOP_K)]


def _dispatch_kernel(n_blocks, ss_ref, cn_ref, ps_ref, pl_ref, nu_ref, h_ref, te_ref, xs_hbm, lpos_ref,
                     sbuf, zbuf, sem):
    j = pl.program_id(0)
    n_tiles = pl.num_programs(0)

    def zero_fill(act):
        def per_expert(e, carry):
            n = pl_ref[e]

            @pl.when(n > 0)
            def _():
                act(pltpu.make_async_copy(zbuf.at[_slab_rows(0, n)], xs_hbm.at[_slab_rows(ps_ref[e], n)],
                                          sem.at[MOE_RING]))
            return carry
        lax.fori_loop(0, N_EXPERTS, per_expert, 0)

        def per_block(b, carry):
            act(pltpu.make_async_copy(zbuf, xs_hbm.at[_slab_rows(b * MOE_BM, MOE_BM)], sem.at[MOE_RING]))
            return carry
        lax.fori_loop(nu_ref[0], n_blocks, per_block, 0)

    @pl.when(j == 0)
    def _():
        zbuf[...] = jnp.zeros_like(zbuf)
        zero_fill(lambda cp: cp.start())

    lpos = _local_positions(te_ref[...])
    zi = jnp.zeros_like(lpos[0])
    lpos_ref[...] = jnp.concatenate(lpos + [zi] * (8 - TOP_K), axis=0)
    riota = lax.broadcasted_iota(I32, (TILE_ROWS, TM), 0)
    hit = riota == lpos[0]
    for k in range(1, TOP_K):
        hit = jnp.logical_or(hit, riota == lpos[k])
    perm = jnp.where(hit, 1.0, 0.0).astype(BF16)
    hb = h_ref[...].astype(BF16)
    slot = lax.rem(j, MOE_RING)
    buf = sbuf.at[slot]

    def rows_copy(sl):
        return pltpu.make_async_copy(sbuf.at[sl], xs_hbm.at[_slab_rows(0, TILE_ROWS)], sem.at[sl])

    @pl.when(j >= MOE_RING)
    def _():
        rows_copy(slot).wait()

    for s in range(0, ROW_SLABS, 2):
        xl = _dot(perm, hb[:, s * LANES:(s + 2) * LANES])
        buf[pl.ds(s, TILE_ROWS, stride=ROW_SLABS), :] = xl[:, :LANES]
        buf[pl.ds(s + 1, TILE_ROWS, stride=ROW_SLABS), :] = xl[:, LANES:]

    def segment(e, local):
        n = cn_ref[j * N_EXPERTS + e]

        @pl.when(n > 0)
        def _():
            pltpu.make_async_copy(buf.at[_slab_rows(local, n)],
                                  xs_hbm.at[_slab_rows(ss_ref[j * N_EXPERTS + e], n)], sem.at[slot]).start()
        return local + n
    lax.fori_loop(0, N_EXPERTS, segment, 0)

    @pl.when(j == n_tiles - 1)
    def _():
        for back in range(MOE_RING):
            rows_copy(lax.rem(j + MOE_RING - back, MOE_RING)).wait()
        zero_fill(lambda cp: cp.wait())


def _dispatch(h, top_e, tabs, n_tok):
    _, n_blocks = _moe_rows(n_tok)
    n_tiles = n_tok // TM
    grid_spec = pltpu.PrefetchScalarGridSpec(
        num_scalar_prefetch=5,
        grid=(n_tiles,),
        in_specs=[
            pl.BlockSpec((TM, D_MODEL), lambda j, *_: (j, 0)),
            pl.BlockSpec((8, TM), lambda j, *_: (0, j)),
        ],
        out_specs=[
            pl.BlockSpec(memory_space=pl.ANY),
            pl.BlockSpec((8, TM), lambda j, *_: (0, j)),
        ],
        scratch_shapes=[
            pltpu.VMEM((MOE_RING, TILE_ROWS * ROW_SLABS, LANES), F32),
            pltpu.VMEM((MOE_BM * ROW_SLABS, LANES), F32),
            pltpu.SemaphoreType.DMA((MOE_RING + 1,)),
        ],
    )
    return pl.pallas_call(
        functools.partial(_dispatch_kernel, n_blocks),
        grid_spec=grid_spec,
        out_shape=[
            jax.ShapeDtypeStruct((n_blocks * MOE_BM * ROW_SLABS, LANES), F32),
            jax.ShapeDtypeStruct((8, n_tok), I32),
        ],
        compiler_params=_params(("arbitrary",), VMEM_LIMIT),
        name="moe_dispatch",
    )(tabs["seg_start"], tabs["cnt"], tabs["pad_start"], tabs["pad_len"], tabs["n_used"], h, top_e)


def _moe_kernel(layer, be_ref, nu_ref, end_ref, ne_ref, x_ref, b1_ref, b2_ref, w1_hbm, w2_hbm, y_ref,
                w1s, w2s, w1b, w2b, sem):
    i = pl.program_id(0)
    nu = nu_ref[0]

    def weight_copies(e):
        return (pltpu.make_async_copy(w1_hbm.at[layer, e], w1s, sem.at[0]),
                pltpu.make_async_copy(w2_hbm.at[layer, e], w2s, sem.at[1]))

    @pl.when(i < nu)
    def _():
        e = be_ref[i]

        @pl.when(i == 0)
        def _():
            for cp in weight_copies(e):
                cp.start()

        @pl.when(jnp.logical_or(i == 0, e != be_ref[jnp.maximum(i - 1, 0)]))
        def _():
            for cp in weight_copies(e):
                cp.wait()
            w1b[...] = w1s[...].astype(BF16)
            w2b[...] = w2s[...].astype(BF16)
            nxt = ne_ref[e]

            @pl.when(nxt >= 0)
            def _():
                for cp in weight_copies(nxt):
                    cp.start()

        def run(n_rows):
            slab_rows = pl.ds(0, n_rows * ROW_SLABS)
            x = _load_row_slabs(x_ref.at[slab_rows], n_rows).astype(BF16)
            h1 = _dot(x, w1b[...]) + b1_ref[0, 0]
            gate = jnp.minimum(h1[:, :D_MODEL], SWIGLU_LIMIT)
            up = jnp.clip(h1[:, D_MODEL:], -SWIGLU_LIMIT, SWIGLU_LIMIT)
            act = (up + 1.0) * gate * _sigmoid(SWIGLU_ALPHA * gate)
            y = _dot(act.astype(BF16), w2b[...]) + b2_ref[0, 0]
            _store_row_slabs(y_ref.at[slab_rows], y, n_rows)

        n_valid = end_ref[e] - i * MOE_BM

        @pl.when(n_valid > MOE_HALF)
        def _():
            run(MOE_BM)

        @pl.when(n_valid <= MOE_HALF)
        def _():
            run(MOE_HALF)
            y_ref[pl.ds(MOE_HALF * ROW_SLABS, MOE_HALF * ROW_SLABS), :] = jnp.zeros(
                (MOE_HALF * ROW_SLABS, LANES), F32)

    @pl.when(i >= nu)
    def _():
        y_ref[...] = jnp.zeros_like(y_ref)


def _moe(xs, tabs, layer, w1, b1, w2, b2, n_tok):
    _, n_blocks = _moe_rows(n_tok)
    d2 = 2 * D_MODEL
    blk = MOE_BM * ROW_SLABS
    grid_spec = pltpu.PrefetchScalarGridSpec(
        num_scalar_prefetch=4,
        grid=(n_blocks,),
        in_specs=[
            pl.BlockSpec((blk, LANES), lambda i, be, nu, *_: (jnp.minimum(i, nu[0] - 1), 0)),
            pl.BlockSpec((1, 1, 1, d2), lambda i, be, *_: (layer, be[i], 0, 0)),
            pl.BlockSpec((1, 1, 1, D_MODEL), lambda i, be, *_: (layer, be[i], 0, 0)),
            pl.BlockSpec(memory_space=pl.ANY),
            pl.BlockSpec(memory_space=pl.ANY),
        ],
        out_specs=pl.BlockSpec((blk, LANES), lambda i, *_: (i, 0)),
        scratch_shapes=[
            pltpu.VMEM((D_MODEL, d2), F32),
            pltpu.VMEM((D_MODEL, D_MODEL), F32),
            pltpu.VMEM((D_MODEL, d2), BF16),
            pltpu.VMEM((D_MODEL, D_MODEL), BF16),
            pltpu.SemaphoreType.DMA((2,)),
        ],
    )
    return pl.pallas_call(
        functools.partial(_moe_kernel, layer),
        grid_spec=grid_spec,
        out_shape=jax.ShapeDtypeStruct((n_blocks * blk, LANES), F32),
        compiler_params=_params(("arbitrary",), VMEM_LIMIT),
        name="moe_experts",
    )(tabs["block_e"], tabs["n_used"], tabs["row_end"], tabs["next_e"], xs,
      b1.reshape(DEPTH, N_EXPERTS, 1, d2), b2.reshape(DEPTH, N_EXPERTS, 1, D_MODEL), w1, w2)


def _combine_kernel(final, n_tiles, ss_ref, cn_ref, lpos_ref, tg_ref, x_ref, mod_ref, g_ref, ys_hbm, o_ref,
                    cbuf, sem):
    t = pl.program_id(0) * n_tiles + pl.program_id(1)
    n_total = pl.num_programs(0) * n_tiles
    slot = lax.rem(t, MOE_RING)
    ahead = MOE_RING - 1

    def fetch(tile, sl):
        def segment(e, local):
            n = cn_ref[tile * N_EXPERTS + e]

            @pl.when(n > 0)
            def _():
                pltpu.make_async_copy(ys_hbm.at[_slab_rows(ss_ref[tile * N_EXPERTS + e], n)],
                                      cbuf.at[sl, _slab_rows(local, n)], sem.at[sl]).start()
            return local + n
        lax.fori_loop(0, N_EXPERTS, segment, 0)

    @pl.when(t == 0)
    def _():
        for first in range(ahead):
            fetch(first, first)

    @pl.when(t + ahead < n_total)
    def _():
        fetch(t + ahead, lax.rem(t + ahead, MOE_RING))

    pltpu.make_async_copy(ys_hbm.at[_slab_rows(0, TILE_ROWS)], cbuf.at[slot], sem.at[slot]).wait()
    yb = _load_row_slabs(cbuf.at[slot], TILE_ROWS).astype(BF16)

    lpos = lpos_ref[...]
    tg = tg_ref[...]
    riota = lax.broadcasted_iota(I32, (TILE_ROWS, TM), 0)
    gsel = jnp.where(riota == lpos[0:1], tg[0:1], 0.0)
    for k in range(1, TOP_K):
        gsel = gsel + jnp.where(riota == lpos[k:k + 1], tg[k:k + 1], 0.0)
    g_hi = gsel.astype(BF16)
    g_lo = (gsel - g_hi.astype(F32)).astype(BF16)
    f = _dot_tn(g_hi, yb) + _dot_tn(g_lo, yb)

    mod = mod_ref[0, 0]
    xn = x_ref[0] + mod[5:6] * f
    if final:
        ms = jnp.mean(xn * xn, axis=-1, keepdims=True)
        xn = xn * lax.rsqrt(ms + EPS) * g_ref[...]
    o_ref[0] = xn


def _combine(ys, tabs, lpos, top_g, xres, n_tiles, mod, mod_seg, g, final):
    grid_spec = pltpu.PrefetchScalarGridSpec(
        num_scalar_prefetch=2,
        grid=(BATCH, n_tiles),
        in_specs=[
            pl.BlockSpec((8, TM), lambda b, j, *_: (0, b * n_tiles + j)),
            pl.BlockSpec((8, TM), lambda b, j, *_: (0, b * n_tiles + j)),
            pl.BlockSpec((1, TM, D_MODEL), lambda b, j, *_: (b, j, 0)),
            pl.BlockSpec((1, 1, 8, D_MODEL), lambda b, j, *_: (b, mod_seg(j), 0, 0)),
            pl.BlockSpec((1, D_MODEL), lambda b, j, *_: (0, 0)),
            pl.BlockSpec(memory_space=pl.ANY),
        ],
        out_specs=pl.BlockSpec((1, TM, D_MODEL), lambda b, j, *_: (b, j, 0)),
        scratch_shapes=[
            pltpu.VMEM((MOE_RING, TILE_ROWS * ROW_SLABS, LANES), F32),
            pltpu.SemaphoreType.DMA((MOE_RING,)),
        ],
    )
    return pl.pallas_call(
        functools.partial(_combine_kernel, final, n_tiles),
        grid_spec=grid_spec,
        out_shape=jax.ShapeDtypeStruct((BATCH, n_tiles * TM, D_MODEL), F32),
        compiler_params=_params(("arbitrary", "arbitrary"), VMEM_LIMIT),
        name="moe_combine",
    )(tabs["seg_start"], tabs["cnt"], lpos, top_g, xres, mod, g, ys)


def _moe_ffn(h, top_e, top_g, cnt, layer, xres, n_tiles, mod, mod_seg, w1, b1, w2, b2, g, final):
    n_tok = BATCH * n_tiles * TM
    tabs = _route_tables(cnt[:, :, 0], n_tok)
    xs, lpos = _dispatch(h, top_e, tabs, n_tok)
    ys = _moe(xs, tabs, layer, w1, b1, w2, b2, n_tok)
    return _combine(ys, tabs, lpos, top_g, xres, n_tiles, mod, mod_seg, g, final)


def _qkv_kernel(x_ref, mod_ref, g_ref, w_ref, qn_ref, kn_ref, cos_ref, sin_ref, q_ref, k_ref, v_ref):
    mod = mod_ref[0, 0]
    h = _norm_mod(x_ref[0], g_ref[...], mod[0:1], mod[1:2]).astype(BF16)
    qkv = _dot(h, w_ref[...])
    cos = cos_ref[...]
    sin = sin_ref[...]

    def head(xh, gn):
        ms = jnp.mean(xh * xh, axis=-1, keepdims=True)
        y = xh * lax.rsqrt(ms + EPS) * gn
        return y * cos + pltpu.roll(y, C_HEAD_DIM // 2, 1) * sin

    qn = qn_ref[...]
    kn = kn_ref[...]
    for hq in range(C_HEADS):
        sl = slice(hq * C_HEAD_DIM, (hq + 1) * C_HEAD_DIM)
        q_ref[0, :, sl] = head(qkv[:, sl], qn).astype(BF16)
    for hk in range(C_KV_HEADS):
        src = slice((C_HEADS + hk) * C_HEAD_DIM, (C_HEADS + hk + 1) * C_HEAD_DIM)
        k_ref[0, :, hk * C_HEAD_DIM:(hk + 1) * C_HEAD_DIM] = head(qkv[:, src], kn).astype(BF16)
    ones = jnp.ones((TM, C_HEAD_DIM), BF16)
    for hk in range(C_KV_HEADS):
        src = slice((C_HEADS + C_KV_HEADS + hk) * C_HEAD_DIM, (C_HEADS + C_KV_HEADS + hk + 1) * C_HEAD_DIM)
        v_ref[0, :, 2 * hk * C_HEAD_DIM:(2 * hk + 1) * C_HEAD_DIM] = qkv[:, src].astype(BF16)
        v_ref[0, :, (2 * hk + 1) * C_HEAD_DIM:(2 * hk + 2) * C_HEAD_DIM] = ones


def _qkv(xall, mod, g, w, qn, kn, cos, sin):
    nj = L_ALL // TM
    kvw = C_KV_HEADS * C_HEAD_DIM
    return pl.pallas_call(
        _qkv_kernel,
        grid=(BATCH, nj),
        in_specs=[
            pl.BlockSpec((1, TM, D_MODEL), lambda b, j: (b, j, 0)),
            pl.BlockSpec((1, 1, 8, D_MODEL), lambda b, j: (b, jnp.minimum(j, 1), 0, 0)),
            pl.BlockSpec((1, D_MODEL), lambda b, j: (0, 0)),
            pl.BlockSpec((D_MODEL, C_QKV), lambda b, j: (0, 0)),
            pl.BlockSpec((1, C_HEAD_DIM), lambda b, j: (0, 0)),
            pl.BlockSpec((1, C_HEAD_DIM), lambda b, j: (0, 0)),
            pl.BlockSpec((TM, C_HEAD_DIM), lambda b, j: (j, 0)),
            pl.BlockSpec((TM, C_HEAD_DIM), lambda b, j: (j, 0)),
        ],
        out_specs=[
            pl.BlockSpec((1, TM, D_MODEL), lambda b, j: (b, jnp.maximum(j - 1, 0), 0)),
            pl.BlockSpec((1, TM, kvw), lambda b, j: (b, j, 0)),
            pl.BlockSpec((1, TM, 2 * kvw), lambda b, j: (b, j, 0)),
        ],
        out_shape=[
            jax.ShapeDtypeStruct((BATCH, SEQ, D_MODEL), BF16),
            jax.ShapeDtypeStruct((BATCH, L_ALL, kvw), BF16),
            jax.ShapeDtypeStruct((BATCH, L_ALL, 2 * kvw), BF16),
        ],
        compiler_params=_params(("arbitrary", "arbitrary"), VMEM_LIMIT),
        name="qkv_rope",
    )(xall, mod, g, w, qn, kn, cos, sin)


def _rope_perm():
    qd = C_HEAD_DIM // 4
    order = (0, 2, 1, 3)
    return [blk * qd + i for blk in order for i in range(qd)]


def _rope_tables():
    rows = SEQ // GRID_W
    row = jnp.repeat(jnp.arange(rows), GRID_W).astype(F32)
    col = jnp.tile(jnp.arange(GRID_W), rows).astype(F32)
    half = C_HEAD_DIM // 2
    inv_freq = ROPE_THETA ** (-jnp.arange(0, half, 2, dtype=F32) / half)
    ar = row[:, None] * inv_freq
    ac = col[:, None] * inv_freq
    cos = jnp.concatenate([jnp.cos(ar), jnp.cos(ac), jnp.cos(ar), jnp.cos(ac)], axis=-1)
    sin = jnp.concatenate([-jnp.sin(ar), -jnp.sin(ac), jnp.sin(ar), jnp.sin(ac)], axis=-1)
    cos = jnp.concatenate([jnp.ones((CTX_LEN, C_HEAD_DIM), F32), cos], axis=0)
    sin = jnp.concatenate([jnp.zeros((CTX_LEN, C_HEAD_DIM), F32), sin], axis=0)
    return cos, sin


def _attn_kernel(q_ref, k_ref, v_ref, o_ref):
    c = (C_HEAD_DIM ** -0.5) * LOG2_E
    def head_cols(h):
        return slice(h * C_HEAD_DIM, (h + 1) * C_HEAD_DIM)

    def scores(h):
        return _dot_nt(q_ref[0, :, head_cols(h)], k_ref[0, :, head_cols(h // C_GROUP)])

    s_next = scores(0)
    for h in range(C_HEADS):
        s = s_next
        if h + 1 < C_HEADS:
            s_next = scores(h + 1)
        g = h // C_GROUP
        m = jnp.max(s, axis=-1, keepdims=True)
        p = jnp.exp2((s - m) * c).astype(BF16)
        oe = _dot(p, v_ref[0, :, 2 * g * C_HEAD_DIM:(2 * g + 2) * C_HEAD_DIM])
        o_ref[0, :, head_cols(h)] = (oe[:, :C_HEAD_DIM] / oe[:, C_HEAD_DIM:]).astype(BF16)


def _attention(q, k, v):
    kvw = C_KV_HEADS * C_HEAD_DIM
    return pl.pallas_call(
        _attn_kernel,
        grid=(BATCH, SEQ // ATT_TQ),
        in_specs=[
            pl.BlockSpec((1, ATT_TQ, D_MODEL), lambda b, j: (b, j, 0)),
            pl.BlockSpec((1, L_ALL, kvw), lambda b, j: (b, 0, 0)),
            pl.BlockSpec((1, L_ALL, 2 * kvw), lambda b, j: (b, 0, 0)),
        ],
        out_specs=pl.BlockSpec((1, ATT_TQ, D_MODEL), lambda b, j: (b, j, 0)),
        out_shape=jax.ShapeDtypeStruct((BATCH, SEQ, D_MODEL), BF16),
        compiler_params=_params(("arbitrary", "arbitrary"), VMEM_LIMIT),
        name="attention",
    )(q, k, v)


def _mod_table(ada_layer):
    m = ada_layer.reshape(16, N_MOD, D_MODEL)
    m_lat = m[:BATCH]
    m_ctx = jnp.broadcast_to(m[BATCH], (BATCH, N_MOD, D_MODEL))
    t = jnp.stack([m_ctx, m_lat], axis=1)
    return jnp.pad(t, ((0, 0), (0, 0), (0, 8 - N_MOD), (0, 0)))


def kernel(x, c, ctx, c_ctx, ada_w, ada_b, norm_mix, norm_ffn, ab_w_in, ab_gate_w, ab_gate_b, ab_out_norm,
           ab_conv_w, ab_conv_b, ab_w_out, attn_w_qkv, attn_q_norm, attn_k_norm, attn_w_o, router_w,
           router_b, moe_w1, moe_b1, moe_w2, moe_b2, final_norm):
    cond = jnp.zeros((16, D_MODEL), F32).at[:BATCH].set(c).at[BATCH].set(c_ctx)
    ada = _ada(cond, ada_w, ada_b)
    mod0 = _mod_table(ada[0])
    mod1 = _mod_table(ada[1])
    seg_all = lambda j: jnp.minimum(j, 1)
    seg_lat = lambda j: 1
    n_tiles_all = L_ALL // TM
    n_tiles_lat = SEQ // TM
    fnorm = final_norm.reshape(1, D_MODEL)

    w_in = ab_w_in[0]
    lr0 = 2 * A_HEADS * A_DK + A_HEADS * A_DV
    lr1 = lr0 + 2 * A_GATE_RANK
    w_main = jnp.concatenate([w_in[:, :lr0], w_in[:, lr1:]], axis=1).astype(BF16)
    w_lr = jnp.pad(w_in[:, lr0:lr1], ((0, 0), (0, LANES - 2 * A_GATE_RANK))).astype(BF16)
    p, lr = _proj_in(ctx, x, mod0, norm_mix[0:1], w_main, w_lr)

    gw = ab_gate_w[0]
    gw_pad = jnp.zeros((2, LANES, A_HEADS * A_DK), F32)
    gw_pad = gw_pad.at[0, :A_GATE_RANK].set(gw[0]).at[1, A_GATE_RANK:2 * A_GATE_RANK].set(gw[1])
    cw = jnp.concatenate([ab_conv_w[0], ab_conv_b[0][None], jnp.zeros((4, B_WIDTH), F32)], axis=0)
    yg, yc = _gla_conv(p, lr, gw_pad.astype(BF16), ab_gate_b[0].reshape(2, 1, A_HEADS * A_DK),
                       ab_out_norm[0:1], cw)

    w_out = ab_w_out[0].astype(BF16)
    ngla = A_HEADS * A_DV
    xmid, h, top_e, top_g, cnt = _outproj(
        [yg, yc], [w_out[:ngla], w_out[ngla:]], [ctx, x], _ctx_latent_specs(), n_tiles_all, mod0, seg_all,
        norm_ffn[0:1], router_w[0].T, router_b[0].reshape(N_EXPERTS, 1))
    x1 = _moe_ffn(h, top_e, top_g, cnt, 0, xmid, n_tiles_all, mod0, seg_all,
                  moe_w1, moe_b1, moe_w2, moe_b2, fnorm, False)

    cos, sin = _rope_tables()
    perm = _rope_perm()
    qk_heads = C_HEADS + C_KV_HEADS
    cols = [hd * C_HEAD_DIM + p for hd in range(qk_heads) for p in perm] + list(range(qk_heads * C_HEAD_DIM, C_QKV))
    w_qkv = attn_w_qkv[0][:, jnp.asarray(cols, I32)].astype(BF16)
    perm_idx = jnp.asarray(perm, I32)
    q, k, v = _qkv(x1, mod1, norm_mix[1:2], w_qkv, attn_q_norm[0:1, perm_idx],
                   attn_k_norm[0:1, perm_idx], cos, sin)
    o = _attention(q, k, v)
    xmid, h, top_e, top_g, cnt = _outproj(
        [o], [attn_w_o[0].astype(BF16)], [x1],
        [pl.BlockSpec((1, TM, D_MODEL), lambda b, j: (b, j + CTX_LEN // TM, 0))], n_tiles_lat, mod1, seg_lat,
        norm_ffn[1:2], router_w[1].T, router_b[1].reshape(N_EXPERTS, 1))
    return _moe_ffn(h, top_e, top_g, cnt, 1, xmid, n_tiles_lat, mod1, seg_lat,
                    moe_w1, moe_b1, moe_w2, moe_b2, fnorm, True)
```

```python
import functools

import jax
import jax.numpy as jnp
from jax import lax
from jax.experimental import pallas as pl
from jax.experimental.pallas import tpu as pltpu

F32 = jnp.float32
BF16 = jnp.bfloat16
I32 = jnp.int32

D_MODEL = 1024
BATCH = 8
SEQ = 2048
DEPTH = 2
GRID_W = 64
CTX_LEN = 256
L_ALL = CTX_LEN + SEQ
N_MOD = 6
EPS = 1e-6

A_DV = 128
A_HEADS = 4
A_DK = 64
A_GATE_RANK = 16
A_GATE_TAU = 16.0
GLA_CHUNK = 64
B_WIDTH = 512
AB_MAIN = 3072

C_HEAD_DIM = 128
C_HEADS = 8
C_KV_HEADS = 2
C_GROUP = 4
C_QKV = (C_HEADS + 2 * C_KV_HEADS) * C_HEAD_DIM
ROPE_THETA = 10000.0
LOG2_E = 1.4426950408889634

N_EXPERTS = 32
TOP_K = 4
SWIGLU_LIMIT = 7.0
SWIGLU_ALPHA = 1.702

LANES = 128
SUBLANES = 8
ROW_SLABS = D_MODEL // LANES
TM = 256
ATT_TQ = 512
TILE_ROWS = TM * TOP_K
MOE_BM = 512
MOE_HALF = MOE_BM // 2
MOE_RING = 3
VMEM_LIMIT = 56 * 1024 * 1024

NT_DIMS = (((1,), (1,)), ((), ()))
TN_DIMS = (((0,), (0,)), ((), ()))


def _dot(a, b):
    return jnp.dot(a, b, preferred_element_type=F32)


def _dot_nt(a, b):
    return lax.dot_general(a, b, NT_DIMS, preferred_element_type=F32)


def _dot_tn(a, b):
    return lax.dot_general(a, b, TN_DIMS, preferred_element_type=F32)


def _params(sem, vmem=None):
    return pltpu.CompilerParams(dimension_semantics=sem, vmem_limit_bytes=vmem)


def _norm_mod(x, g, shift, scale):
    ms = jnp.mean(x * x, axis=-1, keepdims=True)
    y = x * lax.rsqrt(ms + EPS) * g
    return y * (1.0 + scale) + shift


def _sigmoid(x):
    return 1.0 / (1.0 + jnp.exp(-x))


def _ada_kernel(cond_ref, w_ref, b_ref, o_ref):
    c = cond_ref[...]
    s = (c * _sigmoid(c)).astype(BF16)
    o_ref[0] = _dot(s, w_ref[0].astype(BF16)) + b_ref[0]


def _ada(cond, ada_w, ada_b):
    tn = 1536
    n = N_MOD * D_MODEL
    return pl.pallas_call(
        _ada_kernel,
        grid=(DEPTH, n // tn),
        in_specs=[
            pl.BlockSpec((16, D_MODEL), lambda l, j: (0, 0)),
            pl.BlockSpec((1, D_MODEL, tn), lambda l, j: (l, 0, j)),
            pl.BlockSpec((1, 1, tn), lambda l, j: (l, 0, j)),
        ],
        out_specs=pl.BlockSpec((1, 16, tn), lambda l, j: (l, 0, j)),
        out_shape=jax.ShapeDtypeStruct((DEPTH, 16, n), F32),
        compiler_params=_params(("arbitrary", "arbitrary"), VMEM_LIMIT),
        name="ada",
    )(cond, ada_w, ada_b.reshape(DEPTH, 1, n))


def _ctx_or_latent(ctx_ref, x_ref):
    return jnp.where(pl.program_id(1) == 0, ctx_ref[0], x_ref[0])


def _ctx_latent_specs():
    return [pl.BlockSpec((1, TM, D_MODEL), lambda b, j: (b, 0, 0)),
            pl.BlockSpec((1, TM, D_MODEL), lambda b, j: (b, jnp.maximum(j - 1, 0), 0))]


def _proj_in_kernel(ctx_ref, x_ref, mod_ref, g_ref, w_ref, wlr_ref, p_ref, lr_ref):
    mod = mod_ref[0, 0]
    h = _norm_mod(_ctx_or_latent(ctx_ref, x_ref), g_ref[...], mod[0:1], mod[1:2]).astype(BF16)
    p_ref[0] = _dot(h, w_ref[...]).astype(BF16)
    lr_ref[0] = _dot(h, wlr_ref[...]).astype(BF16)


def _proj_in(ctx, x, mod, g, w_main, w_lr):
    nj = L_ALL // TM
    return pl.pallas_call(
        _proj_in_kernel,
        grid=(BATCH, nj),
        in_specs=_ctx_latent_specs() + [
            pl.BlockSpec((1, 1, 8, D_MODEL), lambda b, j: (b, jnp.minimum(j, 1), 0, 0)),
            pl.BlockSpec((1, D_MODEL), lambda b, j: (0, 0)),
            pl.BlockSpec((D_MODEL, AB_MAIN), lambda b, j: (0, 0)),
            pl.BlockSpec((D_MODEL, LANES), lambda b, j: (0, 0)),
        ],
        out_specs=[
            pl.BlockSpec((1, TM, AB_MAIN), lambda b, j: (b, j, 0)),
            pl.BlockSpec((1, TM, LANES), lambda b, j: (b, j, 0)),
        ],
        out_shape=[
            jax.ShapeDtypeStruct((BATCH, L_ALL, AB_MAIN), BF16),
            jax.ShapeDtypeStruct((BATCH, L_ALL, LANES), BF16),
        ],
        compiler_params=_params(("arbitrary", "arbitrary"), VMEM_LIMIT),
        name="proj_in",
    )(ctx, x, mod, g, w_main, w_lr)


def _log_sigmoid(z):
    return jnp.minimum(z, 0.0) - jnp.log1p(jnp.exp(-jnp.abs(z)))


def _gla_conv_kernel(q_ref, k_ref, v_ref, r_ref, gb_ref, gc_ref, u_ref, lr_ref, gw_ref, gbias_ref,
                     onorm_ref, cw_ref, yg_ref, yc_ref, of_ref, ob_ref, sf_ref, sb_ref, xs_ref):
    ch = GLA_CHUNK
    grp = TM
    n_grp = L_ALL // grp
    cpg = grp // ch
    row = lax.broadcasted_iota(I32, (grp, grp), 0)
    col = lax.broadcasted_iota(I32, (grp, grp), 1)
    same_chunk = (row // ch) == (col // ch)
    tri = (jnp.logical_and(same_chunk, row >= col), jnp.logical_and(same_chunk, col >= row))
    tri_bf = (tri[0].astype(BF16), tri[1].astype(BF16))
    lane = lax.broadcasted_iota(I32, (grp, LANES), 1)
    head_mask = (lane < A_DK, lane >= A_DK)
    chunk_of_row = lax.broadcasted_iota(I32, (grp, LANES), 0) // ch

    sf_ref[...] = jnp.zeros_like(sf_ref)
    sb_ref[...] = jnp.zeros_like(sb_ref)

    def chunk_rows(x, idx):
        return jnp.concatenate(
            [jnp.broadcast_to(x[c * ch + idx:c * ch + idx + 1], (ch, x.shape[1])) for c in range(cpg)], axis=0)

    s_refs = (sf_ref, sb_ref)
    o_refs = (of_ref, ob_ref)
    dirs = (0, 1)
    pairs = [(d, hh) for d in dirs for hh in range(2)]

    def by_chunk_lanes(m, x):
        return jnp.concatenate([jnp.where(jnp.logical_and(m, chunk_of_row == c), x, 0.0) for c in range(cpg)],
                               axis=1).astype(BF16)

    def body(i, carry):
        gidx = (i, jnp.where(i == 0, 0, n_grp - i))
        rows = [pl.ds(pl.multiple_of(g * grp, grp), grp) for g in gidx]
        q = [q_ref[0, r, :].astype(F32) * (A_DK ** -0.5) for r in rows]
        k = [k_ref[0, r, :].astype(F32) for r in rows]
        v = [v_ref[0, r, :] for r in rows]
        z = [_dot(lr_ref[0, rows[d], :].astype(BF16), gw_ref[d]) + gbias_ref[d] for d in dirs]
        a = [_log_sigmoid(z[d]) * (1.0 / A_GATE_TAU) for d in dirs]
        a_hi = [a[d].astype(BF16) for d in dirs]
        a_lo = [(a[d] - a_hi[d].astype(F32)).astype(BF16) for d in dirs]
        cum = [_dot(tri_bf[d], a_hi[d]) + _dot(tri_bf[d], a_lo[d]) for d in dirs]
        ref = (chunk_rows(cum[0], ch // 2 - 1), chunk_rows(cum[1], ch // 2))
        last = (chunk_rows(cum[0], ch - 1), chunk_rows(cum[1], 0))
        order = (range(cpg), range(cpg - 1, -1, -1))
        qe = [q[d] * jnp.exp(cum[d]) for d in dirs]
        qt = [q[d] * jnp.exp(cum[d] - ref[d]) for d in dirs]
        kt = [(k[d] * jnp.exp(ref[d] - cum[d])).astype(BF16) for d in dirs]
        kl = [k[d] * jnp.exp(last[d] - cum[d]) for d in dirs]
        dec = [jnp.exp(last[d]) for d in dirs]
        vh = {(d, hh): v[d][:, hh * A_DV:(hh + 1) * A_DV].astype(BF16) for d, hh in pairs}
        sc = {(d, hh): _dot_nt(jnp.where(head_mask[hh], qt[d], 0.0).astype(BF16), kt[d]) for d, hh in pairs}
        kv = {(d, hh): _dot(vh[d, hh].T, by_chunk_lanes(head_mask[hh], kl[d])) for d, hh in pairs}
        o_intra = {(d, hh): _dot(jnp.where(tri[d], sc[d, hh], 0.0).astype(BF16), vh[d, hh]) for d, hh in pairs}
        for d, hh in pairs:
            st = s_refs[d][hh]
            states = [None] * cpg
            for c in order[d]:
                states[c] = st
                st = st * dec[d][c * ch:c * ch + 1] + kv[d, hh][:, c * LANES:(c + 1) * LANES]
            s_refs[d][hh] = st
            o_inter = _dot_nt(by_chunk_lanes(head_mask[hh], qe[d]), jnp.concatenate(states, axis=1).astype(BF16))
            o_refs[d][rows[d], hh * A_DV:(hh + 1) * A_DV] = o_intra[d, hh] + o_inter
        return carry

    lax.fori_loop(0, n_grp, body, 0)

    pad = SUBLANES
    xs_ref[0:pad, :] = jnp.zeros((pad, xs_ref.shape[1]), F32)
    xs_ref[pad + L_ALL:, :] = jnp.zeros((pad, xs_ref.shape[1]), F32)
    xs_ref[pad:pad + L_ALL, :] = gc_ref[0].astype(F32) * u_ref[0].astype(F32)

    cw = cw_ref[...]
    onorm = onorm_ref[...]
    trow = lax.broadcasted_iota(I32, (TM, 1), 0)
    for ti in range(L_ALL // TM):
        s0 = ti * TM
        rows = slice(s0, s0 + TM)
        o = of_ref[rows, :] + ob_ref[rows, :]
        r = r_ref[0, rows, :].astype(F32)
        parts = []
        for hh in range(2):
            oh = o[:, hh * A_DV:(hh + 1) * A_DV]
            ms = jnp.mean(oh * oh, axis=-1, keepdims=True)
            parts.append(oh * lax.rsqrt(ms + EPS) * onorm[:, hh * A_DV:(hh + 1) * A_DV])
        on = jnp.concatenate(parts, axis=1)
        yg_ref[0, rows, :] = (on * (r * _sigmoid(r))).astype(BF16)
        xm1 = xs_ref[pad + s0 - 1:pad + s0 - 1 + TM, :]
        x0 = xs_ref[pad + s0:pad + s0 + TM, :]
        xp1 = xs_ref[pad + s0 + 1:pad + s0 + 1 + TM, :]
        if s0 + TM == CTX_LEN:
            xp1 = jnp.where(trow == TM - 1, 0.0, xp1)
        if s0 == CTX_LEN:
            xm1 = jnp.where(trow == 0, 0.0, xm1)
        conv = cw[0:1] * xm1 + cw[1:2] * x0 + cw[2:3] * xp1 + cw[3:4]
        yc_ref[0, rows, :] = (gb_ref[0, rows, :].astype(F32) * conv).astype(BF16)


def _gla_conv(p, lr, gw_pad, gbias, onorm, cw):
    hw = 2 * A_DV

    def pspec(width, base):
        return pl.BlockSpec((1, L_ALL, width), lambda b, i: (b, 0, base + i))

    return pl.pallas_call(
        _gla_conv_kernel,
        grid=(BATCH, A_HEADS // 2),
        in_specs=[
            pspec(LANES, 0),
            pspec(LANES, 2),
            pspec(hw, 2),
            pspec(hw, 4),
            pspec(hw, 6),
            pspec(hw, 8),
            pspec(hw, 10),
            pl.BlockSpec((1, L_ALL, LANES), lambda b, i: (b, 0, 0)),
            pl.BlockSpec((2, LANES, LANES), lambda b, i: (0, 0, i)),
            pl.BlockSpec((2, 1, LANES), lambda b, i: (0, 0, i)),
            pl.BlockSpec((1, hw), lambda b, i: (0, i)),
            pl.BlockSpec((8, hw), lambda b, i: (0, i)),
        ],
        out_specs=[
            pl.BlockSpec((1, L_ALL, hw), lambda b, i: (b, 0, i)),
            pl.BlockSpec((1, L_ALL, hw), lambda b, i: (b, 0, i)),
        ],
        out_shape=[
            jax.ShapeDtypeStruct((BATCH, L_ALL, A_HEADS * A_DV), BF16),
            jax.ShapeDtypeStruct((BATCH, L_ALL, B_WIDTH), BF16),
        ],
        scratch_shapes=[
            pltpu.VMEM((L_ALL, hw), F32),
            pltpu.VMEM((L_ALL, hw), F32),
            pltpu.VMEM((2, A_DV, LANES), F32),
            pltpu.VMEM((2, A_DV, LANES), F32),
            pltpu.VMEM((L_ALL + 2 * SUBLANES, hw), F32),
        ],
        compiler_params=_params(("arbitrary", "arbitrary"), VMEM_LIMIT),
        name="gla_conv",
    )(p, p, p, p, p, p, p, lr, gw_pad, gbias, onorm, cw)


def _outproj_kernel(n_in, n_res, *refs):
    y_refs = refs[:n_in]
    w_refs = refs[n_in:2 * n_in]
    res_refs = refs[2 * n_in:2 * n_in + n_res]
    mod_ref, g_ref, rwt_ref, rb_ref, xo_ref, h_ref, te_ref, tg_ref, cnt_ref = refs[2 * n_in + n_res:]
    is_ctx = pl.program_id(1) == 0
    mod = mod_ref[0, 0]
    wt = rwt_ref[...]
    wb = wt.astype(BF16)
    wl = (wt - wb.astype(F32)).astype(BF16)
    eidx = lax.broadcasted_iota(I32, (N_EXPERTS, LANES), 0)
    cnt = jnp.zeros((N_EXPERTS, 1), I32)
    groups = [slice(r0, r0 + LANES) for r0 in range(0, TM, LANES)]
    accs = []
    for rows in groups:
        acc = _dot(y_refs[0][0, rows, :], w_refs[0][...])
        for i in range(1, n_in):
            acc = acc + _dot(y_refs[i][0, rows, :], w_refs[i][...])
        accs.append(acc)
    hs = []
    for rows, acc in zip(groups, accs):
        xres = res_refs[0][0, rows, :]
        if n_res == 2:
            xres = jnp.where(is_ctx, xres, res_refs[1][0, rows, :])
        xn = xres + mod[2:3] * acc
        xo_ref[0, rows, :] = xn
        h = _norm_mod(xn, g_ref[...], mod[3:4], mod[4:5])
        h_ref[rows, :] = h.astype(BF16)
        hs.append(h)
    all_logits = []
    for h in hs:
        hb = h.astype(BF16)
        hl = (h - hb.astype(F32)).astype(BF16)
        all_logits.append(_dot_nt(wb, hb) + _dot_nt(wb, hl) + _dot_nt(wl, hb) + rb_ref[...])
    for rows, logits in zip(groups, all_logits):
        cur = logits
        vals, idxs = [], []
        for _ in range(TOP_K):
            m = jnp.max(cur, axis=0, keepdims=True)
            sel = jnp.min(jnp.where(cur == m, eidx, N_EXPERTS), axis=0, keepdims=True)
            vals.append(m)
            idxs.append(sel)
            cur = jnp.where(eidx == sel, -jnp.inf, cur)
        ex = [jnp.exp(v - vals[0]) for v in vals]
        den = ex[0] + ex[1] + ex[2] + ex[3]
        zi = jnp.zeros_like(idxs[0])
        zf = jnp.zeros_like(den)
        te_ref[:, rows] = jnp.concatenate(idxs + [zi] * (8 - TOP_K), axis=0)
        tg_ref[:, rows] = jnp.concatenate([e / den for e in ex] + [zf] * (8 - TOP_K), axis=0)
        onehot = jnp.where(eidx == idxs[0], 1, 0)
        for k in range(1, TOP_K):
            onehot = onehot + jnp.where(eidx == idxs[k], 1, 0)
        cnt = cnt + jnp.sum(onehot, axis=1, keepdims=True)
    cnt_ref[0] = jnp.broadcast_to(cnt, (N_EXPERTS, LANES))


def _outproj(ys, ws, res, res_specs, n_tiles, mod, mod_seg, g, rwt, rb):
    n_in = len(ys)
    n_tok = BATCH * n_tiles * TM
    in_specs = []
    for y in ys:
        in_specs.append(pl.BlockSpec((1, TM, y.shape[2]), lambda b, j: (b, j, 0)))
    for w in ws:
        in_specs.append(pl.BlockSpec(w.shape, lambda b, j: (0, 0)))
    in_specs += list(res_specs) + [
        pl.BlockSpec((1, 1, 8, D_MODEL), lambda b, j: (b, mod_seg(j), 0, 0)),
        pl.BlockSpec((1, D_MODEL), lambda b, j: (0, 0)),
        pl.BlockSpec((N_EXPERTS, D_MODEL), lambda b, j: (0, 0)),
        pl.BlockSpec((N_EXPERTS, 1), lambda b, j: (0, 0)),
    ]
    return pl.pallas_call(
        functools.partial(_outproj_kernel, n_in, len(res)),
        grid=(BATCH, n_tiles),
        in_specs=in_specs,
        out_specs=[
            pl.BlockSpec((1, TM, D_MODEL), lambda b, j: (b, j, 0)),
            pl.BlockSpec((TM, D_MODEL), lambda b, j: (b * n_tiles + j, 0)),
            pl.BlockSpec((8, TM), lambda b, j: (0, b * n_tiles + j)),
            pl.BlockSpec((8, TM), lambda b, j: (0, b * n_tiles + j)),
            pl.BlockSpec((1, N_EXPERTS, LANES), lambda b, j: (b * n_tiles + j, 0, 0)),
        ],
        out_shape=[
            jax.ShapeDtypeStruct((BATCH, n_tiles * TM, D_MODEL), F32),
            jax.ShapeDtypeStruct((n_tok, D_MODEL), BF16),
            jax.ShapeDtypeStruct((8, n_tok), I32),
            jax.ShapeDtypeStruct((8, n_tok), F32),
            jax.ShapeDtypeStruct((BATCH * n_tiles, N_EXPERTS, LANES), I32),
        ],
        compiler_params=_params(("arbitrary", "arbitrary"), VMEM_LIMIT),
        name="outproj_router",
    )(*ys, *ws, *res, mod, g, rwt, rb)


def _moe_rows(n_tok):
    n_assign = n_tok * TOP_K
    n_blocks = -(-(n_assign + N_EXPERTS * (MOE_BM - 1)) // MOE_BM)
    return n_assign, n_blocks


def _route_tables(cnt, n_tok):
    _, n_blocks = _moe_rows(n_tok)
    counts = jnp.sum(cnt, axis=0)
    padded = (counts + MOE_BM - 1) // MOE_BM * MOE_BM
    pad_end = jnp.cumsum(padded)
    block_start = pad_end - padded
    seg_start = block_start[None, :] + jnp.cumsum(cnt, axis=0) - cnt
    blk_row = jnp.arange(n_blocks, dtype=I32) * MOE_BM
    block_e = jnp.minimum(jnp.sum((pad_end[None, :] <= blk_row[:, None]).astype(I32), axis=1), N_EXPERTS - 1)
    n_used = (pad_end[-1] // MOE_BM).reshape(1)
    eids = jnp.arange(N_EXPERTS, dtype=I32)
    later = jnp.where(jnp.logical_and(eids[None, :] > eids[:, None], counts[None, :] > 0), eids[None, :], N_EXPERTS)
    nxt = jnp.min(later, axis=1)
    next_e = jnp.where(nxt == N_EXPERTS, -1, nxt)
    return dict(block_e=block_e.astype(I32), n_used=n_used.astype(I32),
                row_end=(block_start + counts).astype(I32), next_e=next_e.astype(I32),
                seg_start=seg_start.reshape(-1).astype(I32), cnt=cnt.reshape(-1).astype(I32),
                pad_start=(block_start + counts).astype(I32), pad_len=(padded - counts).astype(I32))


def _store_row_slabs(ref, val, n_rows):
    for s in range(ROW_SLABS):
        ref[pl.ds(s, n_rows, stride=ROW_SLABS), :] = val[:, s * LANES:(s + 1) * LANES]


def _load_row_slabs(ref, n_rows):
    return jnp.concatenate(
        [ref[pl.ds(s, n_rows, stride=ROW_SLABS), :] for s in range(ROW_SLABS)], axis=1)


def _slab_rows(start, n):
    return pl.ds(pl.multiple_of(start * ROW_SLABS, ROW_SLABS), n * ROW_SLABS)


def _local_positions(te):
    eidx = lax.broadcasted_iota(I32, (N_EXPERTS, TM), 0)
    hits = [te[k:k + 1] == eidx for k in range(TOP_K)]
    onehot = jnp.where(hits[0], 1.0, 0.0)
    for k in range(1, TOP_K):
        onehot = onehot + jnp.where(hits[k], 1.0, 0.0)
    mb = onehot.astype(BF16)
    trow = lax.broadcasted_iota(I32, (TM, TM), 0)
    tcol = lax.broadcasted_iota(I32, (TM, TM), 1)
    before = _dot(mb, (trow < tcol).astype(BF16))
    totals = _dot(mb, jnp.ones((TM, TM), BF16))
    erow = lax.broadcasted_iota(I32, (N_EXPERTS, N_EXPERTS), 0)
    ecol = lax.broadcasted_iota(I32, (N_EXPERTS, N_EXPERTS), 1)
    first = _dot((ecol < erow).astype(BF16), totals.astype(BF16))
    base = first + before
    return [jnp.sum(jnp.where(hits[k], base, 0.0), axis=0, keepdims=True).astype(I32) for k in range(TOP_K)]


def _dispatch_kernel(n_blocks, ss_ref, cn_ref, ps_ref, pl_ref, nu_ref, h_ref, te_ref, xs_hbm, lpos_ref,
                     sbuf, zbuf, sem):
    j = pl.program_id(0)
    n_tiles = pl.num_programs(0)

    def zero_fill(act):
        def per_expert(e, carry):
            n = pl_ref[e]

            @pl.when(n > 0)
            def _():
                act(pltpu.make_async_copy(zbuf.at[_slab_rows(0, n)], xs_hbm.at[_slab_rows(ps_ref[e], n)],
                                          sem.at[MOE_RING]))
            return carry
        lax.fori_loop(0, N_EXPERTS, per_expert, 0)

        def per_block(b, carry):
            act(pltpu.make_async_copy(zbuf, xs_hbm.at[_slab_rows(b * MOE_BM, MOE_BM)], sem.at[MOE_RING]))
            return carry
        lax.fori_loop(nu_ref[0], n_blocks, per_block, 0)

    @pl.when(j == 0)
    def _():
        zbuf[...] = jnp.zeros_like(zbuf)
        zero_fill(lambda cp: cp.start())

    lpos = _local_positions(te_ref[...])
    zi = jnp.zeros_like(lpos[0])
    lpos_ref[...] = jnp.concatenate(lpos + [zi] * (8 - TOP_K), axis=0)
    riota = lax.broadcasted_iota(I32, (TILE_ROWS, TM), 0)
    hit = riota == lpos[0]
    for k in range(1, TOP_K):
        hit = jnp.logical_or(hit, riota == lpos[k])
    perm = jnp.where(hit, 1.0, 0.0).astype(BF16)
    hb = h_ref[...].astype(BF16)
    slot = lax.rem(j, MOE_RING)
    buf = sbuf.at[slot]

    def rows_copy(sl):
        return pltpu.make_async_copy(sbuf.at[sl], xs_hbm.at[_slab_rows(0, TILE_ROWS)], sem.at[sl])

    @pl.when(j >= MOE_RING)
    def _():
        rows_copy(slot).wait()

    for s in range(0, ROW_SLABS, 2):
        xl = _dot(perm, hb[:, s * LANES:(s + 2) * LANES])
        buf[pl.ds(s, TILE_ROWS, stride=ROW_SLABS), :] = xl[:, :LANES]
        buf[pl.ds(s + 1, TILE_ROWS, stride=ROW_SLABS), :] = xl[:, LANES:]

    local = 0
    for e in range(N_EXPERTS):
        n = cn_ref[j * N_EXPERTS + e]

        @pl.when(n > 0)
        def _(e=e, n=n, local=local):
            pltpu.make_async_copy(buf.at[_slab_rows(local, n)],
                                  xs_hbm.at[_slab_rows(ss_ref[j * N_EXPERTS + e], n)],
                                  sem.at[slot]).start(priority=e % 2)
        local = local + n

    @pl.when(j == n_tiles - 1)
    def _():
        for back in range(MOE_RING):
            rows_copy(lax.rem(j + MOE_RING - back, MOE_RING)).wait()
        zero_fill(lambda cp: cp.wait())


def _dispatch(h, top_e, tabs, n_tok):
    _, n_blocks = _moe_rows(n_tok)
    n_tiles = n_tok // TM
    grid_spec = pltpu.PrefetchScalarGridSpec(
        num_scalar_prefetch=5,
        grid=(n_tiles,),
        in_specs=[
            pl.BlockSpec((TM, D_MODEL), lambda j, *_: (j, 0)),
            pl.BlockSpec((8, TM), lambda j, *_: (0, j)),
        ],
        out_specs=[
            pl.BlockSpec(memory_space=pl.ANY),
            pl.BlockSpec((8, TM), lambda j, *_: (0, j)),
        ],
        scratch_shapes=[
            pltpu.VMEM((MOE_RING, TILE_ROWS * ROW_SLABS, LANES), F32),
            pltpu.VMEM((MOE_BM * ROW_SLABS, LANES), F32),
            pltpu.SemaphoreType.DMA((MOE_RING + 1,)),
        ],
    )
    return pl.pallas_call(
        functools.partial(_dispatch_kernel, n_blocks),
        grid_spec=grid_spec,
        out_shape=[
            jax.ShapeDtypeStruct((n_blocks * MOE_BM * ROW_SLABS, LANES), F32),
            jax.ShapeDtypeStruct((8, n_tok), I32),
        ],
        compiler_params=_params(("arbitrary",), VMEM_LIMIT),
        name="moe_dispatch",
    )(tabs["seg_start"], tabs["cnt"], tabs["pad_start"], tabs["pad_len"], tabs["n_used"], h, top_e)


def _moe_kernel(layer, be_ref, nu_ref, end_ref, ne_ref, x_ref, b1_ref, b2_ref, w1_hbm, w2_hbm, y_ref,
                w1s, w2s, w1b, w2b, sem):
    i = pl.program_id(0)
    nu = nu_ref[0]

    def weight_copies(e):
        return (pltpu.make_async_copy(w1_hbm.at[layer, e], w1s, sem.at[0]),
                pltpu.make_async_copy(w2_hbm.at[layer, e], w2s, sem.at[1]))

    @pl.when(i < nu)
    def _():
        e = be_ref[i]

        @pl.when(i == 0)
        def _():
            for cp in weight_copies(e):
                cp.start()

        @pl.when(jnp.logical_or(i == 0, e != be_ref[jnp.maximum(i - 1, 0)]))
        def _():
            for cp in weight_copies(e):
                cp.wait()
            w1b[...] = w1s[...].astype(BF16)
            w2b[...] = w2s[...].astype(BF16)
            nxt = ne_ref[e]

            @pl.when(nxt >= 0)
            def _():
                for cp in weight_copies(nxt):
                    cp.start()

        def run(n_rows):
            slab_rows = pl.ds(0, n_rows * ROW_SLABS)
            x = _load_row_slabs(x_ref.at[slab_rows], n_rows).astype(BF16)
            h1 = _dot(x, w1b[...]) + b1_ref[0, 0]
            gate = jnp.minimum(h1[:, :D_MODEL], SWIGLU_LIMIT)
            up = jnp.clip(h1[:, D_MODEL:], -SWIGLU_LIMIT, SWIGLU_LIMIT)
            act = (up + 1.0) * gate * _sigmoid(SWIGLU_ALPHA * gate)
            y = _dot(act.astype(BF16), w2b[...]) + b2_ref[0, 0]
            _store_row_slabs(y_ref.at[slab_rows], y, n_rows)

        n_valid = end_ref[e] - i * MOE_BM

        @pl.when(n_valid > MOE_HALF)
        def _():
            run(MOE_BM)

        @pl.when(n_valid <= MOE_HALF)
        def _():
            run(MOE_HALF)
            y_ref[pl.ds(MOE_HALF * ROW_SLABS, MOE_HALF * ROW_SLABS), :] = jnp.zeros(
                (MOE_HALF * ROW_SLABS, LANES), F32)

    @pl.when(i >= nu)
    def _():
        y_ref[...] = jnp.zeros_like(y_ref)


def _moe(xs, tabs, layer, w1, b1, w2, b2, n_tok):
    _, n_blocks = _moe_rows(n_tok)
    d2 = 2 * D_MODEL
    blk = MOE_BM * ROW_SLABS
    grid_spec = pltpu.PrefetchScalarGridSpec(
        num_scalar_prefetch=4,
        grid=(n_blocks,),
        in_specs=[
            pl.BlockSpec((blk, LANES), lambda i, be, nu, *_: (jnp.minimum(i, nu[0] - 1), 0)),
            pl.BlockSpec((1, 1, 1, d2), lambda i, be, *_: (layer, be[i], 0, 0)),
            pl.BlockSpec((1, 1, 1, D_MODEL), lambda i, be, *_: (layer, be[i], 0, 0)),
            pl.BlockSpec(memory_space=pl.ANY),
            pl.BlockSpec(memory_space=pl.ANY),
        ],
        out_specs=pl.BlockSpec((blk, LANES), lambda i, *_: (i, 0)),
        scratch_shapes=[
            pltpu.VMEM((D_MODEL, d2), F32),
            pltpu.VMEM((D_MODEL, D_MODEL), F32),
            pltpu.VMEM((D_MODEL, d2), BF16),
            pltpu.VMEM((D_MODEL, D_MODEL), BF16),
            pltpu.SemaphoreType.DMA((2,)),
        ],
    )
    return pl.pallas_call(
        functools.partial(_moe_kernel, layer),
        grid_spec=grid_spec,
        out_shape=jax.ShapeDtypeStruct((n_blocks * blk, LANES), F32),
        compiler_params=_params(("arbitrary",), VMEM_LIMIT),
        name="moe_experts",
    )(tabs["block_e"], tabs["n_used"], tabs["row_end"], tabs["next_e"], xs,
      b1.reshape(DEPTH, N_EXPERTS, 1, d2), b2.reshape(DEPTH, N_EXPERTS, 1, D_MODEL), w1, w2)


def _combine_kernel(final, n_tiles, ss_ref, cn_ref, lpos_ref, tg_ref, x_ref, mod_ref, g_ref, ys_hbm, o_ref,
                    cbuf, sem):
    t = pl.program_id(0) * n_tiles + pl.program_id(1)
    n_total = pl.num_programs(0) * n_tiles
    slot = lax.rem(t, MOE_RING)
    ahead = MOE_RING - 1

    def fetch(tile, sl):
        local = 0
        for e in range(N_EXPERTS):
            n = cn_ref[tile * N_EXPERTS + e]

            @pl.when(n > 0)
            def _(e=e, n=n, local=local):
                pltpu.make_async_copy(ys_hbm.at[_slab_rows(ss_ref[tile * N_EXPERTS + e], n)],
                                      cbuf.at[sl, _slab_rows(local, n)], sem.at[sl]).start(priority=e % 2)
            local = local + n

    @pl.when(t == 0)
    def _():
        for first in range(ahead):
            fetch(first, first)

    @pl.when(t + ahead < n_total)
    def _():
        fetch(t + ahead, lax.rem(t + ahead, MOE_RING))

    pltpu.make_async_copy(ys_hbm.at[_slab_rows(0, TILE_ROWS)], cbuf.at[slot], sem.at[slot]).wait()
    yb = _load_row_slabs(cbuf.at[slot], TILE_ROWS).astype(BF16)

    lpos = lpos_ref[...]
    tg = tg_ref[...]
    riota = lax.broadcasted_iota(I32, (TILE_ROWS, TM), 0)
    gsel = jnp.where(riota == lpos[0:1], tg[0:1], 0.0)
    for k in range(1, TOP_K):
        gsel = gsel + jnp.where(riota == lpos[k:k + 1], tg[k:k + 1], 0.0)
    g_hi = gsel.astype(BF16)
    g_lo = (gsel - g_hi.astype(F32)).astype(BF16)
    f = _dot_tn(g_hi, yb) + _dot_tn(g_lo, yb)

    mod = mod_ref[0, 0]
    xn = x_ref[0] + mod[5:6] * f
    if final:
        ms = jnp.mean(xn * xn, axis=-1, keepdims=True)
        xn = xn * lax.rsqrt(ms + EPS) * g_ref[...]
    o_ref[0] = xn


def _combine(ys, tabs, lpos, top_g, xres, n_tiles, mod, mod_seg, g, final):
    grid_spec = pltpu.PrefetchScalarGridSpec(
        num_scalar_prefetch=2,
        grid=(BATCH, n_tiles),
        in_specs=[
            pl.BlockSpec((8, TM), lambda b, j, *_: (0, b * n_tiles + j)),
            pl.BlockSpec((8, TM), lambda b, j, *_: (0, b * n_tiles + j)),
            pl.BlockSpec((1, TM, D_MODEL), lambda b, j, *_: (b, j, 0)),
            pl.BlockSpec((1, 1, 8, D_MODEL), lambda b, j, *_: (b, mod_seg(j), 0, 0)),
            pl.BlockSpec((1, D_MODEL), lambda b, j, *_: (0, 0)),
            pl.BlockSpec(memory_space=pl.ANY),
        ],
        out_specs=pl.BlockSpec((1, TM, D_MODEL), lambda b, j, *_: (b, j, 0)),
        scratch_shapes=[
            pltpu.VMEM((MOE_RING, TILE_ROWS * ROW_SLABS, LANES), F32),
            pltpu.SemaphoreType.DMA((MOE_RING,)),
        ],
    )
    return pl.pallas_call(
        functools.partial(_combine_kernel, final, n_tiles),
        grid_spec=grid_spec,
        out_shape=jax.ShapeDtypeStruct((BATCH, n_tiles * TM, D_MODEL), F32),
        compiler_params=_params(("arbitrary", "arbitrary"), VMEM_LIMIT),
        name="moe_combine",
    )(tabs["seg_start"], tabs["cnt"], lpos, top_g, xres, mod, g, ys)


def _moe_ffn(h, top_e, top_g, cnt, layer, xres, n_tiles, mod, mod_seg, w1, b1, w2, b2, g, final):
    n_tok = BATCH * n_tiles * TM
    tabs = _route_tables(cnt[:, :, 0], n_tok)
    xs, lpos = _dispatch(h, top_e, tabs, n_tok)
    ys = _moe(xs, tabs, layer, w1, b1, w2, b2, n_tok)
    return _combine(ys, tabs, lpos, top_g, xres, n_tiles, mod, mod_seg, g, final)


def _qkv_kernel(x_ref, mod_ref, g_ref, w_ref, qn_ref, kn_ref, cos_ref, sin_ref, q_ref, k_ref, v_ref):
    mod = mod_ref[0, 0]
    h = _norm_mod(x_ref[0], g_ref[...], mod[0:1], mod[1:2]).astype(BF16)
    qkv = _dot(h, w_ref[...])
    cos = cos_ref[...]
    sin = sin_ref[...]

    def head(xh, gn):
        ms = jnp.mean(xh * xh, axis=-1, keepdims=True)
        y = xh * lax.rsqrt(ms + EPS) * gn
        return y * cos + pltpu.roll(y, C_HEAD_DIM // 2, 1) * sin

    qn = qn_ref[...]
    kn = kn_ref[...]
    for hq in range(C_HEADS):
        sl = slice(hq * C_HEAD_DIM, (hq + 1) * C_HEAD_DIM)
        q_ref[0, :, sl] = head(qkv[:, sl], qn).astype(BF16)
    for hk in range(C_KV_HEADS):
        src = slice((C_HEADS + hk) * C_HEAD_DIM, (C_HEADS + hk + 1) * C_HEAD_DIM)
        k_ref[0, :, hk * C_HEAD_DIM:(hk + 1) * C_HEAD_DIM] = head(qkv[:, src], kn).astype(BF16)
    ones = jnp.ones((TM, C_HEAD_DIM), BF16)
    for hk in range(C_KV_HEADS):
        src = slice((C_HEADS + C_KV_HEADS + hk) * C_HEAD_DIM, (C_HEADS + C_KV_HEADS + hk + 1) * C_HEAD_DIM)
        v_ref[0, :, 2 * hk * C_HEAD_DIM:(2 * hk + 1) * C_HEAD_DIM] = qkv[:, src].astype(BF16)
        v_ref[0, :, (2 * hk + 1) * C_HEAD_DIM:(2 * hk + 2) * C_HEAD_DIM] = ones


def _qkv(xall, mod, g, w, qn, kn, cos, sin):
    nj = L_ALL // TM
    kvw = C_KV_HEADS * C_HEAD_DIM
    return pl.pallas_call(
        _qkv_kernel,
        grid=(BATCH, nj),
        in_specs=[
            pl.BlockSpec((1, TM, D_MODEL), lambda b, j: (b, j, 0)),
            pl.BlockSpec((1, 1, 8, D_MODEL), lambda b, j: (b, jnp.minimum(j, 1), 0, 0)),
            pl.BlockSpec((1, D_MODEL), lambda b, j: (0, 0)),
            pl.BlockSpec((D_MODEL, C_QKV), lambda b, j: (0, 0)),
            pl.BlockSpec((1, C_HEAD_DIM), lambda b, j: (0, 0)),
            pl.BlockSpec((1, C_HEAD_DIM), lambda b, j: (0, 0)),
            pl.BlockSpec((TM, C_HEAD_DIM), lambda b, j: (j, 0)),
            pl.BlockSpec((TM, C_HEAD_DIM), lambda b, j: (j, 0)),
        ],
        out_specs=[
            pl.BlockSpec((1, TM, D_MODEL), lambda b, j: (b, jnp.maximum(j - 1, 0), 0)),
            pl.BlockSpec((1, TM, kvw), lambda b, j: (b, j, 0)),
            pl.BlockSpec((1, TM, 2 * kvw), lambda b, j: (b, j, 0)),
        ],
        out_shape=[
            jax.ShapeDtypeStruct((BATCH, SEQ, D_MODEL), BF16),
            jax.ShapeDtypeStruct((BATCH, L_ALL, kvw), BF16),
            jax.ShapeDtypeStruct((BATCH, L_ALL, 2 * kvw), BF16),
        ],
        compiler_params=_params(("arbitrary", "arbitrary"), VMEM_LIMIT),
        name="qkv_rope",
    )(xall, mod, g, w, qn, kn, cos, sin)


def _rope_perm():
    qd = C_HEAD_DIM // 4
    order = (0, 2, 1, 3)
    return [blk * qd + i for blk in order for i in range(qd)]


def _rope_tables():
    rows = SEQ // GRID_W
    row = jnp.repeat(jnp.arange(rows), GRID_W).astype(F32)
    col = jnp.tile(jnp.arange(GRID_W), rows).astype(F32)
    half = C_HEAD_DIM // 2
    inv_freq = ROPE_THETA ** (-jnp.arange(0, half, 2, dtype=F32) / half)
    ar = row[:, None] * inv_freq
    ac = col[:, None] * inv_freq
    cos = jnp.concatenate([jnp.cos(ar), jnp.cos(ac), jnp.cos(ar), jnp.cos(ac)], axis=-1)
    sin = jnp.concatenate([-jnp.sin(ar), -jnp.sin(ac), jnp.sin(ar), jnp.sin(ac)], axis=-1)
    cos = jnp.concatenate([jnp.ones((CTX_LEN, C_HEAD_DIM), F32), cos], axis=0)
    sin = jnp.concatenate([jnp.zeros((CTX_LEN, C_HEAD_DIM), F32), sin], axis=0)
    return cos, sin


def _attn_kernel(q_ref, k_ref, v_ref, o_ref):
    c = (C_HEAD_DIM ** -0.5) * LOG2_E
    def head_cols(h):
        return slice(h * C_HEAD_DIM, (h + 1) * C_HEAD_DIM)

    def scores(h):
        return _dot_nt(q_ref[0, :, head_cols(h)], k_ref[0, :, head_cols(h // C_GROUP)])

    s_next = scores(0)
    for h in range(C_HEADS):
        s = s_next
        if h + 1 < C_HEADS:
            s_next = scores(h + 1)
        g = h // C_GROUP
        m = jnp.max(s, axis=-1, keepdims=True)
        p = jnp.exp2((s - m) * c).astype(BF16)
        oe = _dot(p, v_ref[0, :, 2 * g * C_HEAD_DIM:(2 * g + 2) * C_HEAD_DIM])
        o_ref[0, :, head_cols(h)] = (oe[:, :C_HEAD_DIM] / oe[:, C_HEAD_DIM:]).astype(BF16)


def _attention(q, k, v):
    kvw = C_KV_HEADS * C_HEAD_DIM
    return pl.pallas_call(
        _attn_kernel,
        grid=(BATCH, SEQ // ATT_TQ),
        in_specs=[
            pl.BlockSpec((1, ATT_TQ, D_MODEL), lambda b, j: (b, j, 0)),
            pl.BlockSpec((1, L_ALL, kvw), lambda b, j: (b, 0, 0)),
            pl.BlockSpec((1, L_ALL, 2 * kvw), lambda b, j: (b, 0, 0)),
        ],
        out_specs=pl.BlockSpec((1, ATT_TQ, D_MODEL), lambda b, j: (b, j, 0)),
        out_shape=jax.ShapeDtypeStruct((BATCH, SEQ, D_MODEL), BF16),
        compiler_params=_params(("arbitrary", "arbitrary"), VMEM_LIMIT),
        name="attention",
    )(q, k, v)


def _mod_table(ada_layer):
    m = ada_layer.reshape(16, N_MOD, D_MODEL)
    m_lat = m[:BATCH]
    m_ctx = jnp.broadcast_to(m[BATCH], (BATCH, N_MOD, D_MODEL))
    t = jnp.stack([m_ctx, m_lat], axis=1)
    return jnp.pad(t, ((0, 0), (0, 0), (0, 8 - N_MOD), (0, 0)))


def kernel(x, c, ctx, c_ctx, ada_w, ada_b, norm_mix, norm_ffn, ab_w_in, ab_gate_w, ab_gate_b, ab_out_norm,
           ab_conv_w, ab_conv_b, ab_w_out, attn_w_qkv, attn_q_norm, attn_k_norm, attn_w_o, router_w,
           router_b, moe_w1, moe_b1, moe_w2, moe_b2, final_norm):
    cond = jnp.zeros((16, D_MODEL), F32).at[:BATCH].set(c).at[BATCH].set(c_ctx)
    ada = _ada(cond, ada_w, ada_b)
    mod0 = _mod_table(ada[0])
    mod1 = _mod_table(ada[1])
    seg_all = lambda j: jnp.minimum(j, 1)
    seg_lat = lambda j: 1
    n_tiles_all = L_ALL // TM
    n_tiles_lat = SEQ // TM
    fnorm = final_norm.reshape(1, D_MODEL)

    w_in = ab_w_in[0]
    lr0 = 2 * A_HEADS * A_DK + A_HEADS * A_DV
    lr1 = lr0 + 2 * A_GATE_RANK
    w_main = jnp.concatenate([w_in[:, :lr0], w_in[:, lr1:]], axis=1).astype(BF16)
    w_lr = jnp.pad(w_in[:, lr0:lr1], ((0, 0), (0, LANES - 2 * A_GATE_RANK))).astype(BF16)
    p, lr = _proj_in(ctx, x, mod0, norm_mix[0:1], w_main, w_lr)

    gw = ab_gate_w[0]
    gw_pad = jnp.zeros((2, LANES, A_HEADS * A_DK), F32)
    gw_pad = gw_pad.at[0, :A_GATE_RANK].set(gw[0]).at[1, A_GATE_RANK:2 * A_GATE_RANK].set(gw[1])
    cw = jnp.concatenate([ab_conv_w[0], ab_conv_b[0][None], jnp.zeros((4, B_WIDTH), F32)], axis=0)
    yg, yc = _gla_conv(p, lr, gw_pad.astype(BF16), ab_gate_b[0].reshape(2, 1, A_HEADS * A_DK),
                       ab_out_norm[0:1], cw)

    w_out = ab_w_out[0].astype(BF16)
    ngla = A_HEADS * A_DV
    xmid, h, top_e, top_g, cnt = _outproj(
        [yg, yc], [w_out[:ngla], w_out[ngla:]], [ctx, x], _ctx_latent_specs(), n_tiles_all, mod0, seg_all,
        norm_ffn[0:1], router_w[0].T, router_b[0].reshape(N_EXPERTS, 1))
    x1 = _moe_ffn(h, top_e, top_g, cnt, 0, xmid, n_tiles_all, mod0, seg_all,
                  moe_w1, moe_b1, moe_w2, moe_b2, fnorm, False)

    cos, sin = _rope_tables()
    perm = _rope_perm()
    qk_heads = C_HEADS + C_KV_HEADS
    cols = [hd * C_HEAD_DIM + p for hd in range(qk_heads) for p in perm] + list(range(qk_heads * C_HEAD_DIM, C_QKV))
    w_qkv = attn_w_qkv[0][:, jnp.asarray(cols, I32)].astype(BF16)
    perm_idx = jnp.asarray(perm, I32)
    q, k, v = _qkv(x1, mod1, norm_mix[1:2], w_qkv, attn_q_norm[0:1, perm_idx],
                   attn_k_norm[0:1, perm_idx], cos, sin)
    o = _attention(q, k, v)
    xmid, h, top_e, top_g, cnt = _outproj(
        [o], [attn_w_o[0].astype(BF16)], [x1],
        [pl.BlockSpec((1, TM, D_MODEL), lambda b, j: (b, j + CTX_LEN // TM, 0))], n_tiles_lat, mod1, seg_lat,
        norm_ffn[1:2], router_w[1].T, router_b[1].reshape(N_EXPERTS, 1))
    return _moe_ffn(h, top_e, top_g, cnt, 1, xmid, n_tiles_lat, mod1, seg_lat,
                    moe_w1, moe_b1, moe_w2, moe_b2, fnorm, True)
```

```python
import functools

import jax
import jax.numpy as jnp
from jax import lax
from jax.experimental import pallas as pl
from jax.experimental.pallas import tpu as pltpu

F32 = jnp.float32
BF16 = jnp.bfloat16
I32 = jnp.int32

D_MODEL = 1024
BATCH = 8
SEQ = 2048
DEPTH = 2
GRID_W = 64
CTX_LEN = 256
L_ALL = CTX_LEN + SEQ
N_MOD = 6
EPS = 1e-6

A_DV = 128
A_HEADS = 4
A_DK = 64
A_GATE_RANK = 16
A_GATE_TAU = 16.0
GLA_CHUNK = 64
B_WIDTH = 512
AB_MAIN = 3072

C_HEAD_DIM = 128
C_HEADS = 8
C_KV_HEADS = 2
C_GROUP = 4
C_QKV = (C_HEADS + 2 * C_KV_HEADS) * C_HEAD_DIM
ROPE_THETA = 10000.0
LOG2_E = 1.4426950408889634

N_EXPERTS = 32
TOP_K = 4
SWIGLU_LIMIT = 7.0
SWIGLU_ALPHA = 1.702

LANES = 128
SUBLANES = 8
ROW_SLABS = D_MODEL // LANES
TM = 256
ATT_TQ = 512
TILE_ROWS = TM * TOP_K
MOE_BM = 512
MOE_HALF = MOE_BM // 2
MOE_RING = 3
VMEM_LIMIT = 56 * 1024 * 1024

NT_DIMS = (((1,), (1,)), ((), ()))
TN_DIMS = (((0,), (0,)), ((), ()))


def _dot(a, b):
    return jnp.dot(a, b, preferred_element_type=F32)


def _dot_nt(a, b):
    return lax.dot_general(a, b, NT_DIMS, preferred_element_type=F32)


def _dot_tn(a, b):
    return lax.dot_general(a, b, TN_DIMS, preferred_element_type=F32)


def _params(sem, vmem=None):
    return pltpu.CompilerParams(dimension_semantics=sem, vmem_limit_bytes=vmem)


def _norm_mod(x, g, shift, scale):
    ms = jnp.mean(x * x, axis=-1, keepdims=True)
    y = x * lax.rsqrt(ms + EPS) * g
    return y * (1.0 + scale) + shift


def _sigmoid(x):
    return 1.0 / (1.0 + jnp.exp(-x))


def _ada_kernel(cond_ref, w_ref, b_ref, o_ref):
    c = cond_ref[...]
    s = (c * _sigmoid(c)).astype(BF16)
    o_ref[0] = _dot(s, w_ref[0].astype(BF16)) + b_ref[0]


def _ada(cond, ada_w, ada_b):
    tn = 1536
    n = N_MOD * D_MODEL
    return pl.pallas_call(
        _ada_kernel,
        grid=(DEPTH, n // tn),
        in_specs=[
            pl.BlockSpec((16, D_MODEL), lambda l, j: (0, 0)),
            pl.BlockSpec((1, D_MODEL, tn), lambda l, j: (l, 0, j)),
            pl.BlockSpec((1, 1, tn), lambda l, j: (l, 0, j)),
        ],
        out_specs=pl.BlockSpec((1, 16, tn), lambda l, j: (l, 0, j)),
        out_shape=jax.ShapeDtypeStruct((DEPTH, 16, n), F32),
        compiler_params=_params(("arbitrary", "arbitrary"), VMEM_LIMIT),
        name="ada",
    )(cond, ada_w, ada_b.reshape(DEPTH, 1, n))


def _ctx_or_latent(ctx_ref, x_ref):
    return jnp.where(pl.program_id(1) == 0, ctx_ref[0], x_ref[0])


def _ctx_latent_specs():
    return [pl.BlockSpec((1, TM, D_MODEL), lambda b, j: (b, 0, 0)),
            pl.BlockSpec((1, TM, D_MODEL), lambda b, j: (b, jnp.maximum(j - 1, 0), 0))]


def _proj_in_kernel(ctx_ref, x_ref, mod_ref, g_ref, w_ref, wlr_ref, p_ref, lr_ref):
    mod = mod_ref[0, 0]
    h = _norm_mod(_ctx_or_latent(ctx_ref, x_ref), g_ref[...], mod[0:1], mod[1:2]).astype(BF16)
    p_ref[0] = _dot(h, w_ref[...]).astype(BF16)
    lr_ref[0] = _dot(h, wlr_ref[...]).astype(BF16)


def _proj_in(ctx, x, mod, g, w_main, w_lr):
    nj = L_ALL // TM
    return pl.pallas_call(
        _proj_in_kernel,
        grid=(BATCH, nj),
        in_specs=_ctx_latent_specs() + [
            pl.BlockSpec((1, 1, 8, D_MODEL), lambda b, j: (b, jnp.minimum(j, 1), 0, 0)),
            pl.BlockSpec((1, D_MODEL), lambda b, j: (0, 0)),
            pl.BlockSpec((D_MODEL, AB_MAIN), lambda b, j: (0, 0)),
            pl.BlockSpec((D_MODEL, LANES), lambda b, j: (0, 0)),
        ],
        out_specs=[
            pl.BlockSpec((1, TM, AB_MAIN), lambda b, j: (b, j, 0)),
            pl.BlockSpec((1, TM, LANES), lambda b, j: (b, j, 0)),
        ],
        out_shape=[
            jax.ShapeDtypeStruct((BATCH, L_ALL, AB_MAIN), BF16),
            jax.ShapeDtypeStruct((BATCH, L_ALL, LANES), BF16),
        ],
        compiler_params=_params(("arbitrary", "arbitrary"), VMEM_LIMIT),
        name="proj_in",
    )(ctx, x, mod, g, w_main, w_lr)


def _log_sigmoid(z):
    return jnp.minimum(z, 0.0) - jnp.log1p(jnp.exp(-jnp.abs(z)))


def _gla_conv_kernel(q_ref, k_ref, v_ref, r_ref, gb_ref, gc_ref, u_ref, lr_ref, gw_ref, gbias_ref,
                     onorm_ref, cw_ref, yg_ref, yc_ref, of_ref, ob_ref, sf_ref, sb_ref, xs_ref):
    ch = GLA_CHUNK
    grp = TM
    n_grp = L_ALL // grp
    cpg = grp // ch
    row = lax.broadcasted_iota(I32, (grp, grp), 0)
    col = lax.broadcasted_iota(I32, (grp, grp), 1)
    same_chunk = (row // ch) == (col // ch)
    tri = (jnp.logical_and(same_chunk, row >= col), jnp.logical_and(same_chunk, col >= row))
    tri_bf = (tri[0].astype(BF16), tri[1].astype(BF16))
    lane = lax.broadcasted_iota(I32, (grp, LANES), 1)
    head_mask = (lane < A_DK, lane >= A_DK)
    chunk_of_row = lax.broadcasted_iota(I32, (grp, LANES), 0) // ch

    sf_ref[...] = jnp.zeros_like(sf_ref)
    sb_ref[...] = jnp.zeros_like(sb_ref)

    def chunk_rows(x, idx):
        return jnp.concatenate(
            [jnp.broadcast_to(x[c * ch + idx:c * ch + idx + 1], (ch, x.shape[1])) for c in range(cpg)], axis=0)

    s_refs = (sf_ref, sb_ref)
    o_refs = (of_ref, ob_ref)
    dirs = (0, 1)
    pairs = [(d, hh) for d in dirs for hh in range(2)]

    def by_chunk_lanes(m, x):
        return jnp.concatenate([jnp.where(jnp.logical_and(m, chunk_of_row == c), x, 0.0) for c in range(cpg)],
                               axis=1).astype(BF16)

    def body(i, carry):
        gidx = (i, jnp.where(i == 0, 0, n_grp - i))
        rows = [pl.ds(pl.multiple_of(g * grp, grp), grp) for g in gidx]
        q = [q_ref[0, r, :].astype(F32) * (A_DK ** -0.5) for r in rows]
        k = [k_ref[0, r, :].astype(F32) for r in rows]
        v = [v_ref[0, r, :] for r in rows]
        z = [_dot(lr_ref[0, rows[d], :].astype(BF16), gw_ref[d]) + gbias_ref[d] for d in dirs]
        a = [_log_sigmoid(z[d]) * (1.0 / A_GATE_TAU) for d in dirs]
        a_hi = [a[d].astype(BF16) for d in dirs]
        a_lo = [(a[d] - a_hi[d].astype(F32)).astype(BF16) for d in dirs]
        cum = [_dot(tri_bf[d], a_hi[d]) + _dot(tri_bf[d], a_lo[d]) for d in dirs]
        ref = (chunk_rows(cum[0], ch // 2 - 1), chunk_rows(cum[1], ch // 2))
        last = (chunk_rows(cum[0], ch - 1), chunk_rows(cum[1], 0))
        order = (range(cpg), range(cpg - 1, -1, -1))
        qe = [q[d] * jnp.exp(cum[d]) for d in dirs]
        qt = [q[d] * jnp.exp(cum[d] - ref[d]) for d in dirs]
        kt = [(k[d] * jnp.exp(ref[d] - cum[d])).astype(BF16) for d in dirs]
        kl = [k[d] * jnp.exp(last[d] - cum[d]) for d in dirs]
        dec = [jnp.exp(last[d]) for d in dirs]
        vh = {(d, hh): v[d][:, hh * A_DV:(hh + 1) * A_DV].astype(BF16) for d, hh in pairs}
        sc = {(d, hh): _dot_nt(jnp.where(head_mask[hh], qt[d], 0.0).astype(BF16), kt[d]) for d, hh in pairs}
        kv = {(d, hh): _dot(vh[d, hh].T, by_chunk_lanes(head_mask[hh], kl[d])) for d, hh in pairs}
        o_intra = {(d, hh): _dot(jnp.where(tri[d], sc[d, hh], 0.0).astype(BF16), vh[d, hh]) for d, hh in pairs}
        for d, hh in pairs:
            st = s_refs[d][hh]
            states = [None] * cpg
            for c in order[d]:
                states[c] = st
                st = st * dec[d][c * ch:c * ch + 1] + kv[d, hh][:, c * LANES:(c + 1) * LANES]
            s_refs[d][hh] = st
            o_inter = _dot_nt(by_chunk_lanes(head_mask[hh], qe[d]), jnp.concatenate(states, axis=1).astype(BF16))
            o_refs[d][rows[d], hh * A_DV:(hh + 1) * A_DV] = o_intra[d, hh] + o_inter
        return carry

    lax.fori_loop(0, n_grp, body, 0)

    pad = SUBLANES
    xs_ref[0:pad, :] = jnp.zeros((pad, xs_ref.shape[1]), F32)
    xs_ref[pad + L_ALL:, :] = jnp.zeros((pad, xs_ref.shape[1]), F32)
    xs_ref[pad:pad + L_ALL, :] = gc_ref[0].astype(F32) * u_ref[0].astype(F32)

    cw = cw_ref[...]
    onorm = onorm_ref[...]
    trow = lax.broadcasted_iota(I32, (TM, 1), 0)
    for ti in range(L_ALL // TM):
        s0 = ti * TM
        rows = slice(s0, s0 + TM)
        o = of_ref[rows, :] + ob_ref[rows, :]
        r = r_ref[0, rows, :].astype(F32)
        parts = []
        for hh in range(2):
            oh = o[:, hh * A_DV:(hh + 1) * A_DV]
            ms = jnp.mean(oh * oh, axis=-1, keepdims=True)
            parts.append(oh * lax.rsqrt(ms + EPS) * onorm[:, hh * A_DV:(hh + 1) * A_DV])
        on = jnp.concatenate(parts, axis=1)
        yg_ref[0, rows, :] = (on * (r * _sigmoid(r))).astype(BF16)
        xm1 = xs_ref[pad + s0 - 1:pad + s0 - 1 + TM, :]
        x0 = xs_ref[pad + s0:pad + s0 + TM, :]
        xp1 = xs_ref[pad + s0 + 1:pad + s0 + 1 + TM, :]
        if s0 + TM == CTX_LEN:
            xp1 = jnp.where(trow == TM - 1, 0.0, xp1)
        if s0 == CTX_LEN:
            xm1 = jnp.where(trow == 0, 0.0, xm1)
        conv = cw[0:1] * xm1 + cw[1:2] * x0 + cw[2:3] * xp1 + cw[3:4]
        yc_ref[0, rows, :] = (gb_ref[0, rows, :].astype(F32) * conv).astype(BF16)


def _gla_conv(p, lr, gw_pad, gbias, onorm, cw):
    hw = 2 * A_DV

    def pspec(width, base):
        return pl.BlockSpec((1, L_ALL, width), lambda b, i: (b, 0, base + i))

    return pl.pallas_call(
        _gla_conv_kernel,
        grid=(BATCH, A_HEADS // 2),
        in_specs=[
            pspec(LANES, 0),
            pspec(LANES, 2),
            pspec(hw, 2),
            pspec(hw, 4),
            pspec(hw, 6),
            pspec(hw, 8),
            pspec(hw, 10),
            pl.BlockSpec((1, L_ALL, LANES), lambda b, i: (b, 0, 0)),
            pl.BlockSpec((2, LANES, LANES), lambda b, i: (0, 0, i)),
            pl.BlockSpec((2, 1, LANES), lambda b, i: (0, 0, i)),
            pl.BlockSpec((1, hw), lambda b, i: (0, i)),
            pl.BlockSpec((8, hw), lambda b, i: (0, i)),
        ],
        out_specs=[
            pl.BlockSpec((1, L_ALL, hw), lambda b, i: (b, 0, i)),
            pl.BlockSpec((1, L_ALL, hw), lambda b, i: (b, 0, i)),
        ],
        out_shape=[
            jax.ShapeDtypeStruct((BATCH, L_ALL, A_HEADS * A_DV), BF16),
            jax.ShapeDtypeStruct((BATCH, L_ALL, B_WIDTH), BF16),
        ],
        scratch_shapes=[
            pltpu.VMEM((L_ALL, hw), F32),
            pltpu.VMEM((L_ALL, hw), F32),
            pltpu.VMEM((2, A_DV, LANES), F32),
            pltpu.VMEM((2, A_DV, LANES), F32),
            pltpu.VMEM((L_ALL + 2 * SUBLANES, hw), F32),
        ],
        compiler_params=_params(("arbitrary", "arbitrary"), VMEM_LIMIT),
        name="gla_conv",
    )(p, p, p, p, p, p, p, lr, gw_pad, gbias, onorm, cw)


def _outproj_kernel(n_in, n_res, *refs):
    y_refs = refs[:n_in]
    w_refs = refs[n_in:2 * n_in]
    res_refs = refs[2 * n_in:2 * n_in + n_res]
    mod_ref, g_ref, rwt_ref, rb_ref, xo_ref, h_ref, te_ref, tg_ref, cnt_ref = refs[2 * n_in + n_res:]
    is_ctx = pl.program_id(1) == 0
    mod = mod_ref[0, 0]
    wt = rwt_ref[...]
    wb = wt.astype(BF16)
    wl = (wt - wb.astype(F32)).astype(BF16)
    eidx = lax.broadcasted_iota(I32, (N_EXPERTS, LANES), 0)
    cnt = jnp.zeros((N_EXPERTS, 1), I32)
    groups = [slice(r0, r0 + LANES) for r0 in range(0, TM, LANES)]
    accs = []
    for rows in groups:
        acc = _dot(y_refs[0][0, rows, :], w_refs[0][...])
        for i in range(1, n_in):
            acc = acc + _dot(y_refs[i][0, rows, :], w_refs[i][...])
        accs.append(acc)
    hs = []
    for rows, acc in zip(groups, accs):
        xres = res_refs[0][0, rows, :]
        if n_res == 2:
            xres = jnp.where(is_ctx, xres, res_refs[1][0, rows, :])
        xn = xres + mod[2:3] * acc
        xo_ref[0, rows, :] = xn
        h = _norm_mod(xn, g_ref[...], mod[3:4], mod[4:5])
        h_ref[rows, :] = h.astype(BF16)
        hs.append(h)
    all_logits = []
    for h in hs:
        hb = h.astype(BF16)
        hl = (h - hb.astype(F32)).astype(BF16)
        all_logits.append(_dot_nt(wb, hb) + _dot_nt(wb, hl) + _dot_nt(wl, hb) + rb_ref[...])
    for rows, logits in zip(groups, all_logits):
        cur = logits
        vals, idxs = [], []
        for _ in range(TOP_K):
            m = jnp.max(cur, axis=0, keepdims=True)
            sel = jnp.min(jnp.where(cur == m, eidx, N_EXPERTS), axis=0, keepdims=True)
            vals.append(m)
            idxs.append(sel)
            cur = jnp.where(eidx == sel, -jnp.inf, cur)
        ex = [jnp.exp(v - vals[0]) for v in vals]
        den = ex[0] + ex[1] + ex[2] + ex[3]
        zi = jnp.zeros_like(idxs[0])
        zf = jnp.zeros_like(den)
        te_ref[:, rows] = jnp.concatenate(idxs + [zi] * (8 - TOP_K), axis=0)
        tg_ref[:, rows] = jnp.concatenate([e / den for e in ex] + [zf] * (8 - TOP_K), axis=0)
        onehot = jnp.where(eidx == idxs[0], 1, 0)
        for k in range(1, TOP_K):
            onehot = onehot + jnp.where(eidx == idxs[k], 1, 0)
        cnt = cnt + jnp.sum(onehot, axis=1, keepdims=True)
    cnt_ref[0] = jnp.broadcast_to(cnt, (N_EXPERTS, LANES))


def _outproj(ys, ws, res, res_specs, n_tiles, mod, mod_seg, g, rwt, rb):
    n_in = len(ys)
    n_tok = BATCH * n_tiles * TM
    in_specs = []
    for y in ys:
        in_specs.append(pl.BlockSpec((1, TM, y.shape[2]), lambda b, j: (b, j, 0)))
    for w in ws:
        in_specs.append(pl.BlockSpec(w.shape, lambda b, j: (0, 0)))
    in_specs += list(res_specs) + [
        pl.BlockSpec((1, 1, 8, D_MODEL), lambda b, j: (b, mod_seg(j), 0, 0)),
        pl.BlockSpec((1, D_MODEL), lambda b, j: (0, 0)),
        pl.BlockSpec((N_EXPERTS, D_MODEL), lambda b, j: (0, 0)),
        pl.BlockSpec((N_EXPERTS, 1), lambda b, j: (0, 0)),
    ]
    return pl.pallas_call(
        functools.partial(_outproj_kernel, n_in, len(res)),
        grid=(BATCH, n_tiles),
        in_specs=in_specs,
        out_specs=[
            pl.BlockSpec((1, TM, D_MODEL), lambda b, j: (b, j, 0)),
            pl.BlockSpec((TM, D_MODEL), lambda b, j: (b * n_tiles + j, 0)),
            pl.BlockSpec((8, TM), lambda b, j: (0, b * n_tiles + j)),
            pl.BlockSpec((8, TM), lambda b, j: (0, b * n_tiles + j)),
            pl.BlockSpec((1, N_EXPERTS, LANES), lambda b, j: (b * n_tiles + j, 0, 0)),
        ],
        out_shape=[
            jax.ShapeDtypeStruct((BATCH, n_tiles * TM, D_MODEL), F32),
            jax.ShapeDtypeStruct((n_tok, D_MODEL), BF16),
            jax.ShapeDtypeStruct((8, n_tok), I32),
            jax.ShapeDtypeStruct((8, n_tok), F32),
            jax.ShapeDtypeStruct((BATCH * n_tiles, N_EXPERTS, LANES), I32),
        ],
        compiler_params=_params(("arbitrary", "arbitrary"), VMEM_LIMIT),
        name="outproj_router",
    )(*ys, *ws, *res, mod, g, rwt, rb)


def _moe_rows(n_tok):
    n_assign = n_tok * TOP_K
    n_blocks = -(-(n_assign + N_EXPERTS * (MOE_BM - 1)) // MOE_BM)
    return n_assign, n_blocks


def _route_tables(cnt, n_tok):
    _, n_blocks = _moe_rows(n_tok)
    counts = jnp.sum(cnt, axis=0)
    padded = (counts + MOE_BM - 1) // MOE_BM * MOE_BM
    pad_end = jnp.cumsum(padded)
    block_start = pad_end - padded
    seg_start = block_start[None, :] + jnp.cumsum(cnt, axis=0) - cnt
    blk_row = jnp.arange(n_blocks, dtype=I32) * MOE_BM
    block_e = jnp.minimum(jnp.sum((pad_end[None, :] <= blk_row[:, None]).astype(I32), axis=1), N_EXPERTS - 1)
    n_used = (pad_end[-1] // MOE_BM).reshape(1)
    eids = jnp.arange(N_EXPERTS, dtype=I32)
    later = jnp.where(jnp.logical_and(eids[None, :] > eids[:, None], counts[None, :] > 0), eids[None, :], N_EXPERTS)
    nxt = jnp.min(later, axis=1)
    next_e = jnp.where(nxt == N_EXPERTS, -1, nxt)
    return dict(block_e=block_e.astype(I32), n_used=n_used.astype(I32),
                row_end=(block_start + counts).astype(I32), next_e=next_e.astype(I32),
                seg_start=seg_start.reshape(-1).astype(I32), cnt=cnt.reshape(-1).astype(I32),
                pad_start=(block_start + counts).astype(I32), pad_len=(padded - counts).astype(I32))


def _store_row_slabs(ref, val, n_rows):
    for s in range(ROW_SLABS):
        ref[pl.ds(s, n_rows, stride=ROW_SLABS), :] = val[:, s * LANES:(s + 1) * LANES]


def _load_row_slabs(ref, n_rows):
    return jnp.concatenate(
        [ref[pl.ds(s, n_rows, stride=ROW_SLABS), :] for s in range(ROW_SLABS)], axis=1)


def _slab_rows(start, n):
    return pl.ds(pl.multiple_of(start * ROW_SLABS, ROW_SLABS), n * ROW_SLABS)


def _local_positions(te):
    eidx = lax.broadcasted_iota(I32, (N_EXPERTS, TM), 0)
    hits = [te[k:k + 1] == eidx for k in range(TOP_K)]
    onehot = jnp.where(hits[0], 1.0, 0.0)
    for k in range(1, TOP_K):
        onehot = onehot + jnp.where(hits[k], 1.0, 0.0)
    mb = onehot.astype(BF16)
    trow = lax.broadcasted_iota(I32, (TM, TM), 0)
    tcol = lax.broadcasted_iota(I32, (TM, TM), 1)
    before = _dot(mb, (trow < tcol).astype(BF16))
    totals = _dot(mb, jnp.ones((TM, TM), BF16))
    erow = lax.broadcasted_iota(I32, (N_EXPERTS, N_EXPERTS), 0)
    ecol = lax.broadcasted_iota(I32, (N_EXPERTS, N_EXPERTS), 1)
    first = _dot((ecol < erow).astype(BF16), totals.astype(BF16))
    base = first + before
    return [jnp.sum(jnp.where(hits[k], base, 0.0), axis=0, keepdims=True).astype(I32) for k in range(TOP_K)]


def _dispatch_kernel(n_blocks, ss_ref, cn_ref, ps_ref, pl_ref, nu_ref, h_ref, te_ref, xs_hbm, lpos_ref,
                     sbuf, zbuf, sem):
    j = pl.program_id(0)
    n_tiles = pl.num_programs(0)

    def zero_fill(act):
        def per_expert(e, carry):
            n = pl_ref[e]

            @pl.when(n > 0)
            def _():
                act(pltpu.make_async_copy(zbuf.at[_slab_rows(0, n)], xs_hbm.at[_slab_rows(ps_ref[e], n)],
                                          sem.at[MOE_RING]))
            return carry
        lax.fori_loop(0, N_EXPERTS, per_expert, 0)

        def per_block(b, carry):
            act(pltpu.make_async_copy(zbuf, xs_hbm.at[_slab_rows(b * MOE_BM, MOE_BM)], sem.at[MOE_RING]))
            return carry
        lax.fori_loop(nu_ref[0], n_blocks, per_block, 0)

    @pl.when(j == 0)
    def _():
        zbuf[...] = jnp.zeros_like(zbuf)
        zero_fill(lambda cp: cp.start())

    lpos = _local_positions(te_ref[...])
    zi = jnp.zeros_like(lpos[0])
    lpos_ref[...] = jnp.concatenate(lpos + [zi] * (8 - TOP_K), axis=0)
    riota = lax.broadcasted_iota(I32, (TILE_ROWS, TM), 0)
    hit = riota == lpos[0]
    for k in range(1, TOP_K):
        hit = jnp.logical_or(hit, riota == lpos[k])
    perm = jnp.where(hit, 1.0, 0.0).astype(BF16)
    hb = h_ref[...].astype(BF16)
    slot = lax.rem(j, MOE_RING)
    buf = sbuf.at[slot]

    def rows_copy(sl):
        return pltpu.make_async_copy(sbuf.at[sl], xs_hbm.at[_slab_rows(0, TILE_ROWS)], sem.at[sl])

    @pl.when(j >= MOE_RING)
    def _():
        rows_copy(slot).wait()

    for s in range(0, ROW_SLABS, 2):
        xl = _dot(perm, hb[:, s * LANES:(s + 2) * LANES])
        buf[pl.ds(s, TILE_ROWS, stride=ROW_SLABS), :] = xl[:, :LANES]
        buf[pl.ds(s + 1, TILE_ROWS, stride=ROW_SLABS), :] = xl[:, LANES:]

    local = 0
    for e in range(N_EXPERTS):
        n = cn_ref[j * N_EXPERTS + e]

        @pl.when(n > 0)
        def _(e=e, n=n, local=local):
            pltpu.make_async_copy(buf.at[_slab_rows(local, n)],
                                  xs_hbm.at[_slab_rows(ss_ref[j * N_EXPERTS + e], n)],
                                  sem.at[slot]).start(priority=e % 2)
        local = local + n

    @pl.when(j == n_tiles - 1)
    def _():
        for back in range(MOE_RING):
            rows_copy(lax.rem(j + MOE_RING - back, MOE_RING)).wait()
        zero_fill(lambda cp: cp.wait())


def _dispatch(h, top_e, tabs, n_tok):
    _, n_blocks = _moe_rows(n_tok)
    n_tiles = n_tok // TM
    grid_spec = pltpu.PrefetchScalarGridSpec(
        num_scalar_prefetch=5,
        grid=(n_tiles,),
        in_specs=[
            pl.BlockSpec((TM, D_MODEL), lambda j, *_: (j, 0)),
            pl.BlockSpec((8, TM), lambda j, *_: (0, j)),
        ],
        out_specs=[
            pl.BlockSpec(memory_space=pl.ANY),
            pl.BlockSpec((8, TM), lambda j, *_: (0, j)),
        ],
        scratch_shapes=[
            pltpu.VMEM((MOE_RING, TILE_ROWS * ROW_SLABS, LANES), F32),
            pltpu.VMEM((MOE_BM * ROW_SLABS, LANES), F32),
            pltpu.SemaphoreType.DMA((MOE_RING + 1,)),
        ],
    )
    return pl.pallas_call(
        functools.partial(_dispatch_kernel, n_blocks),
        grid_spec=grid_spec,
        out_shape=[
            jax.ShapeDtypeStruct((n_blocks * MOE_BM * ROW_SLABS, LANES), F32),
            jax.ShapeDtypeStruct((8, n_tok), I32),
        ],
        compiler_params=_params(("arbitrary",), VMEM_LIMIT),
        name="moe_dispatch",
    )(tabs["seg_start"], tabs["cnt"], tabs["pad_start"], tabs["pad_len"], tabs["n_used"], h, top_e)


def _moe_kernel(layer, be_ref, nu_ref, end_ref, ne_ref, x_ref, b1_ref, b2_ref, w1_hbm, w2_hbm, y_ref,
                w1s, w2s, w1b, w2b, sem):
    i = pl.program_id(0)
    nu = nu_ref[0]

    def weight_copies(e):
        return (pltpu.make_async_copy(w1_hbm.at[layer, e], w1s, sem.at[0]),
                pltpu.make_async_copy(w2_hbm.at[layer, e], w2s, sem.at[1]))

    @pl.when(i < nu)
    def _():
        e = be_ref[i]

        @pl.when(i == 0)
        def _():
            for cp in weight_copies(e):
                cp.start()

        @pl.when(jnp.logical_or(i == 0, e != be_ref[jnp.maximum(i - 1, 0)]))
        def _():
            for cp in weight_copies(e):
                cp.wait()
            w1b[...] = w1s[...].astype(BF16)
            w2b[...] = w2s[...].astype(BF16)
            nxt = ne_ref[e]

            @pl.when(nxt >= 0)
            def _():
                for cp in weight_copies(nxt):
                    cp.start()

        def run(n_rows):
            slab_rows = pl.ds(0, n_rows * ROW_SLABS)
            x = _load_row_slabs(x_ref.at[slab_rows], n_rows).astype(BF16)
            h1 = _dot(x, w1b[...]) + b1_ref[0, 0]
            gate = jnp.minimum(h1[:, :D_MODEL], SWIGLU_LIMIT)
            up = jnp.clip(h1[:, D_MODEL:], -SWIGLU_LIMIT, SWIGLU_LIMIT)
            act = (up + 1.0) * gate * _sigmoid(SWIGLU_ALPHA * gate)
            y = _dot(act.astype(BF16), w2b[...]) + b2_ref[0, 0]
            _store_row_slabs(y_ref.at[slab_rows], y, n_rows)

        n_valid = end_ref[e] - i * MOE_BM

        @pl.when(n_valid > MOE_HALF)
        def _():
            run(MOE_BM)

        @pl.when(n_valid <= MOE_HALF)
        def _():
            run(MOE_HALF)
            y_ref[pl.ds(MOE_HALF * ROW_SLABS, MOE_HALF * ROW_SLABS), :] = jnp.zeros(
                (MOE_HALF * ROW_SLABS, LANES), F32)

    @pl.when(i >= nu)
    def _():
        y_ref[...] = jnp.zeros_like(y_ref)


def _moe(xs, tabs, layer, w1, b1, w2, b2, n_tok):
    _, n_blocks = _moe_rows(n_tok)
    d2 = 2 * D_MODEL
    blk = MOE_BM * ROW_SLABS
    grid_spec = pltpu.PrefetchScalarGridSpec(
        num_scalar_prefetch=4,
        grid=(n_blocks,),
        in_specs=[
            pl.BlockSpec((blk, LANES), lambda i, be, nu, *_: (jnp.minimum(i, nu[0] - 1), 0)),
            pl.BlockSpec((1, 1, 1, d2), lambda i, be, *_: (layer, be[i], 0, 0)),
            pl.BlockSpec((1, 1, 1, D_MODEL), lambda i, be, *_: (layer, be[i], 0, 0)),
            pl.BlockSpec(memory_space=pl.ANY),
            pl.BlockSpec(memory_space=pl.ANY),
        ],
        out_specs=pl.BlockSpec((blk, LANES), lambda i, *_: (i, 0)),
        scratch_shapes=[
            pltpu.VMEM((D_MODEL, d2), F32),
            pltpu.VMEM((D_MODEL, D_MODEL), F32),
            pltpu.VMEM((D_MODEL, d2), BF16),
            pltpu.VMEM((D_MODEL, D_MODEL), BF16),
            pltpu.SemaphoreType.DMA((2,)),
        ],
    )
    return pl.pallas_call(
        functools.partial(_moe_kernel, layer),
        grid_spec=grid_spec,
        out_shape=jax.ShapeDtypeStruct((n_blocks * blk, LANES), F32),
        compiler_params=_params(("arbitrary",), VMEM_LIMIT),
        name="moe_experts",
    )(tabs["block_e"], tabs["n_used"], tabs["row_end"], tabs["next_e"], xs,
      b1.reshape(DEPTH, N_EXPERTS, 1, d2), b2.reshape(DEPTH, N_EXPERTS, 1, D_MODEL), w1, w2)


def _combine_kernel(n_next, n_tiles, ss_ref, cn_ref, lpos_ref, tg_ref, x_ref, mod_ref, g_ref, ys_hbm, *rest):
    final = n_next == 0
    next_in, (o_ref, *next_out), (cbuf, sem) = rest[:n_next], rest[n_next:len(rest) - 2], rest[len(rest) - 2:]
    t = pl.program_id(0) * n_tiles + pl.program_id(1)
    n_total = pl.num_programs(0) * n_tiles
    slot = lax.rem(t, MOE_RING)
    ahead = MOE_RING - 1

    def segment_copy(tile, sl, e, local, n):
        return pltpu.make_async_copy(ys_hbm.at[_slab_rows(ss_ref[tile * N_EXPERTS + e], n)],
                                     cbuf.at[sl, _slab_rows(local, n)], sem.at[sl])

    @pl.when(t == 0)
    def _():
        for first in range(ahead):
            def segment(e, local, first=first):
                n = cn_ref[first * N_EXPERTS + e]

                @pl.when(n > 0)
                def _():
                    segment_copy(first, first, e, local, n).start()
                return local + n
            lax.fori_loop(0, N_EXPERTS, segment, 0)

    @pl.when(t + ahead < n_total)
    def _():
        tile = t + ahead
        sl = lax.rem(tile, MOE_RING)
        local = 0
        for e in range(N_EXPERTS):
            n = cn_ref[tile * N_EXPERTS + e]

            @pl.when(n > 0)
            def _(e=e, n=n, local=local):
                segment_copy(tile, sl, e, local, n).start(priority=e % 2)
            local = local + n

    pltpu.make_async_copy(ys_hbm.at[_slab_rows(0, TILE_ROWS)], cbuf.at[slot], sem.at[slot]).wait()

    lpos = lpos_ref[...]
    tg = tg_ref[...]
    riota = lax.broadcasted_iota(I32, (TILE_ROWS, TM), 0)
    hit = riota == lpos[0:1]
    gsel = jnp.where(hit, tg[0:1], 0.0)
    for k in range(1, TOP_K):
        hit_k = riota == lpos[k:k + 1]
        gsel = gsel + jnp.where(hit_k, tg[k:k + 1], 0.0)
        hit = jnp.logical_or(hit, hit_k)
    row_gate = jnp.sum(gsel, axis=1, keepdims=True)
    yg = (_load_row_slabs(cbuf.at[slot], TILE_ROWS) * row_gate).astype(BF16)
    f = _dot_tn(jnp.where(hit, 1.0, 0.0).astype(BF16), yg)

    mod = mod_ref[0, 0]
    xn = x_ref[0] + mod[5:6] * f
    if final:
        ms = jnp.mean(xn * xn, axis=-1, keepdims=True)
        xn = xn * lax.rsqrt(ms + EPS) * g_ref[...]
    o_ref[0] = xn
    if not final:
        _qkv_body(xn, *next_in, *next_out)


def _combine(ys, tabs, lpos, top_g, xres, n_tiles, mod, mod_seg, g, next_qkv):
    final = next_qkv is None
    in_specs = [
        pl.BlockSpec((8, TM), lambda b, j, *_: (0, b * n_tiles + j)),
        pl.BlockSpec((8, TM), lambda b, j, *_: (0, b * n_tiles + j)),
        pl.BlockSpec((1, TM, D_MODEL), lambda b, j, *_: (b, j, 0)),
        pl.BlockSpec((1, 1, 8, D_MODEL), lambda b, j, *_: (b, mod_seg(j), 0, 0)),
        pl.BlockSpec((1, D_MODEL), lambda b, j, *_: (0, 0)),
        pl.BlockSpec(memory_space=pl.ANY),
    ]
    out_specs = [pl.BlockSpec((1, TM, D_MODEL), lambda b, j, *_: (b, j, 0))]
    out_shape = [jax.ShapeDtypeStruct((BATCH, n_tiles * TM, D_MODEL), F32)]
    extra = ()
    if not final:
        qkv_in, qkv_out, qkv_shape = _qkv_specs()
        in_specs += qkv_in
        out_specs += qkv_out
        out_shape += qkv_shape
        extra = tuple(next_qkv)
    grid_spec = pltpu.PrefetchScalarGridSpec(
        num_scalar_prefetch=2,
        grid=(BATCH, n_tiles),
        in_specs=in_specs,
        out_specs=out_specs,
        scratch_shapes=[
            pltpu.VMEM((MOE_RING, TILE_ROWS * ROW_SLABS, LANES), F32),
            pltpu.SemaphoreType.DMA((MOE_RING,)),
        ],
    )
    out = pl.pallas_call(
        functools.partial(_combine_kernel, len(extra), n_tiles),
        grid_spec=grid_spec,
        out_shape=out_shape,
        compiler_params=_params(("arbitrary", "arbitrary"), VMEM_LIMIT),
        name="moe_combine",
    )(tabs["seg_start"], tabs["cnt"], lpos, top_g, xres, mod, g, ys, *extra)
    return out[0] if final else out


def _moe_ffn(h, top_e, top_g, cnt, layer, xres, n_tiles, mod, mod_seg, w1, b1, w2, b2, g, next_qkv):
    n_tok = BATCH * n_tiles * TM
    tabs = _route_tables(cnt[:, :, 0], n_tok)
    xs, lpos = _dispatch(h, top_e, tabs, n_tok)
    ys = _moe(xs, tabs, layer, w1, b1, w2, b2, n_tok)
    return _combine(ys, tabs, lpos, top_g, xres, n_tiles, mod, mod_seg, g, next_qkv)


def _qkv_body(x, mod_ref, g_ref, w_ref, qn_ref, kn_ref, cos_ref, sin_ref, q_ref, k_ref, v_ref):
    mod = mod_ref[0, 0]
    h = _norm_mod(x, g_ref[...], mod[0:1], mod[1:2]).astype(BF16)
    qkv = _dot(h, w_ref[...])
    cos = cos_ref[...]
    sin = sin_ref[...]

    def head(xh, gn):
        ms = jnp.mean(xh * xh, axis=-1, keepdims=True)
        y = xh * lax.rsqrt(ms + EPS) * gn
        return y * cos + pltpu.roll(y, C_HEAD_DIM // 2, 1) * sin

    qn = qn_ref[...]
    kn = kn_ref[...]
    for hq in range(C_HEADS):
        sl = slice(hq * C_HEAD_DIM, (hq + 1) * C_HEAD_DIM)
        q_ref[0, :, sl] = head(qkv[:, sl], qn).astype(BF16)
    for hk in range(C_KV_HEADS):
        src = slice((C_HEADS + hk) * C_HEAD_DIM, (C_HEADS + hk + 1) * C_HEAD_DIM)
        k_ref[0, :, hk * C_HEAD_DIM:(hk + 1) * C_HEAD_DIM] = head(qkv[:, src], kn).astype(BF16)
    ones = jnp.ones((TM, C_HEAD_DIM), BF16)
    for hk in range(C_KV_HEADS):
        src = slice((C_HEADS + C_KV_HEADS + hk) * C_HEAD_DIM, (C_HEADS + C_KV_HEADS + hk + 1) * C_HEAD_DIM)
        v_ref[0, :, 2 * hk * C_HEAD_DIM:(2 * hk + 1) * C_HEAD_DIM] = qkv[:, src].astype(BF16)
        v_ref[0, :, (2 * hk + 1) * C_HEAD_DIM:(2 * hk + 2) * C_HEAD_DIM] = ones


def _qkv_specs():
    kvw = C_KV_HEADS * C_HEAD_DIM
    in_specs = [
        pl.BlockSpec((1, 1, 8, D_MODEL), lambda b, j, *_: (b, jnp.minimum(j, 1), 0, 0)),
        pl.BlockSpec((1, D_MODEL), lambda b, j, *_: (0, 0)),
        pl.BlockSpec((D_MODEL, C_QKV), lambda b, j, *_: (0, 0)),
        pl.BlockSpec((1, C_HEAD_DIM), lambda b, j, *_: (0, 0)),
        pl.BlockSpec((1, C_HEAD_DIM), lambda b, j, *_: (0, 0)),
        pl.BlockSpec((TM, C_HEAD_DIM), lambda b, j, *_: (j, 0)),
        pl.BlockSpec((TM, C_HEAD_DIM), lambda b, j, *_: (j, 0)),
    ]
    out_specs = [
        pl.BlockSpec((1, TM, D_MODEL), lambda b, j, *_: (b, jnp.maximum(j - 1, 0), 0)),
        pl.BlockSpec((1, TM, kvw), lambda b, j, *_: (b, j, 0)),
        pl.BlockSpec((1, TM, 2 * kvw), lambda b, j, *_: (b, j, 0)),
    ]
    out_shape = [
        jax.ShapeDtypeStruct((BATCH, SEQ, D_MODEL), BF16),
        jax.ShapeDtypeStruct((BATCH, L_ALL, kvw), BF16),
        jax.ShapeDtypeStruct((BATCH, L_ALL, 2 * kvw), BF16),
    ]
    return in_specs, out_specs, out_shape


def _rope_perm():
    qd = C_HEAD_DIM // 4
    order = (0, 2, 1, 3)
    return [blk * qd + i for blk in order for i in range(qd)]


def _rope_tables():
    rows = SEQ // GRID_W
    row = jnp.repeat(jnp.arange(rows), GRID_W).astype(F32)
    col = jnp.tile(jnp.arange(GRID_W), rows).astype(F32)
    half = C_HEAD_DIM // 2
    inv_freq = ROPE_THETA ** (-jnp.arange(0, half, 2, dtype=F32) / half)
    ar = row[:, None] * inv_freq
    ac = col[:, None] * inv_freq
    cos = jnp.concatenate([jnp.cos(ar), jnp.cos(ac), jnp.cos(ar), jnp.cos(ac)], axis=-1)
    sin = jnp.concatenate([-jnp.sin(ar), -jnp.sin(ac), jnp.sin(ar), jnp.sin(ac)], axis=-1)
    cos = jnp.concatenate([jnp.ones((CTX_LEN, C_HEAD_DIM), F32), cos], axis=0)
    sin = jnp.concatenate([jnp.zeros((CTX_LEN, C_HEAD_DIM), F32), sin], axis=0)
    return cos, sin


def _attn_kernel(q_ref, k_ref, v_ref, o_ref):
    c = (C_HEAD_DIM ** -0.5) * LOG2_E
    def head_cols(h):
        return slice(h * C_HEAD_DIM, (h + 1) * C_HEAD_DIM)

    def scores(h):
        return _dot_nt(q_ref[0, :, head_cols(h)], k_ref[0, :, head_cols(h // C_GROUP)])

    s_next = scores(0)
    for h in range(C_HEADS):
        s = s_next
        if h + 1 < C_HEADS:
            s_next = scores(h + 1)
        g = h // C_GROUP
        m = jnp.max(s, axis=-1, keepdims=True)
        p = jnp.exp2((s - m) * c).astype(BF16)
        oe = _dot(p, v_ref[0, :, 2 * g * C_HEAD_DIM:(2 * g + 2) * C_HEAD_DIM])
        o_ref[0, :, head_cols(h)] = (oe[:, :C_HEAD_DIM] / oe[:, C_HEAD_DIM:]).astype(BF16)


def _attention(q, k, v):
    kvw = C_KV_HEADS * C_HEAD_DIM
    return pl.pallas_call(
        _attn_kernel,
        grid=(BATCH, SEQ // ATT_TQ),
        in_specs=[
            pl.BlockSpec((1, ATT_TQ, D_MODEL), lambda b, j: (b, j, 0)),
            pl.BlockSpec((1, L_ALL, kvw), lambda b, j: (b, 0, 0)),
            pl.BlockSpec((1, L_ALL, 2 * kvw), lambda b, j: (b, 0, 0)),
        ],
        out_specs=pl.BlockSpec((1, ATT_TQ, D_MODEL), lambda b, j: (b, j, 0)),
        out_shape=jax.ShapeDtypeStruct((BATCH, SEQ, D_MODEL), BF16),
        compiler_params=_params(("arbitrary", "arbitrary"), VMEM_LIMIT),
        name="attention",
    )(q, k, v)


def _mod_table(ada_layer):
    m = ada_layer.reshape(16, N_MOD, D_MODEL)
    m_lat = m[:BATCH]
    m_ctx = jnp.broadcast_to(m[BATCH], (BATCH, N_MOD, D_MODEL))
    t = jnp.stack([m_ctx, m_lat], axis=1)
    return jnp.pad(t, ((0, 0), (0, 0), (0, 8 - N_MOD), (0, 0)))


def kernel(x, c, ctx, c_ctx, ada_w, ada_b, norm_mix, norm_ffn, ab_w_in, ab_gate_w, ab_gate_b, ab_out_norm,
           ab_conv_w, ab_conv_b, ab_w_out, attn_w_qkv, attn_q_norm, attn_k_norm, attn_w_o, router_w,
           router_b, moe_w1, moe_b1, moe_w2, moe_b2, final_norm):
    cond = jnp.zeros((16, D_MODEL), F32).at[:BATCH].set(c).at[BATCH].set(c_ctx)
    ada = _ada(cond, ada_w, ada_b)
    mod0 = _mod_table(ada[0])
    mod1 = _mod_table(ada[1])
    seg_all = lambda j: jnp.minimum(j, 1)
    seg_lat = lambda j: 1
    n_tiles_all = L_ALL // TM
    n_tiles_lat = SEQ // TM
    fnorm = final_norm.reshape(1, D_MODEL)

    w_in = ab_w_in[0]
    lr0 = 2 * A_HEADS * A_DK + A_HEADS * A_DV
    lr1 = lr0 + 2 * A_GATE_RANK
    w_main = jnp.concatenate([w_in[:, :lr0], w_in[:, lr1:]], axis=1).astype(BF16)
    w_lr = jnp.pad(w_in[:, lr0:lr1], ((0, 0), (0, LANES - 2 * A_GATE_RANK))).astype(BF16)
    p, lr = _proj_in(ctx, x, mod0, norm_mix[0:1], w_main, w_lr)

    gw = ab_gate_w[0]
    gw_pad = jnp.zeros((2, LANES, A_HEADS * A_DK), F32)
    gw_pad = gw_pad.at[0, :A_GATE_RANK].set(gw[0]).at[1, A_GATE_RANK:2 * A_GATE_RANK].set(gw[1])
    cw = jnp.concatenate([ab_conv_w[0], ab_conv_b[0][None], jnp.zeros((4, B_WIDTH), F32)], axis=0)
    yg, yc = _gla_conv(p, lr, gw_pad.astype(BF16), ab_gate_b[0].reshape(2, 1, A_HEADS * A_DK),
                       ab_out_norm[0:1], cw)

    w_out = ab_w_out[0].astype(BF16)
    ngla = A_HEADS * A_DV
    xmid, h, top_e, top_g, cnt = _outproj(
        [yg, yc], [w_out[:ngla], w_out[ngla:]], [ctx, x], _ctx_latent_specs(), n_tiles_all, mod0, seg_all,
        norm_ffn[0:1], router_w[0].T, router_b[0].reshape(N_EXPERTS, 1))
    cos, sin = _rope_tables()
    perm = _rope_perm()
    qk_heads = C_HEADS + C_KV_HEADS
    cols = [hd * C_HEAD_DIM + p for hd in range(qk_heads) for p in perm] + list(range(qk_heads * C_HEAD_DIM, C_QKV))
    w_qkv = attn_w_qkv[0][:, jnp.asarray(cols, I32)].astype(BF16)
    perm_idx = jnp.asarray(perm, I32)
    next_qkv = (mod1, norm_mix[1:2], w_qkv, attn_q_norm[0:1, perm_idx], attn_k_norm[0:1, perm_idx], cos, sin)
    x1, q, k, v = _moe_ffn(h, top_e, top_g, cnt, 0, xmid, n_tiles_all, mod0, seg_all,
                           moe_w1, moe_b1, moe_w2, moe_b2, fnorm, next_qkv)

    o = _attention(q, k, v)
    xmid, h, top_e, top_g, cnt = _outproj(
        [o], [attn_w_o[0].astype(BF16)], [x1],
        [pl.BlockSpec((1, TM, D_MODEL), lambda b, j: (b, j + CTX_LEN // TM, 0))], n_tiles_lat, mod1, seg_lat,
        norm_ffn[1:2], router_w[1].T, router_b[1].reshape(N_EXPERTS, 1))
    return _moe_ffn(h, top_e, top_g, cnt, 1, xmid, n_tiles_lat, mod1, seg_lat,
                    moe_w1, moe_b1, moe_w2, moe_b2, fnorm, None)
```

```python
import functools

import jax
import jax.numpy as jnp
from jax import lax
from jax.experimental import pallas as pl
from jax.experimental.pallas import tpu as pltpu

F32 = jnp.float32
BF16 = jnp.bfloat16
I32 = jnp.int32

D_MODEL = 1024
BATCH = 8
SEQ = 2048
DEPTH = 2
GRID_W = 64
CTX_LEN = 256
L_ALL = CTX_LEN + SEQ
N_MOD = 6
EPS = 1e-6

A_DV = 128
A_HEADS = 4
A_DK = 64
A_GATE_RANK = 16
A_GATE_TAU = 16.0
GLA_CHUNK = 64
B_WIDTH = 512
AB_MAIN = 3072

C_HEAD_DIM = 128
C_HEADS = 8
C_KV_HEADS = 2
C_GROUP = 4
C_QKV = (C_HEADS + 2 * C_KV_HEADS) * C_HEAD_DIM
ROPE_THETA = 10000.0
LOG2_E = 1.4426950408889634

N_EXPERTS = 32
TOP_K = 4
SWIGLU_LIMIT = 7.0
SWIGLU_ALPHA = 1.702

LANES = 128
SUBLANES = 8
ROW_SLABS = D_MODEL // LANES
TM = 256
ATT_TQ = 512
TILE_ROWS = TM * TOP_K
MOE_BM = 512
MOE_HALF = MOE_BM // 2
MOE_RING = 3
ADA_ROWS = 16
ADA_TN = 1536
VMEM_LIMIT = 56 * 1024 * 1024

assert CTX_LEN == TM and SEQ % TM == 0 and SEQ % ATT_TQ == 0
assert TM % GLA_CHUNK == 0 and BATCH + 1 <= ADA_ROWS and (N_MOD * D_MODEL) % ADA_TN == 0
assert MOE_RING <= SEQ // TM and TILE_ROWS % MOE_HALF == 0

NT_DIMS = (((1,), (1,)), ((), ()))
TN_DIMS = (((0,), (0,)), ((), ()))


def _dot(a, b):
    return jnp.dot(a, b, preferred_element_type=F32)


def _dot_nt(a, b):
    return lax.dot_general(a, b, NT_DIMS, preferred_element_type=F32)


def _dot_tn(a, b):
    return lax.dot_general(a, b, TN_DIMS, preferred_element_type=F32)


def _params(sem, vmem=None):
    return pltpu.CompilerParams(dimension_semantics=sem, vmem_limit_bytes=vmem)


def _norm_mod(x, g, shift, scale):
    ms = jnp.mean(x * x, axis=-1, keepdims=True)
    y = x * lax.rsqrt(ms + EPS) * g
    return y * (1.0 + scale) + shift


def _sigmoid(x):
    return 1.0 / (1.0 + jnp.exp(-x))


def _ada_kernel(cond_ref, w_ref, b_ref, o_ref):
    c = cond_ref[...]
    s = (c * _sigmoid(c)).astype(BF16)
    o_ref[0] = _dot(s, w_ref[0].astype(BF16)) + b_ref[0]


def _ada(cond, ada_w, ada_b):
    tn = ADA_TN
    n = N_MOD * D_MODEL
    return pl.pallas_call(
        _ada_kernel,
        grid=(DEPTH, n // tn),
        in_specs=[
            pl.BlockSpec((ADA_ROWS, D_MODEL), lambda l, j: (0, 0)),
            pl.BlockSpec((1, D_MODEL, tn), lambda l, j: (l, 0, j)),
            pl.BlockSpec((1, 1, tn), lambda l, j: (l, 0, j)),
        ],
        out_specs=pl.BlockSpec((1, ADA_ROWS, tn), lambda l, j: (l, 0, j)),
        out_shape=jax.ShapeDtypeStruct((DEPTH, ADA_ROWS, n), F32),
        compiler_params=_params(("arbitrary", "arbitrary"), VMEM_LIMIT),
        name="ada",
    )(cond, ada_w, ada_b.reshape(DEPTH, 1, n))


def _ctx_or_latent(ctx_ref, x_ref):
    return jnp.where(pl.program_id(1) == 0, ctx_ref[0], x_ref[0])


def _ctx_latent_specs():
    return [pl.BlockSpec((1, TM, D_MODEL), lambda b, j: (b, 0, 0)),
            pl.BlockSpec((1, TM, D_MODEL), lambda b, j: (b, jnp.maximum(j - 1, 0), 0))]


def _proj_in_kernel(ctx_ref, x_ref, mod_ref, g_ref, w_ref, wlr_ref, p_ref, lr_ref):
    mod = mod_ref[0, 0]
    h = _norm_mod(_ctx_or_latent(ctx_ref, x_ref), g_ref[...], mod[0:1], mod[1:2]).astype(BF16)
    p_ref[0] = _dot(h, w_ref[...]).astype(BF16)
    lr_ref[0] = _dot(h, wlr_ref[...]).astype(BF16)


def _proj_in(ctx, x, mod, g, w_main, w_lr):
    nj = L_ALL // TM
    return pl.pallas_call(
        _proj_in_kernel,
        grid=(BATCH, nj),
        in_specs=_ctx_latent_specs() + [
            pl.BlockSpec((1, 1, SUBLANES, D_MODEL), lambda b, j: (b, jnp.minimum(j, 1), 0, 0)),
            pl.BlockSpec((1, D_MODEL), lambda b, j: (0, 0)),
            pl.BlockSpec((D_MODEL, AB_MAIN), lambda b, j: (0, 0)),
            pl.BlockSpec((D_MODEL, LANES), lambda b, j: (0, 0)),
        ],
        out_specs=[
            pl.BlockSpec((1, TM, AB_MAIN), lambda b, j: (b, j, 0)),
            pl.BlockSpec((1, TM, LANES), lambda b, j: (b, j, 0)),
        ],
        out_shape=[
            jax.ShapeDtypeStruct((BATCH, L_ALL, AB_MAIN), BF16),
            jax.ShapeDtypeStruct((BATCH, L_ALL, LANES), BF16),
        ],
        compiler_params=_params(("arbitrary", "arbitrary"), VMEM_LIMIT),
        name="proj_in",
    )(ctx, x, mod, g, w_main, w_lr)


def _log_sigmoid(z):
    return jnp.minimum(z, 0.0) - jnp.log1p(jnp.exp(-jnp.abs(z)))


def _gla_conv_kernel(q_ref, k_ref, v_ref, r_ref, gb_ref, gc_ref, u_ref, lr_ref, gw_ref, gbias_ref,
                     onorm_ref, cw_ref, yg_ref, yc_ref, of_ref, ob_ref, sf_ref, sb_ref, xs_ref):
    ch = GLA_CHUNK
    grp = TM
    n_grp = L_ALL // grp
    cpg = grp // ch
    row = lax.broadcasted_iota(I32, (grp, grp), 0)
    col = lax.broadcasted_iota(I32, (grp, grp), 1)
    same_chunk = (row // ch) == (col // ch)
    tri = (jnp.logical_and(same_chunk, row >= col), jnp.logical_and(same_chunk, col >= row))
    tri_bf = (tri[0].astype(BF16), tri[1].astype(BF16))
    lane = lax.broadcasted_iota(I32, (grp, LANES), 1)
    head_mask = (lane < A_DK, lane >= A_DK)
    chunk_of_row = lax.broadcasted_iota(I32, (grp, LANES), 0) // ch

    sf_ref[...] = jnp.zeros_like(sf_ref)
    sb_ref[...] = jnp.zeros_like(sb_ref)

    def chunk_rows(x, idx):
        return jnp.concatenate(
            [jnp.broadcast_to(x[c * ch + idx:c * ch + idx + 1], (ch, x.shape[1])) for c in range(cpg)], axis=0)

    s_refs = (sf_ref, sb_ref)
    o_refs = (of_ref, ob_ref)
    dirs = (0, 1)
    pairs = [(d, hh) for d in dirs for hh in range(2)]

    def by_chunk_lanes(m, x):
        return jnp.concatenate([jnp.where(jnp.logical_and(m, chunk_of_row == c), x, 0.0) for c in range(cpg)],
                               axis=1).astype(BF16)

    def body(i, carry):
        gidx = (i, jnp.where(i == 0, 0, n_grp - i))
        rows = [pl.ds(pl.multiple_of(g * grp, grp), grp) for g in gidx]
        q = [q_ref[0, r, :].astype(F32) * (A_DK ** -0.5) for r in rows]
        k = [k_ref[0, r, :].astype(F32) for r in rows]
        v = [v_ref[0, r, :] for r in rows]
        z = [_dot(lr_ref[0, rows[d], :].astype(BF16), gw_ref[d]) + gbias_ref[d] for d in dirs]
        a = [_log_sigmoid(z[d]) * (1.0 / A_GATE_TAU) for d in dirs]
        a_hi = [a[d].astype(BF16) for d in dirs]
        a_lo = [(a[d] - a_hi[d].astype(F32)).astype(BF16) for d in dirs]
        cum = [_dot(tri_bf[d], a_hi[d]) + _dot(tri_bf[d], a_lo[d]) for d in dirs]
        ref = (chunk_rows(cum[0], ch // 2 - 1), chunk_rows(cum[1], ch // 2))
        last = (chunk_rows(cum[0], ch - 1), chunk_rows(cum[1], 0))
        order = (range(cpg), range(cpg - 1, -1, -1))
        qe = [q[d] * jnp.exp(cum[d]) for d in dirs]
        qt = [q[d] * jnp.exp(cum[d] - ref[d]) for d in dirs]
        kt = [(k[d] * jnp.exp(ref[d] - cum[d])).astype(BF16) for d in dirs]
        kl = [k[d] * jnp.exp(last[d] - cum[d]) for d in dirs]
        dec = [jnp.exp(last[d]) for d in dirs]
        vh = {(d, hh): v[d][:, hh * A_DV:(hh + 1) * A_DV].astype(BF16) for d, hh in pairs}
        sc = {(d, hh): _dot_nt(jnp.where(head_mask[hh], qt[d], 0.0).astype(BF16), kt[d]) for d, hh in pairs}
        kv = {(d, hh): _dot(vh[d, hh].T, by_chunk_lanes(head_mask[hh], kl[d])) for d, hh in pairs}
        o_intra = {(d, hh): _dot(jnp.where(tri[d], sc[d, hh], 0.0).astype(BF16), vh[d, hh]) for d, hh in pairs}
        for d, hh in pairs:
            st = s_refs[d][hh]
            states = [None] * cpg
            for c in order[d]:
                states[c] = st
                st = st * dec[d][c * ch:c * ch + 1] + kv[d, hh][:, c * LANES:(c + 1) * LANES]
            s_refs[d][hh] = st
            qem = jnp.where(head_mask[hh], qe[d], 0.0).astype(BF16)
            o_inter = jnp.concatenate(
                [_dot_nt(qem[c * ch:(c + 1) * ch], states[c].astype(BF16)) for c in range(cpg)], axis=0)
            o_refs[d][rows[d], hh * A_DV:(hh + 1) * A_DV] = o_intra[d, hh] + o_inter
        return carry

    lax.fori_loop(0, n_grp, body, 0)

    pad = SUBLANES
    xs_ref[0:pad, :] = jnp.zeros((pad, xs_ref.shape[1]), F32)
    xs_ref[pad + L_ALL:, :] = jnp.zeros((pad, xs_ref.shape[1]), F32)
    xs_ref[pad:pad + L_ALL, :] = gc_ref[0].astype(F32) * u_ref[0].astype(F32)

    cw = cw_ref[...]
    onorm = onorm_ref[...]
    trow = lax.broadcasted_iota(I32, (TM, 1), 0)
    for ti in range(L_ALL // TM):
        s0 = ti * TM
        rows = slice(s0, s0 + TM)
        o = of_ref[rows, :] + ob_ref[rows, :]
        r = r_ref[0, rows, :].astype(F32)
        parts = []
        for hh in range(2):
            oh = o[:, hh * A_DV:(hh + 1) * A_DV]
            ms = jnp.mean(oh * oh, axis=-1, keepdims=True)
            parts.append(oh * lax.rsqrt(ms + EPS) * onorm[:, hh * A_DV:(hh + 1) * A_DV])
        on = jnp.concatenate(parts, axis=1)
        yg_ref[0, rows, :] = (on * (r * _sigmoid(r))).astype(BF16)
        xm1 = xs_ref[pad + s0 - 1:pad + s0 - 1 + TM, :]
        x0 = xs_ref[pad + s0:pad + s0 + TM, :]
        xp1 = xs_ref[pad + s0 + 1:pad + s0 + 1 + TM, :]
        if s0 + TM == CTX_LEN:
            xp1 = jnp.where(trow == TM - 1, 0.0, xp1)
        if s0 == CTX_LEN:
            xm1 = jnp.where(trow == 0, 0.0, xm1)
        conv = cw[0:1] * xm1 + cw[1:2] * x0 + cw[2:3] * xp1 + cw[3:4]
        yc_ref[0, rows, :] = (gb_ref[0, rows, :].astype(F32) * conv).astype(BF16)


def _gla_conv(p, lr, gw_pad, gbias, onorm, cw):
    hw = 2 * A_DV

    def pspec(width, base):
        return pl.BlockSpec((1, L_ALL, width), lambda b, i: (b, 0, base + i))

    return pl.pallas_call(
        _gla_conv_kernel,
        grid=(BATCH, A_HEADS // 2),
        in_specs=[
            pspec(LANES, 0),
            pspec(LANES, 2),
            pspec(hw, 2),
            pspec(hw, 4),
            pspec(hw, 6),
            pspec(hw, 8),
            pspec(hw, 10),
            pl.BlockSpec((1, L_ALL, LANES), lambda b, i: (b, 0, 0)),
            pl.BlockSpec((2, LANES, LANES), lambda b, i: (0, 0, i)),
            pl.BlockSpec((2, 1, LANES), lambda b, i: (0, 0, i)),
            pl.BlockSpec((1, hw), lambda b, i: (0, i)),
            pl.BlockSpec((SUBLANES, hw), lambda b, i: (0, i)),
        ],
        out_specs=[
            pl.BlockSpec((1, L_ALL, hw), lambda b, i: (b, 0, i)),
            pl.BlockSpec((1, L_ALL, hw), lambda b, i: (b, 0, i)),
        ],
        out_shape=[
            jax.ShapeDtypeStruct((BATCH, L_ALL, A_HEADS * A_DV), BF16),
            jax.ShapeDtypeStruct((BATCH, L_ALL, B_WIDTH), BF16),
        ],
        scratch_shapes=[
            pltpu.VMEM((L_ALL, hw), F32),
            pltpu.VMEM((L_ALL, hw), F32),
            pltpu.VMEM((2, A_DV, LANES), F32),
            pltpu.VMEM((2, A_DV, LANES), F32),
            pltpu.VMEM((L_ALL + 2 * SUBLANES, hw), F32),
        ],
        compiler_params=_params(("arbitrary", "arbitrary"), VMEM_LIMIT),
        name="gla_conv",
    )(p, p, p, p, p, p, p, lr, gw_pad, gbias, onorm, cw)


def _outproj_kernel(n_in, n_res, *refs):
    y_refs = refs[:n_in]
    w_refs = refs[n_in:2 * n_in]
    res_refs = refs[2 * n_in:2 * n_in + n_res]
    mod_ref, g_ref, rwt_ref, rb_ref, xo_ref, h_ref, te_ref, tg_ref, cnt_ref = refs[2 * n_in + n_res:]
    is_ctx = pl.program_id(1) == 0
    mod = mod_ref[0, 0]
    wt = rwt_ref[...]
    wb = wt.astype(BF16)
    wl = (wt - wb.astype(F32)).astype(BF16)
    eidx = lax.broadcasted_iota(I32, (N_EXPERTS, LANES), 0)
    cnt = jnp.zeros((N_EXPERTS, 1), I32)
    groups = [slice(r0, r0 + LANES) for r0 in range(0, TM, LANES)]
    accs = []
    for rows in groups:
        acc = _dot(y_refs[0][0, rows, :], w_refs[0][...])
        for i in range(1, n_in):
            acc = acc + _dot(y_refs[i][0, rows, :], w_refs[i][...])
        accs.append(acc)
    hs = []
    for rows, acc in zip(groups, accs):
        xres = res_refs[0][0, rows, :]
        if n_res == 2:
            xres = jnp.where(is_ctx, xres, res_refs[1][0, rows, :])
        xn = xres + mod[2:3] * acc
        xo_ref[0, rows, :] = xn
        h = _norm_mod(xn, g_ref[...], mod[3:4], mod[4:5])
        h_ref[rows, :] = h.astype(BF16)
        hs.append(h)
    all_logits = []
    for h in hs:
        hb = h.astype(BF16)
        hl = (h - hb.astype(F32)).astype(BF16)
        all_logits.append(_dot_nt(wb, hb) + _dot_nt(wb, hl) + _dot_nt(wl, hb) + rb_ref[...])
    for rows, logits in zip(groups, all_logits):
        cur = logits
        vals, idxs = [], []
        for _ in range(TOP_K):
            m = jnp.max(cur, axis=0, keepdims=True)
            sel = jnp.min(jnp.where(cur == m, eidx, N_EXPERTS), axis=0, keepdims=True)
            vals.append(m)
            idxs.append(sel)
            cur = jnp.where(eidx == sel, -jnp.inf, cur)
        ex = [jnp.exp(v - vals[0]) for v in vals]
        den = ex[0] + ex[1] + ex[2] + ex[3]
        zi = jnp.zeros_like(idxs[0])
        zf = jnp.zeros_like(den)
        te_ref[:, rows] = jnp.concatenate(idxs + [zi] * (SUBLANES - TOP_K), axis=0)
        tg_ref[:, rows] = jnp.concatenate([e / den for e in ex] + [zf] * (SUBLANES - TOP_K), axis=0)
        onehot = jnp.where(eidx == idxs[0], 1, 0)
        for k in range(1, TOP_K):
            onehot = onehot + jnp.where(eidx == idxs[k], 1, 0)
        cnt = cnt + jnp.sum(onehot, axis=1, keepdims=True)
    cnt_ref[0] = jnp.broadcast_to(cnt, (N_EXPERTS, LANES))


def _outproj(ys, ws, res, res_specs, n_tiles, mod, mod_seg, g, rwt, rb):
    n_in = len(ys)
    n_tok = BATCH * n_tiles * TM
    in_specs = []
    for y in ys:
        in_specs.append(pl.BlockSpec((1, TM, y.shape[2]), lambda b, j: (b, j, 0)))
    for w in ws:
        in_specs.append(pl.BlockSpec(w.shape, lambda b, j: (0, 0)))
    in_specs += list(res_specs) + [
        pl.BlockSpec((1, 1, SUBLANES, D_MODEL), lambda b, j: (b, mod_seg(j), 0, 0)),
        pl.BlockSpec((1, D_MODEL), lambda b, j: (0, 0)),
        pl.BlockSpec((N_EXPERTS, D_MODEL), lambda b, j: (0, 0)),
        pl.BlockSpec((N_EXPERTS, 1), lambda b, j: (0, 0)),
    ]
    return pl.pallas_call(
        functools.partial(_outproj_kernel, n_in, len(res)),
        grid=(BATCH, n_tiles),
        in_specs=in_specs,
        out_specs=[
            pl.BlockSpec((1, TM, D_MODEL), lambda b, j: (b, j, 0)),
            pl.BlockSpec((TM, D_MODEL), lambda b, j: (b * n_tiles + j, 0)),
            pl.BlockSpec((SUBLANES, TM), lambda b, j: (0, b * n_tiles + j)),
            pl.BlockSpec((SUBLANES, TM), lambda b, j: (0, b * n_tiles + j)),
            pl.BlockSpec((1, N_EXPERTS, LANES), lambda b, j: (b * n_tiles + j, 0, 0)),
        ],
        out_shape=[
            jax.ShapeDtypeStruct((BATCH, n_tiles * TM, D_MODEL), F32),
            jax.ShapeDtypeStruct((n_tok, D_MODEL), BF16),
            jax.ShapeDtypeStruct((SUBLANES, n_tok), I32),
            jax.ShapeDtypeStruct((SUBLANES, n_tok), F32),
            jax.ShapeDtypeStruct((BATCH * n_tiles, N_EXPERTS, LANES), I32),
        ],
        compiler_params=_params(("arbitrary", "arbitrary"), VMEM_LIMIT),
        name="outproj_router",
    )(*ys, *ws, *res, mod, g, rwt, rb)


def _moe_rows(n_tok):
    n_assign = n_tok * TOP_K
    n_blocks = -(-(n_assign + N_EXPERTS * (MOE_BM - 1)) // MOE_BM)
    return n_assign, n_blocks


def _route_tables(cnt, n_tok):
    _, n_blocks = _moe_rows(n_tok)
    counts = jnp.sum(cnt, axis=0)
    padded = (counts + MOE_BM - 1) // MOE_BM * MOE_BM
    pad_end = jnp.cumsum(padded)
    block_start = pad_end - padded
    seg_start = block_start[None, :] + jnp.cumsum(cnt, axis=0) - cnt
    blk_row = jnp.arange(n_blocks, dtype=I32) * MOE_BM
    block_e = jnp.minimum(jnp.sum((pad_end[None, :] <= blk_row[:, None]).astype(I32), axis=1), N_EXPERTS - 1)
    n_used = (pad_end[-1] // MOE_BM).reshape(1)
    eids = jnp.arange(N_EXPERTS, dtype=I32)
    later = jnp.where(jnp.logical_and(eids[None, :] > eids[:, None], counts[None, :] > 0), eids[None, :], N_EXPERTS)
    nxt = jnp.min(later, axis=1)
    next_e = jnp.where(nxt == N_EXPERTS, -1, nxt)
    return dict(block_e=block_e.astype(I32), n_used=n_used.astype(I32),
                row_end=(block_start + counts).astype(I32), next_e=next_e.astype(I32),
                seg_start=seg_start.reshape(-1).astype(I32), cnt=cnt.reshape(-1).astype(I32),
                pad_start=(block_start + counts).astype(I32), pad_len=(padded - counts).astype(I32))


def _store_row_slabs(ref, val, n_rows):
    for s in range(ROW_SLABS):
        ref[pl.ds(s, n_rows, stride=ROW_SLABS), :] = val[:, s * LANES:(s + 1) * LANES]


def _load_row_slabs(ref, n_rows):
    return jnp.concatenate(
        [ref[pl.ds(s, n_rows, stride=ROW_SLABS), :] for s in range(ROW_SLABS)], axis=1)


def _slab_rows(start, n):
    return pl.ds(pl.multiple_of(start * ROW_SLABS, ROW_SLABS), n * ROW_SLABS)


def _local_positions(te):
    eidx = lax.broadcasted_iota(I32, (N_EXPERTS, TM), 0)
    hits = [te[k:k + 1] == eidx for k in range(TOP_K)]
    onehot = jnp.where(hits[0], 1.0, 0.0)
    for k in range(1, TOP_K):
        onehot = onehot + jnp.where(hits[k], 1.0, 0.0)
    mb = onehot.astype(BF16)
    trow = lax.broadcasted_iota(I32, (TM, TM), 0)
    tcol = lax.broadcasted_iota(I32, (TM, TM), 1)
    before = _dot(mb, (trow < tcol).astype(BF16))
    totals = _dot(mb, jnp.ones((TM, TM), BF16))
    erow = lax.broadcasted_iota(I32, (N_EXPERTS, N_EXPERTS), 0)
    ecol = lax.broadcasted_iota(I32, (N_EXPERTS, N_EXPERTS), 1)
    first = _dot((ecol < erow).astype(BF16), totals.astype(BF16))
    base = first + before
    return [jnp.sum(jnp.where(hits[k], base, 0.0), axis=0, keepdims=True).astype(I32) for k in range(TOP_K)]


def _dispatch_kernel(n_blocks, ss_ref, cn_ref, ps_ref, pl_ref, nu_ref, h_ref, te_ref, xs_hbm, lpos_ref,
                     sbuf, zbuf, sem):
    j = pl.program_id(0)
    n_tiles = pl.num_programs(0)

    def zero_fill(act):
        def per_expert(e, carry):
            n = pl_ref[e]

            @pl.when(n > 0)
            def _():
                act(pltpu.make_async_copy(zbuf.at[_slab_rows(0, n)], xs_hbm.at[_slab_rows(ps_ref[e], n)],
                                          sem.at[MOE_RING]))
            return carry
        lax.fori_loop(0, N_EXPERTS, per_expert, 0)

        def per_block(b, carry):
            act(pltpu.make_async_copy(zbuf, xs_hbm.at[_slab_rows(b * MOE_BM, MOE_BM)], sem.at[MOE_RING]))
            return carry
        lax.fori_loop(nu_ref[0], n_blocks, per_block, 0)

    @pl.when(j == 0)
    def _():
        zbuf[...] = jnp.zeros_like(zbuf)
        zero_fill(lambda cp: cp.start())

    lpos = _local_positions(te_ref[...])
    zi = jnp.zeros_like(lpos[0])
    lpos_ref[...] = jnp.concatenate(lpos + [zi] * (SUBLANES - TOP_K), axis=0)
    riota = lax.broadcasted_iota(I32, (TILE_ROWS, TM), 0)
    hit = riota == lpos[0]
    for k in range(1, TOP_K):
        hit = jnp.logical_or(hit, riota == lpos[k])
    perm = jnp.where(hit, 1.0, 0.0).astype(BF16)
    hb = h_ref[...].astype(BF16)
    slot = lax.rem(j, MOE_RING)
    buf = sbuf.at[slot]

    def rows_copy(sl):
        return pltpu.make_async_copy(sbuf.at[sl], xs_hbm.at[_slab_rows(0, TILE_ROWS)], sem.at[sl])

    @pl.when(j >= MOE_RING)
    def _():
        rows_copy(slot).wait()

    for s in range(0, ROW_SLABS, 2):
        xl = _dot(perm, hb[:, s * LANES:(s + 2) * LANES])
        buf[pl.ds(s, TILE_ROWS, stride=ROW_SLABS), :] = xl[:, :LANES]
        buf[pl.ds(s + 1, TILE_ROWS, stride=ROW_SLABS), :] = xl[:, LANES:]

    local = 0
    for e in range(N_EXPERTS):
        n = cn_ref[j * N_EXPERTS + e]

        @pl.when(n > 0)
        def _(e=e, n=n, local=local):
            pltpu.make_async_copy(buf.at[_slab_rows(local, n)],
                                  xs_hbm.at[_slab_rows(ss_ref[j * N_EXPERTS + e], n)],
                                  sem.at[slot]).start(priority=e % 2)
        local = local + n

    @pl.when(j == n_tiles - 1)
    def _():
        for back in range(MOE_RING):
            rows_copy(lax.rem(j + MOE_RING - back, MOE_RING)).wait()
        zero_fill(lambda cp: cp.wait())


def _dispatch(h, top_e, tabs, n_tok):
    _, n_blocks = _moe_rows(n_tok)
    n_tiles = n_tok // TM
    grid_spec = pltpu.PrefetchScalarGridSpec(
        num_scalar_prefetch=5,
        grid=(n_tiles,),
        in_specs=[
            pl.BlockSpec((TM, D_MODEL), lambda j, *_: (j, 0)),
            pl.BlockSpec((SUBLANES, TM), lambda j, *_: (0, j)),
        ],
        out_specs=[
            pl.BlockSpec(memory_space=pl.ANY),
            pl.BlockSpec((SUBLANES, TM), lambda j, *_: (0, j)),
        ],
        scratch_shapes=[
            pltpu.VMEM((MOE_RING, TILE_ROWS * ROW_SLABS, LANES), F32),
            pltpu.VMEM((MOE_BM * ROW_SLABS, LANES), F32),
            pltpu.SemaphoreType.DMA((MOE_RING + 1,)),
        ],
    )
    return pl.pallas_call(
        functools.partial(_dispatch_kernel, n_blocks),
        grid_spec=grid_spec,
        out_shape=[
            jax.ShapeDtypeStruct((n_blocks * MOE_BM * ROW_SLABS, LANES), F32),
            jax.ShapeDtypeStruct((SUBLANES, n_tok), I32),
        ],
        compiler_params=_params(("arbitrary",), VMEM_LIMIT),
        name="moe_dispatch",
    )(tabs["seg_start"], tabs["cnt"], tabs["pad_start"], tabs["pad_len"], tabs["n_used"], h, top_e)


def _moe_kernel(layer, be_ref, nu_ref, end_ref, ne_ref, x_ref, b1_ref, b2_ref, w1_hbm, w2_hbm, y_ref,
                w1s, w2s, w1b, w2b, sem):
    i = pl.program_id(0)
    nu = nu_ref[0]

    def weight_copies(e):
        return (pltpu.make_async_copy(w1_hbm.at[layer, e], w1s, sem.at[0]),
                pltpu.make_async_copy(w2_hbm.at[layer, e], w2s, sem.at[1]))

    @pl.when(i < nu)
    def _():
        e = be_ref[i]

        @pl.when(i == 0)
        def _():
            for cp in weight_copies(e):
                cp.start()

        @pl.when(jnp.logical_or(i == 0, e != be_ref[jnp.maximum(i - 1, 0)]))
        def _():
            for cp in weight_copies(e):
                cp.wait()
            w1b[...] = w1s[...].astype(BF16)
            w2b[...] = w2s[...].astype(BF16)
            nxt = ne_ref[e]

            @pl.when(nxt >= 0)
            def _():
                for cp in weight_copies(nxt):
                    cp.start()

        def run(n_rows):
            slab_rows = pl.ds(0, n_rows * ROW_SLABS)
            x = _load_row_slabs(x_ref.at[slab_rows], n_rows).astype(BF16)
            h1 = _dot(x, w1b[...]) + b1_ref[0, 0]
            gate = jnp.minimum(h1[:, :D_MODEL], SWIGLU_LIMIT)
            up = jnp.clip(h1[:, D_MODEL:], -SWIGLU_LIMIT, SWIGLU_LIMIT)
            act = (up + 1.0) * gate * _sigmoid(SWIGLU_ALPHA * gate)
            y = _dot(act.astype(BF16), w2b[...]) + b2_ref[0, 0]
            _store_row_slabs(y_ref.at[slab_rows], y, n_rows)

        n_valid = end_ref[e] - i * MOE_BM

        @pl.when(n_valid > MOE_HALF)
        def _():
            run(MOE_BM)

        @pl.when(n_valid <= MOE_HALF)
        def _():
            run(MOE_HALF)
            y_ref[pl.ds(MOE_HALF * ROW_SLABS, MOE_HALF * ROW_SLABS), :] = jnp.zeros(
                (MOE_HALF * ROW_SLABS, LANES), F32)

    @pl.when(i >= nu)
    def _():
        y_ref[...] = jnp.zeros_like(y_ref)


def _moe(xs, tabs, layer, w1, b1, w2, b2, n_tok):
    _, n_blocks = _moe_rows(n_tok)
    d2 = 2 * D_MODEL
    blk = MOE_BM * ROW_SLABS
    grid_spec = pltpu.PrefetchScalarGridSpec(
        num_scalar_prefetch=4,
        grid=(n_blocks,),
        in_specs=[
            pl.BlockSpec((blk, LANES), lambda i, be, nu, *_: (jnp.minimum(i, nu[0] - 1), 0)),
            pl.BlockSpec((1, 1, 1, d2), lambda i, be, *_: (layer, be[i], 0, 0)),
            pl.BlockSpec((1, 1, 1, D_MODEL), lambda i, be, *_: (layer, be[i], 0, 0)),
            pl.BlockSpec(memory_space=pl.ANY),
            pl.BlockSpec(memory_space=pl.ANY),
        ],
        out_specs=pl.BlockSpec((blk, LANES), lambda i, *_: (i, 0)),
        scratch_shapes=[
            pltpu.VMEM((D_MODEL, d2), F32),
            pltpu.VMEM((D_MODEL, D_MODEL), F32),
            pltpu.VMEM((D_MODEL, d2), BF16),
            pltpu.VMEM((D_MODEL, D_MODEL), BF16),
            pltpu.SemaphoreType.DMA((2,)),
        ],
    )
    return pl.pallas_call(
        functools.partial(_moe_kernel, layer),
        grid_spec=grid_spec,
        out_shape=jax.ShapeDtypeStruct((n_blocks * blk, LANES), F32),
        compiler_params=_params(("arbitrary",), VMEM_LIMIT),
        name="moe_experts",
    )(tabs["block_e"], tabs["n_used"], tabs["row_end"], tabs["next_e"], xs,
      b1.reshape(DEPTH, N_EXPERTS, 1, d2), b2.reshape(DEPTH, N_EXPERTS, 1, D_MODEL), w1, w2)


def _combine_kernel(n_next, n_tiles, ss_ref, cn_ref, lpos_ref, tg_ref, x_ref, mod_ref, g_ref, ys_hbm, *rest):
    final = n_next == 0
    next_in, (o_ref, *next_out), (cbuf, sem) = rest[:n_next], rest[n_next:len(rest) - 2], rest[len(rest) - 2:]
    t = pl.program_id(0) * n_tiles + pl.program_id(1)
    n_total = pl.num_programs(0) * n_tiles
    slot = lax.rem(t, MOE_RING)
    ahead = MOE_RING - 1

    def segment_copy(tile, sl, e, local, n):
        return pltpu.make_async_copy(ys_hbm.at[_slab_rows(ss_ref[tile * N_EXPERTS + e], n)],
                                     cbuf.at[sl, _slab_rows(local, n)], sem.at[sl])

    @pl.when(t == 0)
    def _():
        for first in range(ahead):
            def segment(e, local, first=first):
                n = cn_ref[first * N_EXPERTS + e]

                @pl.when(n > 0)
                def _():
                    segment_copy(first, first, e, local, n).start()
                return local + n
            lax.fori_loop(0, N_EXPERTS, segment, 0)

    @pl.when(t + ahead < n_total)
    def _():
        tile = t + ahead
        sl = lax.rem(tile, MOE_RING)
        local = 0
        for e in range(N_EXPERTS):
            n = cn_ref[tile * N_EXPERTS + e]

            @pl.when(n > 0)
            def _(e=e, n=n, local=local):
                segment_copy(tile, sl, e, local, n).start(priority=e % 2)
            local = local + n

    pltpu.make_async_copy(ys_hbm.at[_slab_rows(0, TILE_ROWS)], cbuf.at[slot], sem.at[slot]).wait()

    lpos = lpos_ref[...]
    tg = tg_ref[...]
    riota = lax.broadcasted_iota(I32, (TILE_ROWS, TM), 0)
    hit = riota == lpos[0:1]
    gsel = jnp.where(hit, tg[0:1], 0.0)
    for k in range(1, TOP_K):
        hit_k = riota == lpos[k:k + 1]
        gsel = gsel + jnp.where(hit_k, tg[k:k + 1], 0.0)
        hit = jnp.logical_or(hit, hit_k)
    row_gate = jnp.sum(gsel, axis=1, keepdims=True)
    yg = (_load_row_slabs(cbuf.at[slot], TILE_ROWS) * row_gate).astype(BF16)
    f = _dot_tn(jnp.where(hit, 1.0, 0.0).astype(BF16), yg)

    mod = mod_ref[0, 0]
    xn = x_ref[0] + mod[5:6] * f
    if final:
        ms = jnp.mean(xn * xn, axis=-1, keepdims=True)
        xn = xn * lax.rsqrt(ms + EPS) * g_ref[...]
    o_ref[0] = xn
    if not final:
        _qkv_body(xn, *next_in, *next_out)


def _combine(ys, tabs, lpos, top_g, xres, n_tiles, mod, mod_seg, g, next_qkv):
    final = next_qkv is None
    in_specs = [
        pl.BlockSpec((SUBLANES, TM), lambda b, j, *_: (0, b * n_tiles + j)),
        pl.BlockSpec((SUBLANES, TM), lambda b, j, *_: (0, b * n_tiles + j)),
        pl.BlockSpec((1, TM, D_MODEL), lambda b, j, *_: (b, j, 0)),
        pl.BlockSpec((1, 1, SUBLANES, D_MODEL), lambda b, j, *_: (b, mod_seg(j), 0, 0)),
        pl.BlockSpec((1, D_MODEL), lambda b, j, *_: (0, 0)),
        pl.BlockSpec(memory_space=pl.ANY),
    ]
    out_specs = [pl.BlockSpec((1, TM, D_MODEL), lambda b, j, *_: (b, j, 0))]
    out_shape = [jax.ShapeDtypeStruct((BATCH, n_tiles * TM, D_MODEL), F32)]
    extra = ()
    if not final:
        qkv_in, qkv_out, qkv_shape = _qkv_specs()
        in_specs += qkv_in
        out_specs += qkv_out
        out_shape += qkv_shape
        extra = tuple(next_qkv)
    grid_spec = pltpu.PrefetchScalarGridSpec(
        num_scalar_prefetch=2,
        grid=(BATCH, n_tiles),
        in_specs=in_specs,
        out_specs=out_specs,
        scratch_shapes=[
            pltpu.VMEM((MOE_RING, TILE_ROWS * ROW_SLABS, LANES), F32),
            pltpu.SemaphoreType.DMA((MOE_RING,)),
        ],
    )
    out = pl.pallas_call(
        functools.partial(_combine_kernel, len(extra), n_tiles),
        grid_spec=grid_spec,
        out_shape=out_shape,
        compiler_params=_params(("arbitrary", "arbitrary"), VMEM_LIMIT),
        name="moe_combine",
    )(tabs["seg_start"], tabs["cnt"], lpos, top_g, xres, mod, g, ys, *extra)
    return out[0] if final else out


def _moe_ffn(h, top_e, top_g, cnt, layer, xres, n_tiles, mod, mod_seg, w1, b1, w2, b2, g, next_qkv):
    n_tok = BATCH * n_tiles * TM
    tabs = _route_tables(cnt[:, :, 0], n_tok)
    xs, lpos = _dispatch(h, top_e, tabs, n_tok)
    ys = _moe(xs, tabs, layer, w1, b1, w2, b2, n_tok)
    return _combine(ys, tabs, lpos, top_g, xres, n_tiles, mod, mod_seg, g, next_qkv)


def _qkv_body(x, mod_ref, g_ref, w_ref, qn_ref, kn_ref, cos_ref, sin_ref, q_ref, k_ref, v_ref):
    mod = mod_ref[0, 0]
    h = _norm_mod(x, g_ref[...], mod[0:1], mod[1:2]).astype(BF16)
    qkv = _dot(h, w_ref[...])
    cos = cos_ref[...]
    sin = sin_ref[...]

    def head(xh, gn):
        ms = jnp.mean(xh * xh, axis=-1, keepdims=True)
        y = xh * lax.rsqrt(ms + EPS) * gn
        return y * cos + pltpu.roll(y, C_HEAD_DIM // 2, 1) * sin

    qn = qn_ref[...]
    kn = kn_ref[...]
    for hq in range(C_HEADS):
        sl = slice(hq * C_HEAD_DIM, (hq + 1) * C_HEAD_DIM)
        q_ref[0, :, sl] = head(qkv[:, sl], qn).astype(BF16)
    for hk in range(C_KV_HEADS):
        src = slice((C_HEADS + hk) * C_HEAD_DIM, (C_HEADS + hk + 1) * C_HEAD_DIM)
        k_ref[0, :, hk * C_HEAD_DIM:(hk + 1) * C_HEAD_DIM] = head(qkv[:, src], kn).astype(BF16)
    ones = jnp.ones((TM, C_HEAD_DIM), BF16)
    for hk in range(C_KV_HEADS):
        src = slice((C_HEADS + C_KV_HEADS + hk) * C_HEAD_DIM, (C_HEADS + C_KV_HEADS + hk + 1) * C_HEAD_DIM)
        v_ref[0, :, 2 * hk * C_HEAD_DIM:(2 * hk + 1) * C_HEAD_DIM] = qkv[:, src].astype(BF16)
        v_ref[0, :, (2 * hk + 1) * C_HEAD_DIM:(2 * hk + 2) * C_HEAD_DIM] = ones


def _qkv_specs():
    kvw = C_KV_HEADS * C_HEAD_DIM
    in_specs = [
        pl.BlockSpec((1, 1, SUBLANES, D_MODEL), lambda b, j, *_: (b, jnp.minimum(j, 1), 0, 0)),
        pl.BlockSpec((1, D_MODEL), lambda b, j, *_: (0, 0)),
        pl.BlockSpec((D_MODEL, C_QKV), lambda b, j, *_: (0, 0)),
        pl.BlockSpec((1, C_HEAD_DIM), lambda b, j, *_: (0, 0)),
        pl.BlockSpec((1, C_HEAD_DIM), lambda b, j, *_: (0, 0)),
        pl.BlockSpec((TM, C_HEAD_DIM), lambda b, j, *_: (j, 0)),
        pl.BlockSpec((TM, C_HEAD_DIM), lambda b, j, *_: (j, 0)),
    ]
    out_specs = [
        pl.BlockSpec((1, TM, D_MODEL), lambda b, j, *_: (b, jnp.maximum(j - 1, 0), 0)),
        pl.BlockSpec((1, TM, kvw), lambda b, j, *_: (b, j, 0)),
        pl.BlockSpec((1, TM, 2 * kvw), lambda b, j, *_: (b, j, 0)),
    ]
    out_shape = [
        jax.ShapeDtypeStruct((BATCH, SEQ, D_MODEL), BF16),
        jax.ShapeDtypeStruct((BATCH, L_ALL, kvw), BF16),
        jax.ShapeDtypeStruct((BATCH, L_ALL, 2 * kvw), BF16),
    ]
    return in_specs, out_specs, out_shape


def _rope_permute(t):
    rows, cols = t.shape
    t = t.reshape(rows, cols // C_HEAD_DIM, 2, 2, C_HEAD_DIM // 4)
    return t.transpose(0, 1, 3, 2, 4).reshape(rows, cols)


def _rope_tables():
    rows = SEQ // GRID_W
    row = jnp.repeat(jnp.arange(rows), GRID_W).astype(F32)
    col = jnp.tile(jnp.arange(GRID_W), rows).astype(F32)
    half = C_HEAD_DIM // 2
    inv_freq = ROPE_THETA ** (-jnp.arange(0, half, 2, dtype=F32) / half)
    ar = row[:, None] * inv_freq
    ac = col[:, None] * inv_freq
    cos = jnp.concatenate([jnp.cos(ar), jnp.cos(ac), jnp.cos(ar), jnp.cos(ac)], axis=-1)
    sin = jnp.concatenate([-jnp.sin(ar), -jnp.sin(ac), jnp.sin(ar), jnp.sin(ac)], axis=-1)
    cos = jnp.concatenate([jnp.ones((CTX_LEN, C_HEAD_DIM), F32), cos], axis=0)
    sin = jnp.concatenate([jnp.zeros((CTX_LEN, C_HEAD_DIM), F32), sin], axis=0)
    return cos, sin


def _attn_kernel(q_ref, k_ref, v_ref, o_ref):
    c = (C_HEAD_DIM ** -0.5) * LOG2_E
    def head_cols(h):
        return slice(h * C_HEAD_DIM, (h + 1) * C_HEAD_DIM)

    def scores(h):
        return _dot_nt(q_ref[0, :, head_cols(h)], k_ref[0, :, head_cols(h // C_GROUP)])

    s_next = scores(0)
    for h in range(C_HEADS):
        s = s_next
        if h + 1 < C_HEADS:
            s_next = scores(h + 1)
        g = h // C_GROUP
        m = jnp.max(s, axis=-1, keepdims=True)
        p = jnp.exp2((s - m) * c).astype(BF16)
        oe = _dot(p, v_ref[0, :, 2 * g * C_HEAD_DIM:(2 * g + 2) * C_HEAD_DIM])
        o_ref[0, :, head_cols(h)] = (oe[:, :C_HEAD_DIM] / oe[:, C_HEAD_DIM:]).astype(BF16)


def _attention(q, k, v):
    kvw = C_KV_HEADS * C_HEAD_DIM
    return pl.pallas_call(
        _attn_kernel,
        grid=(BATCH, SEQ // ATT_TQ),
        in_specs=[
            pl.BlockSpec((1, ATT_TQ, D_MODEL), lambda b, j: (b, j, 0)),
            pl.BlockSpec((1, L_ALL, kvw), lambda b, j: (b, 0, 0)),
            pl.BlockSpec((1, L_ALL, 2 * kvw), lambda b, j: (b, 0, 0)),
        ],
        out_specs=pl.BlockSpec((1, ATT_TQ, D_MODEL), lambda b, j: (b, j, 0)),
        out_shape=jax.ShapeDtypeStruct((BATCH, SEQ, D_MODEL), BF16),
        compiler_params=_params(("arbitrary", "arbitrary"), VMEM_LIMIT),
        name="attention",
    )(q, k, v)


def _mod_table(ada_layer):
    m = ada_layer.reshape(ADA_ROWS, N_MOD, D_MODEL)
    m_lat = m[:BATCH]
    m_ctx = jnp.broadcast_to(m[BATCH], (BATCH, N_MOD, D_MODEL))
    t = jnp.stack([m_ctx, m_lat], axis=1)
    return jnp.pad(t, ((0, 0), (0, 0), (0, SUBLANES - N_MOD), (0, 0)))


def kernel(x, c, ctx, c_ctx, ada_w, ada_b, norm_mix, norm_ffn, ab_w_in, ab_gate_w, ab_gate_b, ab_out_norm,
           ab_conv_w, ab_conv_b, ab_w_out, attn_w_qkv, attn_q_norm, attn_k_norm, attn_w_o, router_w,
           router_b, moe_w1, moe_b1, moe_w2, moe_b2, final_norm):
    assert x.shape == (BATCH, SEQ, D_MODEL) and ctx.shape == (BATCH, CTX_LEN, D_MODEL), (x.shape, ctx.shape)
    assert moe_w1.shape == (DEPTH, N_EXPERTS, D_MODEL, 2 * D_MODEL), moe_w1.shape
    cond = jnp.zeros((ADA_ROWS, D_MODEL), F32).at[:BATCH].set(c).at[BATCH].set(c_ctx)
    ada = _ada(cond, ada_w, ada_b)
    mod0 = _mod_table(ada[0])
    mod1 = _mod_table(ada[1])
    seg_all = lambda j: jnp.minimum(j, 1)
    seg_lat = lambda j: 1
    n_tiles_all = L_ALL // TM
    n_tiles_lat = SEQ // TM
    fnorm = final_norm.reshape(1, D_MODEL)

    w_in = ab_w_in[0]
    lr0 = 2 * A_HEADS * A_DK + A_HEADS * A_DV
    lr1 = lr0 + 2 * A_GATE_RANK
    w_main = jnp.concatenate([w_in[:, :lr0], w_in[:, lr1:]], axis=1).astype(BF16)
    w_lr = jnp.pad(w_in[:, lr0:lr1], ((0, 0), (0, LANES - 2 * A_GATE_RANK))).astype(BF16)
    p, lr = _proj_in(ctx, x, mod0, norm_mix[0:1], w_main, w_lr)

    gw = ab_gate_w[0]
    gw_pad = jnp.zeros((2, LANES, A_HEADS * A_DK), F32)
    gw_pad = gw_pad.at[0, :A_GATE_RANK].set(gw[0]).at[1, A_GATE_RANK:2 * A_GATE_RANK].set(gw[1])
    cw = jnp.concatenate([ab_conv_w[0], ab_conv_b[0][None], jnp.zeros((SUBLANES - 4, B_WIDTH), F32)], axis=0)
    yg, yc = _gla_conv(p, lr, gw_pad.astype(BF16), ab_gate_b[0].reshape(2, 1, A_HEADS * A_DK),
                       ab_out_norm[0:1], cw)

    w_out = ab_w_out[0].astype(BF16)
    ngla = A_HEADS * A_DV
    xmid, h, top_e, top_g, cnt = _outproj(
        [yg, yc], [w_out[:ngla], w_out[ngla:]], [ctx, x], _ctx_latent_specs(), n_tiles_all, mod0, seg_all,
        norm_ffn[0:1], router_w[0].T, router_b[0].reshape(N_EXPERTS, 1))
    cos, sin = _rope_tables()
    qk_cols = (C_HEADS + C_KV_HEADS) * C_HEAD_DIM
    w_qkv = jnp.concatenate([_rope_permute(attn_w_qkv[0][:, :qk_cols]), attn_w_qkv[0][:, qk_cols:]],
                            axis=1).astype(BF16)
    next_qkv = (mod1, norm_mix[1:2], w_qkv, _rope_permute(attn_q_norm[0:1]), _rope_permute(attn_k_norm[0:1]),
                cos, sin)
    x1, q, k, v = _moe_ffn(h, top_e, top_g, cnt, 0, xmid, n_tiles_all, mod0, seg_all,
                           moe_w1, moe_b1, moe_w2, moe_b2, fnorm, next_qkv)

    o = _attention(q, k, v)
    xmid, h, top_e, top_g, cnt = _outproj(
        [o], [attn_w_o[0].astype(BF16)], [x1],
        [pl.BlockSpec((1, TM, D_MODEL), lambda b, j: (b, j + CTX_LEN // TM, 0))], n_tiles_lat, mod1, seg_lat,
        norm_ffn[1:2], router_w[1].T, router_b[1].reshape(N_EXPERTS, 1))
    return _moe_ffn(h, top_e, top_g, cnt, 1, xmid, n_tiles_lat, mod1, seg_lat,
                    moe_w1, moe_b1, moe_w2, moe_b2, fnorm, None)
```

```python
import functools

import jax
import jax.numpy as jnp
from jax import lax
from jax.experimental import pallas as pl
from jax.experimental.pallas import tpu as pltpu

F32 = jnp.float32
BF16 = jnp.bfloat16
I32 = jnp.int32

D_MODEL = 1024
BATCH = 8
SEQ = 2048
DEPTH = 2
GRID_W = 64
CTX_LEN = 256
L_ALL = CTX_LEN + SEQ
N_MOD = 6
EPS = 1e-6

A_DV = 128
A_HEADS = 4
A_DK = 64
A_GATE_RANK = 16
A_GATE_TAU = 16.0
GLA_CHUNK = 64
B_WIDTH = 512
AB_MAIN = 3072

C_HEAD_DIM = 128
C_HEADS = 8
C_KV_HEADS = 2
C_GROUP = 4
C_QKV = (C_HEADS + 2 * C_KV_HEADS) * C_HEAD_DIM
ROPE_THETA = 10000.0
LOG2_E = 1.4426950408889634

N_EXPERTS = 32
TOP_K = 4
SWIGLU_LIMIT = 7.0
SWIGLU_ALPHA = 1.702

LANES = 128
SUBLANES = 8
ROW_SLABS = D_MODEL // LANES
TM = 256
ATT_TQ = 512
TILE_ROWS = TM * TOP_K
MOE_BM = 512
MOE_PART = 128
MOE_RING = 3
ADA_ROWS = 16
ADA_TN = 1536
VMEM_LIMIT = 56 * 1024 * 1024

assert CTX_LEN == TM and SEQ % TM == 0 and SEQ % ATT_TQ == 0
assert TM % GLA_CHUNK == 0 and BATCH + 1 <= ADA_ROWS and (N_MOD * D_MODEL) % ADA_TN == 0
assert MOE_RING <= SEQ // TM and MOE_BM % MOE_PART == 0

NT_DIMS = (((1,), (1,)), ((), ()))
TN_DIMS = (((0,), (0,)), ((), ()))


def _dot(a, b):
    return jnp.dot(a, b, preferred_element_type=F32)


def _dot_nt(a, b):
    return lax.dot_general(a, b, NT_DIMS, preferred_element_type=F32)


def _dot_tn(a, b):
    return lax.dot_general(a, b, TN_DIMS, preferred_element_type=F32)


def _params(sem, vmem=None):
    return pltpu.CompilerParams(dimension_semantics=sem, vmem_limit_bytes=vmem)


def _norm_mod(x, g, shift, scale):
    ms = jnp.mean(x * x, axis=-1, keepdims=True)
    y = x * lax.rsqrt(ms + EPS) * g
    return y * (1.0 + scale) + shift


def _sigmoid(x):
    return 1.0 / (1.0 + jnp.exp(-x))


def _ada_kernel(cond_ref, w_ref, b_ref, o_ref):
    c = cond_ref[...]
    s = (c * _sigmoid(c)).astype(BF16)
    o_ref[0] = _dot(s, w_ref[0].astype(BF16)) + b_ref[0]


def _ada(cond, ada_w, ada_b):
    tn = ADA_TN
    n = N_MOD * D_MODEL
    return pl.pallas_call(
        _ada_kernel,
        grid=(DEPTH, n // tn),
        in_specs=[
            pl.BlockSpec((ADA_ROWS, D_MODEL), lambda l, j: (0, 0)),
            pl.BlockSpec((1, D_MODEL, tn), lambda l, j: (l, 0, j)),
            pl.BlockSpec((1, 1, tn), lambda l, j: (l, 0, j)),
        ],
        out_specs=pl.BlockSpec((1, ADA_ROWS, tn), lambda l, j: (l, 0, j)),
        out_shape=jax.ShapeDtypeStruct((DEPTH, ADA_ROWS, n), F32),
        compiler_params=_params(("arbitrary", "arbitrary"), VMEM_LIMIT),
        name="ada",
    )(cond, ada_w, ada_b.reshape(DEPTH, 1, n))


def _ctx_or_latent(ctx_ref, x_ref):
    return jnp.where(pl.program_id(1) == 0, ctx_ref[0], x_ref[0])


def _ctx_latent_specs():
    return [pl.BlockSpec((1, TM, D_MODEL), lambda b, j: (b, 0, 0)),
            pl.BlockSpec((1, TM, D_MODEL), lambda b, j: (b, jnp.maximum(j - 1, 0), 0))]


def _proj_in_kernel(ctx_ref, x_ref, mod_ref, g_ref, w_ref, wlr_ref, p_ref, lr_ref):
    mod = mod_ref[0, 0]
    h = _norm_mod(_ctx_or_latent(ctx_ref, x_ref), g_ref[...], mod[0:1], mod[1:2]).astype(BF16)
    p_ref[0] = _dot(h, w_ref[...]).astype(BF16)
    lr_ref[0] = _dot(h, wlr_ref[...]).astype(BF16)


def _proj_in(ctx, x, mod, g, w_main, w_lr):
    nj = L_ALL // TM
    return pl.pallas_call(
        _proj_in_kernel,
        grid=(BATCH, nj),
        in_specs=_ctx_latent_specs() + [
            pl.BlockSpec((1, 1, SUBLANES, D_MODEL), lambda b, j: (b, jnp.minimum(j, 1), 0, 0)),
            pl.BlockSpec((1, D_MODEL), lambda b, j: (0, 0)),
            pl.BlockSpec((D_MODEL, AB_MAIN), lambda b, j: (0, 0)),
            pl.BlockSpec((D_MODEL, LANES), lambda b, j: (0, 0)),
        ],
        out_specs=[
            pl.BlockSpec((1, TM, AB_MAIN), lambda b, j: (b, j, 0)),
            pl.BlockSpec((1, TM, LANES), lambda b, j: (b, j, 0)),
        ],
        out_shape=[
            jax.ShapeDtypeStruct((BATCH, L_ALL, AB_MAIN), BF16),
            jax.ShapeDtypeStruct((BATCH, L_ALL, LANES), BF16),
        ],
        compiler_params=_params(("arbitrary", "arbitrary"), VMEM_LIMIT),
        name="proj_in",
    )(ctx, x, mod, g, w_main, w_lr)


def _log_sigmoid(z):
    return jnp.minimum(z, 0.0) - jnp.log1p(jnp.exp(-jnp.abs(z)))


def _gla_conv_kernel(q_ref, k_ref, v_ref, r_ref, gb_ref, gc_ref, u_ref, lr_ref, gw_ref, gbias_ref,
                     onorm_ref, cw_ref, yg_ref, yc_ref, of_ref, ob_ref, sf_ref, sb_ref, xs_ref):
    ch = GLA_CHUNK
    grp = TM
    n_grp = L_ALL // grp
    cpg = grp // ch
    row = lax.broadcasted_iota(I32, (grp, grp), 0)
    col = lax.broadcasted_iota(I32, (grp, grp), 1)
    same_chunk = (row // ch) == (col // ch)
    tri = (jnp.logical_and(same_chunk, row >= col), jnp.logical_and(same_chunk, col >= row))
    tri_bf = (tri[0].astype(BF16), tri[1].astype(BF16))
    lane = lax.broadcasted_iota(I32, (grp, LANES), 1)
    head_mask = (lane < A_DK, lane >= A_DK)
    chunk_of_row = lax.broadcasted_iota(I32, (grp, LANES), 0) // ch

    sf_ref[...] = jnp.zeros_like(sf_ref)
    sb_ref[...] = jnp.zeros_like(sb_ref)

    def chunk_rows(x, idx):
        return jnp.concatenate(
            [jnp.broadcast_to(x[c * ch + idx:c * ch + idx + 1], (ch, x.shape[1])) for c in range(cpg)], axis=0)

    s_refs = (sf_ref, sb_ref)
    o_refs = (of_ref, ob_ref)
    dirs = (0, 1)
    pairs = [(d, hh) for d in dirs for hh in range(2)]

    def by_chunk_lanes(m, x):
        return jnp.concatenate([jnp.where(jnp.logical_and(m, chunk_of_row == c), x, 0.0) for c in range(cpg)],
                               axis=1).astype(BF16)

    def body(i, carry):
        gidx = (i, jnp.where(i == 0, 0, n_grp - i))
        rows = [pl.ds(pl.multiple_of(g * grp, grp), grp) for g in gidx]
        q = [q_ref[0, r, :].astype(F32) * (A_DK ** -0.5) for r in rows]
        k = [k_ref[0, r, :].astype(F32) for r in rows]
        v = [v_ref[0, r, :] for r in rows]
        z = [_dot(lr_ref[0, rows[d], :].astype(BF16), gw_ref[d]) + gbias_ref[d] for d in dirs]
        a = [_log_sigmoid(z[d]) * (1.0 / A_GATE_TAU) for d in dirs]
        a_hi = [a[d].astype(BF16) for d in dirs]
        a_lo = [(a[d] - a_hi[d].astype(F32)).astype(BF16) for d in dirs]
        cum = [_dot(tri_bf[d], a_hi[d]) + _dot(tri_bf[d], a_lo[d]) for d in dirs]
        ref = (chunk_rows(cum[0], ch // 2 - 1), chunk_rows(cum[1], ch // 2))
        last = (chunk_rows(cum[0], ch - 1), chunk_rows(cum[1], 0))
        order = (range(cpg), range(cpg - 1, -1, -1))
        qe = [q[d] * jnp.exp(cum[d]) for d in dirs]
        qt = [q[d] * jnp.exp(cum[d] - ref[d]) for d in dirs]
        kt = [(k[d] * jnp.exp(ref[d] - cum[d])).astype(BF16) for d in dirs]
        kl = [k[d] * jnp.exp(last[d] - cum[d]) for d in dirs]
        dec = [jnp.exp(last[d]) for d in dirs]
        vh = {(d, hh): v[d][:, hh * A_DV:(hh + 1) * A_DV].astype(BF16) for d, hh in pairs}
        sc = {(d, hh): _dot_nt(jnp.where(head_mask[hh], qt[d], 0.0).astype(BF16), kt[d]) for d, hh in pairs}
        kv = {(d, hh): _dot(vh[d, hh].T, by_chunk_lanes(head_mask[hh], kl[d])) for d, hh in pairs}
        o_intra = {(d, hh): _dot(jnp.where(tri[d], sc[d, hh], 0.0).astype(BF16), vh[d, hh]) for d, hh in pairs}
        for d, hh in pairs:
            st = s_refs[d][hh]
            states = [None] * cpg
            for c in order[d]:
                states[c] = st
                st = st * dec[d][c * ch:c * ch + 1] + kv[d, hh][:, c * LANES:(c + 1) * LANES]
            s_refs[d][hh] = st
            qem = jnp.where(head_mask[hh], qe[d], 0.0).astype(BF16)
            o_inter = jnp.concatenate(
                [_dot_nt(qem[c * ch:(c + 1) * ch], states[c].astype(BF16)) for c in range(cpg)], axis=0)
            o_refs[d][rows[d], hh * A_DV:(hh + 1) * A_DV] = o_intra[d, hh] + o_inter
        return carry

    lax.fori_loop(0, n_grp, body, 0)

    pad = SUBLANES
    xs_ref[0:pad, :] = jnp.zeros((pad, xs_ref.shape[1]), F32)
    xs_ref[pad + L_ALL:, :] = jnp.zeros((pad, xs_ref.shape[1]), F32)
    xs_ref[pad:pad + L_ALL, :] = gc_ref[0].astype(F32) * u_ref[0].astype(F32)

    cw = cw_ref[...]
    onorm = onorm_ref[...]
    trow = lax.broadcasted_iota(I32, (TM, 1), 0)
    for ti in range(L_ALL // TM):
        s0 = ti * TM
        rows = slice(s0, s0 + TM)
        o = of_ref[rows, :] + ob_ref[rows, :]
        r = r_ref[0, rows, :].astype(F32)
        parts = []
        for hh in range(2):
            oh = o[:, hh * A_DV:(hh + 1) * A_DV]
            ms = jnp.mean(oh * oh, axis=-1, keepdims=True)
            parts.append(oh * lax.rsqrt(ms + EPS) * onorm[:, hh * A_DV:(hh + 1) * A_DV])
        on = jnp.concatenate(parts, axis=1)
        yg_ref[0, rows, :] = (on * (r * _sigmoid(r))).astype(BF16)
        xm1 = xs_ref[pad + s0 - 1:pad + s0 - 1 + TM, :]
        x0 = xs_ref[pad + s0:pad + s0 + TM, :]
        xp1 = xs_ref[pad + s0 + 1:pad + s0 + 1 + TM, :]
        if s0 + TM == CTX_LEN:
            xp1 = jnp.where(trow == TM - 1, 0.0, xp1)
        if s0 == CTX_LEN:
            xm1 = jnp.where(trow == 0, 0.0, xm1)
        conv = cw[0:1] * xm1 + cw[1:2] * x0 + cw[2:3] * xp1 + cw[3:4]
        yc_ref[0, rows, :] = (gb_ref[0, rows, :].astype(F32) * conv).astype(BF16)


def _gla_conv(p, lr, gw_pad, gbias, onorm, cw):
    hw = 2 * A_DV

    def pspec(width, base):
        return pl.BlockSpec((1, L_ALL, width), lambda b, i: (b, 0, base + i))

    return pl.pallas_call(
        _gla_conv_kernel,
        grid=(BATCH, A_HEADS // 2),
        in_specs=[
            pspec(LANES, 0),
            pspec(LANES, 2),
            pspec(hw, 2),
            pspec(hw, 4),
            pspec(hw, 6),
            pspec(hw, 8),
            pspec(hw, 10),
            pl.BlockSpec((1, L_ALL, LANES), lambda b, i: (b, 0, 0)),
            pl.BlockSpec((2, LANES, LANES), lambda b, i: (0, 0, i)),
            pl.BlockSpec((2, 1, LANES), lambda b, i: (0, 0, i)),
            pl.BlockSpec((1, hw), lambda b, i: (0, i)),
            pl.BlockSpec((SUBLANES, hw), lambda b, i: (0, i)),
        ],
        out_specs=[
            pl.BlockSpec((1, L_ALL, hw), lambda b, i: (b, 0, i)),
            pl.BlockSpec((1, L_ALL, hw), lambda b, i: (b, 0, i)),
        ],
        out_shape=[
            jax.ShapeDtypeStruct((BATCH, L_ALL, A_HEADS * A_DV), BF16),
            jax.ShapeDtypeStruct((BATCH, L_ALL, B_WIDTH), BF16),
        ],
        scratch_shapes=[
            pltpu.VMEM((L_ALL, hw), F32),
            pltpu.VMEM((L_ALL, hw), F32),
            pltpu.VMEM((2, A_DV, LANES), F32),
            pltpu.VMEM((2, A_DV, LANES), F32),
            pltpu.VMEM((L_ALL + 2 * SUBLANES, hw), F32),
        ],
        compiler_params=_params(("arbitrary", "arbitrary"), VMEM_LIMIT),
        name="gla_conv",
    )(p, p, p, p, p, p, p, lr, gw_pad, gbias, onorm, cw)


def _outproj_kernel(n_in, n_res, *refs):
    y_refs = refs[:n_in]
    w_refs = refs[n_in:2 * n_in]
    res_refs = refs[2 * n_in:2 * n_in + n_res]
    mod_ref, g_ref, rwt_ref, rb_ref, xo_ref, h_ref, te_ref, tg_ref, cnt_ref = refs[2 * n_in + n_res:]
    is_ctx = pl.program_id(1) == 0
    mod = mod_ref[0, 0]
    wt = rwt_ref[...]
    wb = wt.astype(BF16)
    wl = (wt - wb.astype(F32)).astype(BF16)
    eidx = lax.broadcasted_iota(I32, (N_EXPERTS, LANES), 0)
    cnt = jnp.zeros((N_EXPERTS, 1), I32)
    groups = [slice(r0, r0 + LANES) for r0 in range(0, TM, LANES)]
    accs = []
    for rows in groups:
        acc = _dot(y_refs[0][0, rows, :], w_refs[0][...])
        for i in range(1, n_in):
            acc = acc + _dot(y_refs[i][0, rows, :], w_refs[i][...])
        accs.append(acc)
    hs = []
    for rows, acc in zip(groups, accs):
        xres = res_refs[0][0, rows, :]
        if n_res == 2:
            xres = jnp.where(is_ctx, xres, res_refs[1][0, rows, :])
        xn = xres + mod[2:3] * acc
        xo_ref[0, rows, :] = xn
        h = _norm_mod(xn, g_ref[...], mod[3:4], mod[4:5])
        h_ref[rows, :] = h.astype(BF16)
        hs.append(h)
    all_logits = []
    for h in hs:
        hb = h.astype(BF16)
        hl = (h - hb.astype(F32)).astype(BF16)
        all_logits.append(_dot_nt(wb, hb) + _dot_nt(wb, hl) + _dot_nt(wl, hb) + rb_ref[...])
    for rows, logits in zip(groups, all_logits):
        cur = logits
        vals, idxs = [], []
        for _ in range(TOP_K):
            m = jnp.max(cur, axis=0, keepdims=True)
            sel = jnp.min(jnp.where(cur == m, eidx, N_EXPERTS), axis=0, keepdims=True)
            vals.append(m)
            idxs.append(sel)
            cur = jnp.where(eidx == sel, -jnp.inf, cur)
        ex = [jnp.exp(v - vals[0]) for v in vals]
        den = ex[0] + ex[1] + ex[2] + ex[3]
        zi = jnp.zeros_like(idxs[0])
        zf = jnp.zeros_like(den)
        te_ref[:, rows] = jnp.concatenate(idxs + [zi] * (SUBLANES - TOP_K), axis=0)
        tg_ref[:, rows] = jnp.concatenate([e / den for e in ex] + [zf] * (SUBLANES - TOP_K), axis=0)
        onehot = jnp.where(eidx == idxs[0], 1, 0)
        for k in range(1, TOP_K):
            onehot = onehot + jnp.where(eidx == idxs[k], 1, 0)
        cnt = cnt + jnp.sum(onehot, axis=1, keepdims=True)
    cnt_ref[0] = jnp.broadcast_to(cnt, (N_EXPERTS, LANES))


def _outproj(ys, ws, res, res_specs, n_tiles, mod, mod_seg, g, rwt, rb):
    n_in = len(ys)
    n_tok = BATCH * n_tiles * TM
    in_specs = []
    for y in ys:
        in_specs.append(pl.BlockSpec((1, TM, y.shape[2]), lambda b, j: (b, j, 0)))
    for w in ws:
        in_specs.append(pl.BlockSpec(w.shape, lambda b, j: (0, 0)))
    in_specs += list(res_specs) + [
        pl.BlockSpec((1, 1, SUBLANES, D_MODEL), lambda b, j: (b, mod_seg(j), 0, 0)),
        pl.BlockSpec((1, D_MODEL), lambda b, j: (0, 0)),
        pl.BlockSpec((N_EXPERTS, D_MODEL), lambda b, j: (0, 0)),
        pl.BlockSpec((N_EXPERTS, 1), lambda b, j: (0, 0)),
    ]
    return pl.pallas_call(
        functools.partial(_outproj_kernel, n_in, len(res)),
        grid=(BATCH, n_tiles),
        in_specs=in_specs,
        out_specs=[
            pl.BlockSpec((1, TM, D_MODEL), lambda b, j: (b, j, 0)),
            pl.BlockSpec((TM, D_MODEL), lambda b, j: (b * n_tiles + j, 0)),
            pl.BlockSpec((SUBLANES, TM), lambda b, j: (0, b * n_tiles + j)),
            pl.BlockSpec((SUBLANES, TM), lambda b, j: (0, b * n_tiles + j)),
            pl.BlockSpec((1, N_EXPERTS, LANES), lambda b, j: (b * n_tiles + j, 0, 0)),
        ],
        out_shape=[
            jax.ShapeDtypeStruct((BATCH, n_tiles * TM, D_MODEL), F32),
            jax.ShapeDtypeStruct((n_tok, D_MODEL), BF16),
            jax.ShapeDtypeStruct((SUBLANES, n_tok), I32),
            jax.ShapeDtypeStruct((SUBLANES, n_tok), F32),
            jax.ShapeDtypeStruct((BATCH * n_tiles, N_EXPERTS, LANES), I32),
        ],
        compiler_params=_params(("arbitrary", "arbitrary"), VMEM_LIMIT),
        name="outproj_router",
    )(*ys, *ws, *res, mod, g, rwt, rb)


def _moe_rows(n_tok):
    n_assign = n_tok * TOP_K
    n_blocks = -(-(n_assign + N_EXPERTS * (MOE_BM - 1)) // MOE_BM)
    return n_assign, n_blocks


def _route_tables(cnt, n_tok):
    _, n_blocks = _moe_rows(n_tok)
    counts = jnp.sum(cnt, axis=0)
    padded = (counts + MOE_BM - 1) // MOE_BM * MOE_BM
    pad_end = jnp.cumsum(padded)
    block_start = pad_end - padded
    seg_start = block_start[None, :] + jnp.cumsum(cnt, axis=0) - cnt
    blk_row = jnp.arange(n_blocks, dtype=I32) * MOE_BM
    block_e = jnp.minimum(jnp.sum((pad_end[None, :] <= blk_row[:, None]).astype(I32), axis=1), N_EXPERTS - 1)
    n_used = (pad_end[-1] // MOE_BM).reshape(1)
    eids = jnp.arange(N_EXPERTS, dtype=I32)
    later = jnp.where(jnp.logical_and(eids[None, :] > eids[:, None], counts[None, :] > 0), eids[None, :], N_EXPERTS)
    nxt = jnp.min(later, axis=1)
    next_e = jnp.where(nxt == N_EXPERTS, -1, nxt)
    return dict(block_e=block_e.astype(I32), n_used=n_used.astype(I32),
                row_end=(block_start + counts).astype(I32), next_e=next_e.astype(I32),
                seg_start=seg_start.reshape(-1).astype(I32), cnt=cnt.reshape(-1).astype(I32),
                pad_start=(block_start + counts).astype(I32), pad_len=(padded - counts).astype(I32))


def _store_row_slabs(ref, val, n_rows):
    for s in range(ROW_SLABS):
        ref[pl.ds(s, n_rows, stride=ROW_SLABS), :] = val[:, s * LANES:(s + 1) * LANES]


def _load_row_slabs(ref, n_rows):
    return jnp.concatenate(
        [ref[pl.ds(s, n_rows, stride=ROW_SLABS), :] for s in range(ROW_SLABS)], axis=1)


def _slab_rows(start, n):
    return pl.ds(pl.multiple_of(start * ROW_SLABS, ROW_SLABS), n * ROW_SLABS)


def _local_positions(te):
    eidx = lax.broadcasted_iota(I32, (N_EXPERTS, TM), 0)
    hits = [te[k:k + 1] == eidx for k in range(TOP_K)]
    onehot = jnp.where(hits[0], 1.0, 0.0)
    for k in range(1, TOP_K):
        onehot = onehot + jnp.where(hits[k], 1.0, 0.0)
    mb = onehot.astype(BF16)
    trow = lax.broadcasted_iota(I32, (TM, TM), 0)
    tcol = lax.broadcasted_iota(I32, (TM, TM), 1)
    before = _dot(mb, (trow < tcol).astype(BF16))
    totals = _dot(mb, jnp.ones((TM, TM), BF16))
    erow = lax.broadcasted_iota(I32, (N_EXPERTS, N_EXPERTS), 0)
    ecol = lax.broadcasted_iota(I32, (N_EXPERTS, N_EXPERTS), 1)
    first = _dot((ecol < erow).astype(BF16), totals.astype(BF16))
    base = first + before
    return [jnp.sum(jnp.where(hits[k], base, 0.0), axis=0, keepdims=True).astype(I32) for k in range(TOP_K)]


def _dispatch_kernel(n_blocks, ss_ref, cn_ref, ps_ref, pl_ref, nu_ref, h_ref, te_ref, xs_hbm, lpos_ref,
                     sbuf, zbuf, sem):
    j = pl.program_id(0)
    n_tiles = pl.num_programs(0)

    def zero_fill(act):
        def per_expert(e, carry):
            n = pl_ref[e]

            @pl.when(n > 0)
            def _():
                act(pltpu.make_async_copy(zbuf.at[_slab_rows(0, n)], xs_hbm.at[_slab_rows(ps_ref[e], n)],
                                          sem.at[MOE_RING]))
            return carry
        lax.fori_loop(0, N_EXPERTS, per_expert, 0)

        def per_block(b, carry):
            act(pltpu.make_async_copy(zbuf, xs_hbm.at[_slab_rows(b * MOE_BM, MOE_BM)], sem.at[MOE_RING]))
            return carry
        lax.fori_loop(nu_ref[0], n_blocks, per_block, 0)

    @pl.when(j == 0)
    def _():
        zbuf[...] = jnp.zeros_like(zbuf)
        zero_fill(lambda cp: cp.start())

    lpos = _local_positions(te_ref[...])
    zi = jnp.zeros_like(lpos[0])
    lpos_ref[...] = jnp.concatenate(lpos + [zi] * (SUBLANES - TOP_K), axis=0)
    riota = lax.broadcasted_iota(I32, (TILE_ROWS, TM), 0)
    hit = riota == lpos[0]
    for k in range(1, TOP_K):
        hit = jnp.logical_or(hit, riota == lpos[k])
    perm = jnp.where(hit, 1.0, 0.0).astype(BF16)
    hb = h_ref[...].astype(BF16)
    slot = lax.rem(j, MOE_RING)
    buf = sbuf.at[slot]

    def rows_copy(sl):
        return pltpu.make_async_copy(sbuf.at[sl], xs_hbm.at[_slab_rows(0, TILE_ROWS)], sem.at[sl])

    @pl.when(j >= MOE_RING)
    def _():
        rows_copy(slot).wait()

    for s in range(0, ROW_SLABS, 2):
        xl = _dot(perm, hb[:, s * LANES:(s + 2) * LANES])
        buf[pl.ds(s, TILE_ROWS, stride=ROW_SLABS), :] = xl[:, :LANES]
        buf[pl.ds(s + 1, TILE_ROWS, stride=ROW_SLABS), :] = xl[:, LANES:]

    local = 0
    for e in range(N_EXPERTS):
        n = cn_ref[j * N_EXPERTS + e]

        @pl.when(n > 0)
        def _(e=e, n=n, local=local):
            pltpu.make_async_copy(buf.at[_slab_rows(local, n)],
                                  xs_hbm.at[_slab_rows(ss_ref[j * N_EXPERTS + e], n)],
                                  sem.at[slot]).start(priority=e % 2)
        local = local + n

    @pl.when(j == n_tiles - 1)
    def _():
        for back in range(MOE_RING):
            rows_copy(lax.rem(j + MOE_RING - back, MOE_RING)).wait()
        zero_fill(lambda cp: cp.wait())


def _dispatch(h, top_e, tabs, n_tok):
    _, n_blocks = _moe_rows(n_tok)
    n_tiles = n_tok // TM
    grid_spec = pltpu.PrefetchScalarGridSpec(
        num_scalar_prefetch=5,
        grid=(n_tiles,),
        in_specs=[
            pl.BlockSpec((TM, D_MODEL), lambda j, *_: (j, 0)),
            pl.BlockSpec((SUBLANES, TM), lambda j, *_: (0, j)),
        ],
        out_specs=[
            pl.BlockSpec(memory_space=pl.ANY),
            pl.BlockSpec((SUBLANES, TM), lambda j, *_: (0, j)),
        ],
        scratch_shapes=[
            pltpu.VMEM((MOE_RING, TILE_ROWS * ROW_SLABS, LANES), F32),
            pltpu.VMEM((MOE_BM * ROW_SLABS, LANES), F32),
            pltpu.SemaphoreType.DMA((MOE_RING + 1,)),
        ],
    )
    return pl.pallas_call(
        functools.partial(_dispatch_kernel, n_blocks),
        grid_spec=grid_spec,
        out_shape=[
            jax.ShapeDtypeStruct((n_blocks * MOE_BM * ROW_SLABS, LANES), F32),
            jax.ShapeDtypeStruct((SUBLANES, n_tok), I32),
        ],
        compiler_params=_params(("arbitrary",), VMEM_LIMIT),
        name="moe_dispatch",
    )(tabs["seg_start"], tabs["cnt"], tabs["pad_start"], tabs["pad_len"], tabs["n_used"], h, top_e)


def _moe_kernel(layer, be_ref, nu_ref, end_ref, ne_ref, x_ref, b1_ref, b2_ref, w1_hbm, w2_hbm, y_ref,
                w1s, w2s, w1b, w2b, sem):
    i = pl.program_id(0)
    nu = nu_ref[0]

    def weight_copies(e):
        return (pltpu.make_async_copy(w1_hbm.at[layer, e], w1s, sem.at[0]),
                pltpu.make_async_copy(w2_hbm.at[layer, e], w2s, sem.at[1]))

    @pl.when(i < nu)
    def _():
        e = be_ref[i]

        @pl.when(i == 0)
        def _():
            for cp in weight_copies(e):
                cp.start()

        @pl.when(jnp.logical_or(i == 0, e != be_ref[jnp.maximum(i - 1, 0)]))
        def _():
            for cp in weight_copies(e):
                cp.wait()
            w1b[...] = w1s[...].astype(BF16)
            w2b[...] = w2s[...].astype(BF16)
            nxt = ne_ref[e]

            @pl.when(nxt >= 0)
            def _():
                for cp in weight_copies(nxt):
                    cp.start()

        def run(n_rows):
            slab_rows = pl.ds(0, n_rows * ROW_SLABS)
            x = _load_row_slabs(x_ref.at[slab_rows], n_rows).astype(BF16)
            h1 = _dot(x, w1b[...]) + b1_ref[0, 0]
            gate = jnp.minimum(h1[:, :D_MODEL], SWIGLU_LIMIT)
            up = jnp.clip(h1[:, D_MODEL:], -SWIGLU_LIMIT, SWIGLU_LIMIT)
            act = (up + 1.0) * gate * _sigmoid(SWIGLU_ALPHA * gate)
            y = _dot(act.astype(BF16), w2b[...]) + b2_ref[0, 0]
            _store_row_slabs(y_ref.at[slab_rows], y, n_rows)

        n_valid = jnp.minimum(end_ref[e] - i * MOE_BM, MOE_BM)
        n_parts = lax.div(n_valid + MOE_PART - 1, MOE_PART)
        for parts in range(1, MOE_BM // MOE_PART + 1):
            @pl.when(n_parts == parts)
            def _(rows=parts * MOE_PART):
                run(rows)
                if rows < MOE_BM:
                    y_ref[pl.ds(rows * ROW_SLABS, (MOE_BM - rows) * ROW_SLABS), :] = jnp.zeros(
                        ((MOE_BM - rows) * ROW_SLABS, LANES), F32)

    @pl.when(i >= nu)
    def _():
        y_ref[...] = jnp.zeros_like(y_ref)


def _moe(xs, tabs, layer, w1, b1, w2, b2, n_tok):
    _, n_blocks = _moe_rows(n_tok)
    d2 = 2 * D_MODEL
    blk = MOE_BM * ROW_SLABS
    grid_spec = pltpu.PrefetchScalarGridSpec(
        num_scalar_prefetch=4,
        grid=(n_blocks,),
        in_specs=[
            pl.BlockSpec((blk, LANES), lambda i, be, nu, *_: (jnp.minimum(i, nu[0] - 1), 0)),
            pl.BlockSpec((1, 1, 1, d2), lambda i, be, *_: (layer, be[i], 0, 0)),
            pl.BlockSpec((1, 1, 1, D_MODEL), lambda i, be, *_: (layer, be[i], 0, 0)),
            pl.BlockSpec(memory_space=pl.ANY),
            pl.BlockSpec(memory_space=pl.ANY),
        ],
        out_specs=pl.BlockSpec((blk, LANES), lambda i, *_: (i, 0)),
        scratch_shapes=[
            pltpu.VMEM((D_MODEL, d2), F32),
            pltpu.VMEM((D_MODEL, D_MODEL), F32),
            pltpu.VMEM((D_MODEL, d2), BF16),
            pltpu.VMEM((D_MODEL, D_MODEL), BF16),
            pltpu.SemaphoreType.DMA((2,)),
        ],
    )
    return pl.pallas_call(
        functools.partial(_moe_kernel, layer),
        grid_spec=grid_spec,
        out_shape=jax.ShapeDtypeStruct((n_blocks * blk, LANES), F32),
        compiler_params=_params(("arbitrary",), VMEM_LIMIT),
        name="moe_experts",
    )(tabs["block_e"], tabs["n_used"], tabs["row_end"], tabs["next_e"], xs,
      b1.reshape(DEPTH, N_EXPERTS, 1, d2), b2.reshape(DEPTH, N_EXPERTS, 1, D_MODEL), w1, w2)


def _combine_kernel(n_next, n_tiles, ss_ref, cn_ref, lpos_ref, tg_ref, x_ref, mod_ref, g_ref, ys_hbm, *rest):
    final = n_next == 0
    next_in, (o_ref, *next_out), (cbuf, sem) = rest[:n_next], rest[n_next:len(rest) - 2], rest[len(rest) - 2:]
    t = pl.program_id(0) * n_tiles + pl.program_id(1)
    n_total = pl.num_programs(0) * n_tiles
    slot = lax.rem(t, MOE_RING)
    ahead = MOE_RING - 1

    def segment_copy(tile, sl, e, local, n):
        return pltpu.make_async_copy(ys_hbm.at[_slab_rows(ss_ref[tile * N_EXPERTS + e], n)],
                                     cbuf.at[sl, _slab_rows(local, n)], sem.at[sl])

    @pl.when(t == 0)
    def _():
        for first in range(ahead):
            def segment(e, local, first=first):
                n = cn_ref[first * N_EXPERTS + e]

                @pl.when(n > 0)
                def _():
                    segment_copy(first, first, e, local, n).start()
                return local + n
            lax.fori_loop(0, N_EXPERTS, segment, 0)

    @pl.when(t + ahead < n_total)
    def _():
        tile = t + ahead
        sl = lax.rem(tile, MOE_RING)
        local = 0
        for e in range(N_EXPERTS):
            n = cn_ref[tile * N_EXPERTS + e]

            @pl.when(n > 0)
            def _(e=e, n=n, local=local):
                segment_copy(tile, sl, e, local, n).start(priority=e % 2)
            local = local + n

    pltpu.make_async_copy(ys_hbm.at[_slab_rows(0, TILE_ROWS)], cbuf.at[slot], sem.at[slot]).wait()

    lpos = lpos_ref[...]
    tg = tg_ref[...]
    riota = lax.broadcasted_iota(I32, (TILE_ROWS, TM), 0)
    hit = riota == lpos[0:1]
    gsel = jnp.where(hit, tg[0:1], 0.0)
    for k in range(1, TOP_K):
        hit_k = riota == lpos[k:k + 1]
        gsel = gsel + jnp.where(hit_k, tg[k:k + 1], 0.0)
        hit = jnp.logical_or(hit, hit_k)
    row_gate = jnp.sum(gsel, axis=1, keepdims=True)
    yg = (_load_row_slabs(cbuf.at[slot], TILE_ROWS) * row_gate).astype(BF16)
    f = _dot_tn(jnp.where(hit, 1.0, 0.0).astype(BF16), yg)

    mod = mod_ref[0, 0]
    xn = x_ref[0] + mod[5:6] * f
    if final:
        ms = jnp.mean(xn * xn, axis=-1, keepdims=True)
        xn = xn * lax.rsqrt(ms + EPS) * g_ref[...]
    o_ref[0] = xn
    if not final:
        _qkv_body(xn, *next_in, *next_out)


def _combine(ys, tabs, lpos, top_g, xres, n_tiles, mod, mod_seg, g, next_qkv):
    final = next_qkv is None
    in_specs = [
        pl.BlockSpec((SUBLANES, TM), lambda b, j, *_: (0, b * n_tiles + j)),
        pl.BlockSpec((SUBLANES, TM), lambda b, j, *_: (0, b * n_tiles + j)),
        pl.BlockSpec((1, TM, D_MODEL), lambda b, j, *_: (b, j, 0)),
        pl.BlockSpec((1, 1, SUBLANES, D_MODEL), lambda b, j, *_: (b, mod_seg(j), 0, 0)),
        pl.BlockSpec((1, D_MODEL), lambda b, j, *_: (0, 0)),
        pl.BlockSpec(memory_space=pl.ANY),
    ]
    out_specs = [pl.BlockSpec((1, TM, D_MODEL), lambda b, j, *_: (b, j, 0))]
    out_shape = [jax.ShapeDtypeStruct((BATCH, n_tiles * TM, D_MODEL), F32)]
    extra = ()
    if not final:
        qkv_in, qkv_out, qkv_shape = _qkv_specs()
        in_specs += qkv_in
        out_specs += qkv_out
        out_shape += qkv_shape
        extra = tuple(next_qkv)
    grid_spec = pltpu.PrefetchScalarGridSpec(
        num_scalar_prefetch=2,
        grid=(BATCH, n_tiles),
        in_specs=in_specs,
        out_specs=out_specs,
        scratch_shapes=[
            pltpu.VMEM((MOE_RING, TILE_ROWS * ROW_SLABS, LANES), F32),
            pltpu.SemaphoreType.DMA((MOE_RING,)),
        ],
    )
    out = pl.pallas_call(
        functools.partial(_combine_kernel, len(extra), n_tiles),
        grid_spec=grid_spec,
        out_shape=out_shape,
        compiler_params=_params(("arbitrary", "arbitrary"), VMEM_LIMIT),
        name="moe_combine",
    )(tabs["seg_start"], tabs["cnt"], lpos, top_g, xres, mod, g, ys, *extra)
    return out[0] if final else out


def _moe_ffn(h, top_e, top_g, cnt, layer, xres, n_tiles, mod, mod_seg, w1, b1, w2, b2, g, next_qkv):
    n_tok = BATCH * n_tiles * TM
    tabs = _route_tables(cnt[:, :, 0], n_tok)
    xs, lpos = _dispatch(h, top_e, tabs, n_tok)
    ys = _moe(xs, tabs, layer, w1, b1, w2, b2, n_tok)
    return _combine(ys, tabs, lpos, top_g, xres, n_tiles, mod, mod_seg, g, next_qkv)


def _qkv_body(x, mod_ref, g_ref, w_ref, qn_ref, kn_ref, cos_ref, sin_ref, q_ref, k_ref, v_ref):
    mod = mod_ref[0, 0]
    h = _norm_mod(x, g_ref[...], mod[0:1], mod[1:2]).astype(BF16)
    qkv = _dot(h, w_ref[...])
    cos = cos_ref[...]
    sin = sin_ref[...]

    def head(xh, gn):
        ms = jnp.mean(xh * xh, axis=-1, keepdims=True)
        y = xh * lax.rsqrt(ms + EPS) * gn
        return y * cos + pltpu.roll(y, C_HEAD_DIM // 2, 1) * sin

    qn = qn_ref[...]
    kn = kn_ref[...]
    for hq in range(C_HEADS):
        sl = slice(hq * C_HEAD_DIM, (hq + 1) * C_HEAD_DIM)
        q_ref[0, :, sl] = head(qkv[:, sl], qn).astype(BF16)
    for hk in range(C_KV_HEADS):
        src = slice((C_HEADS + hk) * C_HEAD_DIM, (C_HEADS + hk + 1) * C_HEAD_DIM)
        k_ref[0, :, hk * C_HEAD_DIM:(hk + 1) * C_HEAD_DIM] = head(qkv[:, src], kn).astype(BF16)
    ones = jnp.ones((TM, C_HEAD_DIM), BF16)
    for hk in range(C_KV_HEADS):
        src = slice((C_HEADS + C_KV_HEADS + hk) * C_HEAD_DIM, (C_HEADS + C_KV_HEADS + hk + 1) * C_HEAD_DIM)
        v_ref[0, :, 2 * hk * C_HEAD_DIM:(2 * hk + 1) * C_HEAD_DIM] = qkv[:, src].astype(BF16)
        v_ref[0, :, (2 * hk + 1) * C_HEAD_DIM:(2 * hk + 2) * C_HEAD_DIM] = ones


def _qkv_specs():
    kvw = C_KV_HEADS * C_HEAD_DIM
    in_specs = [
        pl.BlockSpec((1, 1, SUBLANES, D_MODEL), lambda b, j, *_: (b, jnp.minimum(j, 1), 0, 0)),
        pl.BlockSpec((1, D_MODEL), lambda b, j, *_: (0, 0)),
        pl.BlockSpec((D_MODEL, C_QKV), lambda b, j, *_: (0, 0)),
        pl.BlockSpec((1, C_HEAD_DIM), lambda b, j, *_: (0, 0)),
        pl.BlockSpec((1, C_HEAD_DIM), lambda b, j, *_: (0, 0)),
        pl.BlockSpec((TM, C_HEAD_DIM), lambda b, j, *_: (j, 0)),
        pl.BlockSpec((TM, C_HEAD_DIM), lambda b, j, *_: (j, 0)),
    ]
    out_specs = [
        pl.BlockSpec((1, TM, D_MODEL), lambda b, j, *_: (b, jnp.maximum(j - 1, 0), 0)),
        pl.BlockSpec((1, TM, kvw), lambda b, j, *_: (b, j, 0)),
        pl.BlockSpec((1, TM, 2 * kvw), lambda b, j, *_: (b, j, 0)),
    ]
    out_shape = [
        jax.ShapeDtypeStruct((BATCH, SEQ, D_MODEL), BF16),
        jax.ShapeDtypeStruct((BATCH, L_ALL, kvw), BF16),
        jax.ShapeDtypeStruct((BATCH, L_ALL, 2 * kvw), BF16),
    ]
    return in_specs, out_specs, out_shape


def _rope_permute(t):
    rows, cols = t.shape
    t = t.reshape(rows, cols // C_HEAD_DIM, 2, 2, C_HEAD_DIM // 4)
    return t.transpose(0, 1, 3, 2, 4).reshape(rows, cols)


def _rope_tables():
    rows = SEQ // GRID_W
    row = jnp.repeat(jnp.arange(rows), GRID_W).astype(F32)
    col = jnp.tile(jnp.arange(GRID_W), rows).astype(F32)
    half = C_HEAD_DIM // 2
    inv_freq = ROPE_THETA ** (-jnp.arange(0, half, 2, dtype=F32) / half)
    ar = row[:, None] * inv_freq
    ac = col[:, None] * inv_freq
    cos = jnp.concatenate([jnp.cos(ar), jnp.cos(ac), jnp.cos(ar), jnp.cos(ac)], axis=-1)
    sin = jnp.concatenate([-jnp.sin(ar), -jnp.sin(ac), jnp.sin(ar), jnp.sin(ac)], axis=-1)
    cos = jnp.concatenate([jnp.ones((CTX_LEN, C_HEAD_DIM), F32), cos], axis=0)
    sin = jnp.concatenate([jnp.zeros((CTX_LEN, C_HEAD_DIM), F32), sin], axis=0)
    return cos, sin


def _attn_kernel(q_ref, k_ref, v_ref, o_ref):
    c = (C_HEAD_DIM ** -0.5) * LOG2_E
    def head_cols(h):
        return slice(h * C_HEAD_DIM, (h + 1) * C_HEAD_DIM)

    def scores(h):
        return _dot_nt(q_ref[0, :, head_cols(h)], k_ref[0, :, head_cols(h // C_GROUP)])

    s_next = scores(0)
    for h in range(C_HEADS):
        s = s_next
        if h + 1 < C_HEADS:
            s_next = scores(h + 1)
        g = h // C_GROUP
        m = jnp.max(s, axis=-1, keepdims=True)
        p = jnp.exp2((s - m) * c).astype(BF16)
        oe = _dot(p, v_ref[0, :, 2 * g * C_HEAD_DIM:(2 * g + 2) * C_HEAD_DIM])
        o_ref[0, :, head_cols(h)] = (oe[:, :C_HEAD_DIM] / oe[:, C_HEAD_DIM:]).astype(BF16)


def _attention(q, k, v):
    kvw = C_KV_HEADS * C_HEAD_DIM
    return pl.pallas_call(
        _attn_kernel,
        grid=(BATCH, SEQ // ATT_TQ),
        in_specs=[
            pl.BlockSpec((1, ATT_TQ, D_MODEL), lambda b, j: (b, j, 0)),
            pl.BlockSpec((1, L_ALL, kvw), lambda b, j: (b, 0, 0)),
            pl.BlockSpec((1, L_ALL, 2 * kvw), lambda b, j: (b, 0, 0)),
        ],
        out_specs=pl.BlockSpec((1, ATT_TQ, D_MODEL), lambda b, j: (b, j, 0)),
        out_shape=jax.ShapeDtypeStruct((BATCH, SEQ, D_MODEL), BF16),
        compiler_params=_params(("arbitrary", "arbitrary"), VMEM_LIMIT),
        name="attention",
    )(q, k, v)


def _mod_table(ada_layer):
    m = ada_layer.reshape(ADA_ROWS, N_MOD, D_MODEL)
    m_lat = m[:BATCH]
    m_ctx = jnp.broadcast_to(m[BATCH], (BATCH, N_MOD, D_MODEL))
    t = jnp.stack([m_ctx, m_lat], axis=1)
    return jnp.pad(t, ((0, 0), (0, 0), (0, SUBLANES - N_MOD), (0, 0)))


def kernel(x, c, ctx, c_ctx, ada_w, ada_b, norm_mix, norm_ffn, ab_w_in, ab_gate_w, ab_gate_b, ab_out_norm,
           ab_conv_w, ab_conv_b, ab_w_out, attn_w_qkv, attn_q_norm, attn_k_norm, attn_w_o, router_w,
           router_b, moe_w1, moe_b1, moe_w2, moe_b2, final_norm):
    assert x.shape == (BATCH, SEQ, D_MODEL) and ctx.shape == (BATCH, CTX_LEN, D_MODEL), (x.shape, ctx.shape)
    assert moe_w1.shape == (DEPTH, N_EXPERTS, D_MODEL, 2 * D_MODEL), moe_w1.shape
    cond = jnp.zeros((ADA_ROWS, D_MODEL), F32).at[:BATCH].set(c).at[BATCH].set(c_ctx)
    ada = _ada(cond, ada_w, ada_b)
    mod0 = _mod_table(ada[0])
    mod1 = _mod_table(ada[1])
    seg_all = lambda j: jnp.minimum(j, 1)
    seg_lat = lambda j: 1
    n_tiles_all = L_ALL // TM
    n_tiles_lat = SEQ // TM
    fnorm = final_norm.reshape(1, D_MODEL)

    w_in = ab_w_in[0]
    lr0 = 2 * A_HEADS * A_DK + A_HEADS * A_DV
    lr1 = lr0 + 2 * A_GATE_RANK
    w_main = jnp.concatenate([w_in[:, :lr0], w_in[:, lr1:]], axis=1).astype(BF16)
    w_lr = jnp.pad(w_in[:, lr0:lr1], ((0, 0), (0, LANES - 2 * A_GATE_RANK))).astype(BF16)
    p, lr = _proj_in(ctx, x, mod0, norm_mix[0:1], w_main, w_lr)

    gw = ab_gate_w[0]
    gw_pad = jnp.zeros((2, LANES, A_HEADS * A_DK), F32)
    gw_pad = gw_pad.at[0, :A_GATE_RANK].set(gw[0]).at[1, A_GATE_RANK:2 * A_GATE_RANK].set(gw[1])
    cw = jnp.concatenate([ab_conv_w[0], ab_conv_b[0][None], jnp.zeros((SUBLANES - 4, B_WIDTH), F32)], axis=0)
    yg, yc = _gla_conv(p, lr, gw_pad.astype(BF16), ab_gate_b[0].reshape(2, 1, A_HEADS * A_DK),
                       ab_out_norm[0:1], cw)

    w_out = ab_w_out[0].astype(BF16)
    ngla = A_HEADS * A_DV
    xmid, h, top_e, top_g, cnt = _outproj(
        [yg, yc], [w_out[:ngla], w_out[ngla:]], [ctx, x], _ctx_latent_specs(), n_tiles_all, mod0, seg_all,
        norm_ffn[0:1], router_w[0].T, router_b[0].reshape(N_EXPERTS, 1))
    cos, sin = _rope_tables()
    qk_cols = (C_HEADS + C_KV_HEADS) * C_HEAD_DIM
    w_qkv = jnp.concatenate([_rope_permute(attn_w_qkv[0][:, :qk_cols]), attn_w_qkv[0][:, qk_cols:]],
                            axis=1).astype(BF16)
    next_qkv = (mod1, norm_mix[1:2], w_qkv, _rope_permute(attn_q_norm[0:1]), _rope_permute(attn_k_norm[0:1]),
                cos, sin)
    x1, q, k, v = _moe_ffn(h, top_e, top_g, cnt, 0, xmid, n_tiles_all, mod0, seg_all,
                           moe_w1, moe_b1, moe_w2, moe_b2, fnorm, next_qkv)

    o = _attention(q, k, v)
    xmid, h, top_e, top_g, cnt = _outproj(
        [o], [attn_w_o[0].astype(BF16)], [x1],
        [pl.BlockSpec((1, TM, D_MODEL), lambda b, j: (b, j + CTX_LEN // TM, 0))], n_tiles_lat, mod1, seg_lat,
        norm_ffn[1:2], router_w[1].T, router_b[1].reshape(N_EXPERTS, 1))
    return _moe_ffn(h, top_e, top_g, cnt, 1, xmid, n_tiles_lat, mod1, seg_lat,
                    moe_w1, moe_b1, moe_w2, moe_b2, fnorm, None)
```

```python
import functools

import jax
import jax.numpy as jnp
from jax import lax
from jax.experimental import pallas as pl
from jax.experimental.pallas import tpu as pltpu

F32 = jnp.float32
BF16 = jnp.bfloat16
I32 = jnp.int32

D_MODEL = 1024
BATCH = 8
SEQ = 2048
DEPTH = 2
GRID_W = 64
CTX_LEN = 256
L_ALL = CTX_LEN + SEQ
N_MOD = 6
EPS = 1e-6

A_DV = 128
A_HEADS = 4
A_DK = 64
A_GATE_RANK = 16
A_GATE_TAU = 16.0
GLA_CHUNK = 64
B_WIDTH = 512
AB_MAIN = 3072

C_HEAD_DIM = 128
C_HEADS = 8
C_KV_HEADS = 2
C_GROUP = 4
C_QKV = (C_HEADS + 2 * C_KV_HEADS) * C_HEAD_DIM
ROPE_THETA = 10000.0
LOG2_E = 1.4426950408889634

N_EXPERTS = 32
TOP_K = 4
SWIGLU_LIMIT = 7.0
SWIGLU_ALPHA = 1.702

LANES = 128
SUBLANES = 8
ROW_SLABS = D_MODEL // LANES
TM = 256
ATT_TQ = 1024
TILE_ROWS = TM * TOP_K
MOE_BM = 512
MOE_PART = 128
MOE_RING = 3
ADA_ROWS = 16
ADA_TN = 1536
VMEM_LIMIT = 56 * 1024 * 1024

assert CTX_LEN == TM and SEQ % TM == 0 and SEQ % ATT_TQ == 0
assert TM % GLA_CHUNK == 0 and BATCH + 1 <= ADA_ROWS and (N_MOD * D_MODEL) % ADA_TN == 0
assert MOE_RING <= SEQ // TM and MOE_BM % MOE_PART == 0

NT_DIMS = (((1,), (1,)), ((), ()))
TN_DIMS = (((0,), (0,)), ((), ()))


def _dot(a, b):
    return jnp.dot(a, b, preferred_element_type=F32)


def _dot_nt(a, b):
    return lax.dot_general(a, b, NT_DIMS, preferred_element_type=F32)


def _dot_tn(a, b):
    return lax.dot_general(a, b, TN_DIMS, preferred_element_type=F32)


def _params(sem, vmem=None):
    return pltpu.CompilerParams(dimension_semantics=sem, vmem_limit_bytes=vmem)


def _norm_mod(x, g, shift, scale):
    ms = jnp.mean(x * x, axis=-1, keepdims=True)
    y = x * lax.rsqrt(ms + EPS) * g
    return y * (1.0 + scale) + shift


def _sigmoid(x):
    return 1.0 / (1.0 + jnp.exp(-x))


def _ada_kernel(cond_ref, w_ref, b_ref, o_ref):
    c = cond_ref[...]
    s = (c * _sigmoid(c)).astype(BF16)
    o_ref[0] = _dot(s, w_ref[0].astype(BF16)) + b_ref[0]


def _ada(cond, ada_w, ada_b):
    tn = ADA_TN
    n = N_MOD * D_MODEL
    return pl.pallas_call(
        _ada_kernel,
        grid=(DEPTH, n // tn),
        in_specs=[
            pl.BlockSpec((ADA_ROWS, D_MODEL), lambda l, j: (0, 0)),
            pl.BlockSpec((1, D_MODEL, tn), lambda l, j: (l, 0, j)),
            pl.BlockSpec((1, 1, tn), lambda l, j: (l, 0, j)),
        ],
        out_specs=pl.BlockSpec((1, ADA_ROWS, tn), lambda l, j: (l, 0, j)),
        out_shape=jax.ShapeDtypeStruct((DEPTH, ADA_ROWS, n), F32),
        compiler_params=_params(("arbitrary", "arbitrary"), VMEM_LIMIT),
        name="ada",
    )(cond, ada_w, ada_b.reshape(DEPTH, 1, n))


def _ctx_or_latent(ctx_ref, x_ref):
    return jnp.where(pl.program_id(1) == 0, ctx_ref[0], x_ref[0])


def _ctx_latent_specs():
    return [pl.BlockSpec((1, TM, D_MODEL), lambda b, j: (b, 0, 0)),
            pl.BlockSpec((1, TM, D_MODEL), lambda b, j: (b, jnp.maximum(j - 1, 0), 0))]


def _proj_in_kernel(ctx_ref, x_ref, mod_ref, g_ref, w_ref, wlr_ref, p_ref, lr_ref):
    mod = mod_ref[0, 0]
    h = _norm_mod(_ctx_or_latent(ctx_ref, x_ref), g_ref[...], mod[0:1], mod[1:2]).astype(BF16)
    p_ref[0] = _dot(h, w_ref[...]).astype(BF16)
    lr_ref[0] = _dot(h, wlr_ref[...]).astype(BF16)


def _proj_in(ctx, x, mod, g, w_main, w_lr):
    nj = L_ALL // TM
    return pl.pallas_call(
        _proj_in_kernel,
        grid=(BATCH, nj),
        in_specs=_ctx_latent_specs() + [
            pl.BlockSpec((1, 1, SUBLANES, D_MODEL), lambda b, j: (b, jnp.minimum(j, 1), 0, 0)),
            pl.BlockSpec((1, D_MODEL), lambda b, j: (0, 0)),
            pl.BlockSpec((D_MODEL, AB_MAIN), lambda b, j: (0, 0)),
            pl.BlockSpec((D_MODEL, LANES), lambda b, j: (0, 0)),
        ],
        out_specs=[
            pl.BlockSpec((1, TM, AB_MAIN), lambda b, j: (b, j, 0)),
            pl.BlockSpec((1, TM, LANES), lambda b, j: (b, j, 0)),
        ],
        out_shape=[
            jax.ShapeDtypeStruct((BATCH, L_ALL, AB_MAIN), BF16),
            jax.ShapeDtypeStruct((BATCH, L_ALL, LANES), BF16),
        ],
        compiler_params=_params(("arbitrary", "arbitrary"), VMEM_LIMIT),
        name="proj_in",
    )(ctx, x, mod, g, w_main, w_lr)


def _log_sigmoid(z):
    return jnp.minimum(z, 0.0) - jnp.log1p(jnp.exp(-jnp.abs(z)))


def _gla_conv_kernel(q_ref, k_ref, v_ref, r_ref, gb_ref, gc_ref, u_ref, lr_ref, gw_ref, gbias_ref,
                     onorm_ref, cw_ref, yg_ref, yc_ref, of_ref, ob_ref, sf_ref, sb_ref, xs_ref):
    ch = GLA_CHUNK
    grp = TM
    n_grp = L_ALL // grp
    cpg = grp // ch
    row = lax.broadcasted_iota(I32, (grp, grp), 0)
    col = lax.broadcasted_iota(I32, (grp, grp), 1)
    same_chunk = (row // ch) == (col // ch)
    tri = (jnp.logical_and(same_chunk, row >= col), jnp.logical_and(same_chunk, col >= row))
    tri_bf = (tri[0].astype(BF16), tri[1].astype(BF16))
    lane = lax.broadcasted_iota(I32, (grp, LANES), 1)
    head_mask = (lane < A_DK, lane >= A_DK)
    chunk_of_row = lax.broadcasted_iota(I32, (grp, LANES), 0) // ch

    sf_ref[...] = jnp.zeros_like(sf_ref)
    sb_ref[...] = jnp.zeros_like(sb_ref)

    def chunk_rows(x, idx):
        return jnp.concatenate(
            [jnp.broadcast_to(x[c * ch + idx:c * ch + idx + 1], (ch, x.shape[1])) for c in range(cpg)], axis=0)

    s_refs = (sf_ref, sb_ref)
    o_refs = (of_ref, ob_ref)
    dirs = (0, 1)
    pairs = [(d, hh) for d in dirs for hh in range(2)]

    def by_chunk_lanes(m, x):
        return jnp.concatenate([jnp.where(jnp.logical_and(m, chunk_of_row == c), x, 0.0) for c in range(cpg)],
                               axis=1).astype(BF16)

    def body(i, carry):
        gidx = (i, jnp.where(i == 0, 0, n_grp - i))
        rows = [pl.ds(pl.multiple_of(g * grp, grp), grp) for g in gidx]
        q = [q_ref[0, r, :].astype(F32) * (A_DK ** -0.5) for r in rows]
        k = [k_ref[0, r, :].astype(F32) for r in rows]
        v = [v_ref[0, r, :] for r in rows]
        z = [_dot(lr_ref[0, rows[d], :].astype(BF16), gw_ref[d]) + gbias_ref[d] for d in dirs]
        a = [_log_sigmoid(z[d]) * (1.0 / A_GATE_TAU) for d in dirs]
        a_hi = [a[d].astype(BF16) for d in dirs]
        a_lo = [(a[d] - a_hi[d].astype(F32)).astype(BF16) for d in dirs]
        cum = [_dot(tri_bf[d], a_hi[d]) + _dot(tri_bf[d], a_lo[d]) for d in dirs]
        ref = (chunk_rows(cum[0], ch // 2 - 1), chunk_rows(cum[1], ch // 2))
        last = (chunk_rows(cum[0], ch - 1), chunk_rows(cum[1], 0))
        order = (range(cpg), range(cpg - 1, -1, -1))
        qe = [q[d] * jnp.exp(cum[d]) for d in dirs]
        qt = [q[d] * jnp.exp(cum[d] - ref[d]) for d in dirs]
        kt = [(k[d] * jnp.exp(ref[d] - cum[d])).astype(BF16) for d in dirs]
        kl = [k[d] * jnp.exp(last[d] - cum[d]) for d in dirs]
        dec = [jnp.exp(last[d]) for d in dirs]
        vh = {(d, hh): v[d][:, hh * A_DV:(hh + 1) * A_DV].astype(BF16) for d, hh in pairs}
        sc = {(d, hh): _dot_nt(jnp.where(head_mask[hh], qt[d], 0.0).astype(BF16), kt[d]) for d, hh in pairs}
        kv = {(d, hh): _dot(vh[d, hh].T, by_chunk_lanes(head_mask[hh], kl[d])) for d, hh in pairs}
        o_intra = {(d, hh): _dot(jnp.where(tri[d], sc[d, hh], 0.0).astype(BF16), vh[d, hh]) for d, hh in pairs}
        for d, hh in pairs:
            st = s_refs[d][hh]
            states = [None] * cpg
            for c in order[d]:
                states[c] = st
                st = st * dec[d][c * ch:c * ch + 1] + kv[d, hh][:, c * LANES:(c + 1) * LANES]
            s_refs[d][hh] = st
            qem = jnp.where(head_mask[hh], qe[d], 0.0).astype(BF16)
            o_inter = jnp.concatenate(
                [_dot_nt(qem[c * ch:(c + 1) * ch], states[c].astype(BF16)) for c in range(cpg)], axis=0)
            o_refs[d][rows[d], hh * A_DV:(hh + 1) * A_DV] = o_intra[d, hh] + o_inter
        return carry

    lax.fori_loop(0, n_grp, body, 0)

    pad = SUBLANES
    xs_ref[0:pad, :] = jnp.zeros((pad, xs_ref.shape[1]), F32)
    xs_ref[pad + L_ALL:, :] = jnp.zeros((pad, xs_ref.shape[1]), F32)
    xs_ref[pad:pad + L_ALL, :] = gc_ref[0].astype(F32) * u_ref[0].astype(F32)

    cw = cw_ref[...]
    onorm = onorm_ref[...]
    trow = lax.broadcasted_iota(I32, (TM, 1), 0)
    for ti in range(L_ALL // TM):
        s0 = ti * TM
        rows = slice(s0, s0 + TM)
        o = of_ref[rows, :] + ob_ref[rows, :]
        r = r_ref[0, rows, :].astype(F32)
        parts = []
        for hh in range(2):
            oh = o[:, hh * A_DV:(hh + 1) * A_DV]
            ms = jnp.mean(oh * oh, axis=-1, keepdims=True)
            parts.append(oh * lax.rsqrt(ms + EPS) * onorm[:, hh * A_DV:(hh + 1) * A_DV])
        on = jnp.concatenate(parts, axis=1)
        yg_ref[0, rows, :] = (on * (r * _sigmoid(r))).astype(BF16)
        xm1 = xs_ref[pad + s0 - 1:pad + s0 - 1 + TM, :]
        x0 = xs_ref[pad + s0:pad + s0 + TM, :]
        xp1 = xs_ref[pad + s0 + 1:pad + s0 + 1 + TM, :]
        if s0 + TM == CTX_LEN:
            xp1 = jnp.where(trow == TM - 1, 0.0, xp1)
        if s0 == CTX_LEN:
            xm1 = jnp.where(trow == 0, 0.0, xm1)
        conv = cw[0:1] * xm1 + cw[1:2] * x0 + cw[2:3] * xp1 + cw[3:4]
        yc_ref[0, rows, :] = (gb_ref[0, rows, :].astype(F32) * conv).astype(BF16)


def _gla_conv(p, lr, gw_pad, gbias, onorm, cw):
    hw = 2 * A_DV

    def pspec(width, base):
        return pl.BlockSpec((1, L_ALL, width), lambda b, i: (b, 0, base + i))

    return pl.pallas_call(
        _gla_conv_kernel,
        grid=(BATCH, A_HEADS // 2),
        in_specs=[
            pspec(LANES, 0),
            pspec(LANES, 2),
            pspec(hw, 2),
            pspec(hw, 4),
            pspec(hw, 6),
            pspec(hw, 8),
            pspec(hw, 10),
            pl.BlockSpec((1, L_ALL, LANES), lambda b, i: (b, 0, 0)),
            pl.BlockSpec((2, LANES, LANES), lambda b, i: (0, 0, i)),
            pl.BlockSpec((2, 1, LANES), lambda b, i: (0, 0, i)),
            pl.BlockSpec((1, hw), lambda b, i: (0, i)),
            pl.BlockSpec((SUBLANES, hw), lambda b, i: (0, i)),
        ],
        out_specs=[
            pl.BlockSpec((1, L_ALL, hw), lambda b, i: (b, 0, i)),
            pl.BlockSpec((1, L_ALL, hw), lambda b, i: (b, 0, i)),
        ],
        out_shape=[
            jax.ShapeDtypeStruct((BATCH, L_ALL, A_HEADS * A_DV), BF16),
            jax.ShapeDtypeStruct((BATCH, L_ALL, B_WIDTH), BF16),
        ],
        scratch_shapes=[
            pltpu.VMEM((L_ALL, hw), F32),
            pltpu.VMEM((L_ALL, hw), F32),
            pltpu.VMEM((2, A_DV, LANES), F32),
            pltpu.VMEM((2, A_DV, LANES), F32),
            pltpu.VMEM((L_ALL + 2 * SUBLANES, hw), F32),
        ],
        compiler_params=_params(("arbitrary", "arbitrary"), VMEM_LIMIT),
        name="gla_conv",
    )(p, p, p, p, p, p, p, lr, gw_pad, gbias, onorm, cw)


def _outproj_kernel(n_in, n_res, *refs):
    y_refs = refs[:n_in]
    w_refs = refs[n_in:2 * n_in]
    res_refs = refs[2 * n_in:2 * n_in + n_res]
    mod_ref, g_ref, rwt_ref, rb_ref, xo_ref, h_ref, te_ref, tg_ref, cnt_ref = refs[2 * n_in + n_res:]
    is_ctx = pl.program_id(1) == 0
    mod = mod_ref[0, 0]
    wt = rwt_ref[...]
    wb = wt.astype(BF16)
    wl = (wt - wb.astype(F32)).astype(BF16)
    eidx = lax.broadcasted_iota(I32, (N_EXPERTS, LANES), 0)
    cnt = jnp.zeros((N_EXPERTS, 1), I32)
    groups = [slice(r0, r0 + LANES) for r0 in range(0, TM, LANES)]
    accs = []
    for rows in groups:
        acc = _dot(y_refs[0][0, rows, :], w_refs[0][...])
        for i in range(1, n_in):
            acc = acc + _dot(y_refs[i][0, rows, :], w_refs[i][...])
        accs.append(acc)
    hs = []
    for rows, acc in zip(groups, accs):
        xres = res_refs[0][0, rows, :]
        if n_res == 2:
            xres = jnp.where(is_ctx, xres, res_refs[1][0, rows, :])
        xn = xres + mod[2:3] * acc
        xo_ref[0, rows, :] = xn
        h = _norm_mod(xn, g_ref[...], mod[3:4], mod[4:5])
        h_ref[rows, :] = h.astype(BF16)
        hs.append(h)
    all_logits = []
    for h in hs:
        hb = h.astype(BF16)
        hl = (h - hb.astype(F32)).astype(BF16)
        all_logits.append(_dot_nt(wb, hb) + _dot_nt(wb, hl) + _dot_nt(wl, hb) + rb_ref[...])
    for rows, logits in zip(groups, all_logits):
        cur = logits
        vals, idxs = [], []
        for _ in range(TOP_K):
            m = jnp.max(cur, axis=0, keepdims=True)
            sel = jnp.min(jnp.where(cur == m, eidx, N_EXPERTS), axis=0, keepdims=True)
            vals.append(m)
            idxs.append(sel)
            cur = jnp.where(eidx == sel, -jnp.inf, cur)
        ex = [jnp.exp(v - vals[0]) for v in vals]
        den = ex[0] + ex[1] + ex[2] + ex[3]
        zi = jnp.zeros_like(idxs[0])
        zf = jnp.zeros_like(den)
        te_ref[:, rows] = jnp.concatenate(idxs + [zi] * (SUBLANES - TOP_K), axis=0)
        tg_ref[:, rows] = jnp.concatenate([e / den for e in ex] + [zf] * (SUBLANES - TOP_K), axis=0)
        onehot = jnp.where(eidx == idxs[0], 1, 0)
        for k in range(1, TOP_K):
            onehot = onehot + jnp.where(eidx == idxs[k], 1, 0)
        cnt = cnt + jnp.sum(onehot, axis=1, keepdims=True)
    cnt_ref[0] = jnp.broadcast_to(cnt, (N_EXPERTS, LANES))


def _outproj(ys, ws, res, res_specs, n_tiles, mod, mod_seg, g, rwt, rb):
    n_in = len(ys)
    n_tok = BATCH * n_tiles * TM
    in_specs = []
    for y in ys:
        in_specs.append(pl.BlockSpec((1, TM, y.shape[2]), lambda b, j: (b, j, 0)))
    for w in ws:
        in_specs.append(pl.BlockSpec(w.shape, lambda b, j: (0, 0)))
    in_specs += list(res_specs) + [
        pl.BlockSpec((1, 1, SUBLANES, D_MODEL), lambda b, j: (b, mod_seg(j), 0, 0)),
        pl.BlockSpec((1, D_MODEL), lambda b, j: (0, 0)),
        pl.BlockSpec((N_EXPERTS, D_MODEL), lambda b, j: (0, 0)),
        pl.BlockSpec((N_EXPERTS, 1), lambda b, j: (0, 0)),
    ]
    return pl.pallas_call(
        functools.partial(_outproj_kernel, n_in, len(res)),
        grid=(BATCH, n_tiles),
        in_specs=in_specs,
        out_specs=[
            pl.BlockSpec((1, TM, D_MODEL), lambda b, j: (b, j, 0)),
            pl.BlockSpec((TM, D_MODEL), lambda b, j: (b * n_tiles + j, 0)),
            pl.BlockSpec((SUBLANES, TM), lambda b, j: (0, b * n_tiles + j)),
            pl.BlockSpec((SUBLANES, TM), lambda b, j: (0, b * n_tiles + j)),
            pl.BlockSpec((1, N_EXPERTS, LANES), lambda b, j: (b * n_tiles + j, 0, 0)),
        ],
        out_shape=[
            jax.ShapeDtypeStruct((BATCH, n_tiles * TM, D_MODEL), F32),
            jax.ShapeDtypeStruct((n_tok, D_MODEL), BF16),
            jax.ShapeDtypeStruct((SUBLANES, n_tok), I32),
            jax.ShapeDtypeStruct((SUBLANES, n_tok), F32),
            jax.ShapeDtypeStruct((BATCH * n_tiles, N_EXPERTS, LANES), I32),
        ],
        compiler_params=_params(("arbitrary", "arbitrary"), VMEM_LIMIT),
        name="outproj_router",
    )(*ys, *ws, *res, mod, g, rwt, rb)


def _moe_rows(n_tok):
    n_assign = n_tok * TOP_K
    n_blocks = -(-(n_assign + N_EXPERTS * (MOE_BM - 1)) // MOE_BM)
    return n_assign, n_blocks


def _route_tables(cnt, n_tok):
    _, n_blocks = _moe_rows(n_tok)
    counts = jnp.sum(cnt, axis=0)
    padded = (counts + MOE_BM - 1) // MOE_BM * MOE_BM
    pad_end = jnp.cumsum(padded)
    block_start = pad_end - padded
    seg_start = block_start[None, :] + jnp.cumsum(cnt, axis=0) - cnt
    blk_row = jnp.arange(n_blocks, dtype=I32) * MOE_BM
    block_e = jnp.minimum(jnp.sum((pad_end[None, :] <= blk_row[:, None]).astype(I32), axis=1), N_EXPERTS - 1)
    n_used = (pad_end[-1] // MOE_BM).reshape(1)
    eids = jnp.arange(N_EXPERTS, dtype=I32)
    later = jnp.where(jnp.logical_and(eids[None, :] > eids[:, None], counts[None, :] > 0), eids[None, :], N_EXPERTS)
    nxt = jnp.min(later, axis=1)
    next_e = jnp.where(nxt == N_EXPERTS, -1, nxt)
    return dict(block_e=block_e.astype(I32), n_used=n_used.astype(I32),
                row_end=(block_start + counts).astype(I32), next_e=next_e.astype(I32),
                seg_start=seg_start.reshape(-1).astype(I32), cnt=cnt.reshape(-1).astype(I32),
                pad_start=(block_start + counts).astype(I32), pad_len=(padded - counts).astype(I32))


def _store_row_slabs(ref, val, n_rows):
    for s in range(ROW_SLABS):
        ref[pl.ds(s, n_rows, stride=ROW_SLABS), :] = val[:, s * LANES:(s + 1) * LANES]


def _load_row_slabs(ref, n_rows):
    return jnp.concatenate(
        [ref[pl.ds(s, n_rows, stride=ROW_SLABS), :] for s in range(ROW_SLABS)], axis=1)


def _slab_rows(start, n):
    return pl.ds(pl.multiple_of(start * ROW_SLABS, ROW_SLABS), n * ROW_SLABS)


def _local_positions(te):
    eidx = lax.broadcasted_iota(I32, (N_EXPERTS, TM), 0)
    hits = [te[k:k + 1] == eidx for k in range(TOP_K)]
    onehot = jnp.where(hits[0], 1.0, 0.0)
    for k in range(1, TOP_K):
        onehot = onehot + jnp.where(hits[k], 1.0, 0.0)
    mb = onehot.astype(BF16)
    trow = lax.broadcasted_iota(I32, (TM, TM), 0)
    tcol = lax.broadcasted_iota(I32, (TM, TM), 1)
    before = _dot(mb, (trow < tcol).astype(BF16))
    totals = _dot(mb, jnp.ones((TM, TM), BF16))
    erow = lax.broadcasted_iota(I32, (N_EXPERTS, N_EXPERTS), 0)
    ecol = lax.broadcasted_iota(I32, (N_EXPERTS, N_EXPERTS), 1)
    first = _dot((ecol < erow).astype(BF16), totals.astype(BF16))
    base = first + before
    return [jnp.sum(jnp.where(hits[k], base, 0.0), axis=0, keepdims=True).astype(I32) for k in range(TOP_K)]


def _dispatch_kernel(n_blocks, ss_ref, cn_ref, ps_ref, pl_ref, nu_ref, h_ref, te_ref, xs_hbm, lpos_ref,
                     sbuf, zbuf, sem):
    j = pl.program_id(0)
    n_tiles = pl.num_programs(0)

    def zero_fill(act):
        def per_expert(e, carry):
            n = pl_ref[e]

            @pl.when(n > 0)
            def _():
                act(pltpu.make_async_copy(zbuf.at[_slab_rows(0, n)], xs_hbm.at[_slab_rows(ps_ref[e], n)],
                                          sem.at[MOE_RING]))
            return carry
        lax.fori_loop(0, N_EXPERTS, per_expert, 0)

        def per_block(b, carry):
            act(pltpu.make_async_copy(zbuf, xs_hbm.at[_slab_rows(b * MOE_BM, MOE_BM)], sem.at[MOE_RING]))
            return carry
        lax.fori_loop(nu_ref[0], n_blocks, per_block, 0)

    @pl.when(j == 0)
    def _():
        zbuf[...] = jnp.zeros_like(zbuf)
        zero_fill(lambda cp: cp.start())

    lpos = _local_positions(te_ref[...])
    zi = jnp.zeros_like(lpos[0])
    lpos_ref[...] = jnp.concatenate(lpos + [zi] * (SUBLANES - TOP_K), axis=0)
    riota = lax.broadcasted_iota(I32, (TILE_ROWS, TM), 0)
    hit = riota == lpos[0]
    for k in range(1, TOP_K):
        hit = jnp.logical_or(hit, riota == lpos[k])
    perm = jnp.where(hit, 1.0, 0.0).astype(BF16)
    hb = h_ref[...].astype(BF16)
    slot = lax.rem(j, MOE_RING)
    buf = sbuf.at[slot]

    def rows_copy(sl):
        return pltpu.make_async_copy(sbuf.at[sl], xs_hbm.at[_slab_rows(0, TILE_ROWS)], sem.at[sl])

    @pl.when(j >= MOE_RING)
    def _():
        rows_copy(slot).wait()

    for s in range(0, ROW_SLABS, 2):
        xl = _dot(perm, hb[:, s * LANES:(s + 2) * LANES])
        buf[pl.ds(s, TILE_ROWS, stride=ROW_SLABS), :] = xl[:, :LANES]
        buf[pl.ds(s + 1, TILE_ROWS, stride=ROW_SLABS), :] = xl[:, LANES:]

    local = 0
    for e in range(N_EXPERTS):
        n = cn_ref[j * N_EXPERTS + e]

        @pl.when(n > 0)
        def _(e=e, n=n, local=local):
            pltpu.make_async_copy(buf.at[_slab_rows(local, n)],
                                  xs_hbm.at[_slab_rows(ss_ref[j * N_EXPERTS + e], n)],
                                  sem.at[slot]).start(priority=e % 2)
        local = local + n

    @pl.when(j == n_tiles - 1)
    def _():
        for back in range(MOE_RING):
            rows_copy(lax.rem(j + MOE_RING - back, MOE_RING)).wait()
        zero_fill(lambda cp: cp.wait())


def _dispatch(h, top_e, tabs, n_tok):
    _, n_blocks = _moe_rows(n_tok)
    n_tiles = n_tok // TM
    grid_spec = pltpu.PrefetchScalarGridSpec(
        num_scalar_prefetch=5,
        grid=(n_tiles,),
        in_specs=[
            pl.BlockSpec((TM, D_MODEL), lambda j, *_: (j, 0)),
            pl.BlockSpec((SUBLANES, TM), lambda j, *_: (0, j)),
        ],
        out_specs=[
            pl.BlockSpec(memory_space=pl.ANY),
            pl.BlockSpec((SUBLANES, TM), lambda j, *_: (0, j)),
        ],
        scratch_shapes=[
            pltpu.VMEM((MOE_RING, TILE_ROWS * ROW_SLABS, LANES), F32),
            pltpu.VMEM((MOE_BM * ROW_SLABS, LANES), F32),
            pltpu.SemaphoreType.DMA((MOE_RING + 1,)),
        ],
    )
    return pl.pallas_call(
        functools.partial(_dispatch_kernel, n_blocks),
        grid_spec=grid_spec,
        out_shape=[
            jax.ShapeDtypeStruct((n_blocks * MOE_BM * ROW_SLABS, LANES), F32),
            jax.ShapeDtypeStruct((SUBLANES, n_tok), I32),
        ],
        compiler_params=_params(("arbitrary",), VMEM_LIMIT),
        name="moe_dispatch",
    )(tabs["seg_start"], tabs["cnt"], tabs["pad_start"], tabs["pad_len"], tabs["n_used"], h, top_e)


def _moe_kernel(layer, be_ref, nu_ref, end_ref, ne_ref, x_ref, b1_ref, b2_ref, w1_hbm, w2_hbm, y_ref,
                w1s, w2s, w1b, w2b, sem):
    i = pl.program_id(0)
    nu = nu_ref[0]

    def weight_copies(e):
        return (pltpu.make_async_copy(w1_hbm.at[layer, e], w1s, sem.at[0]),
                pltpu.make_async_copy(w2_hbm.at[layer, e], w2s, sem.at[1]))

    @pl.when(i < nu)
    def _():
        e = be_ref[i]

        @pl.when(i == 0)
        def _():
            for cp in weight_copies(e):
                cp.start()

        @pl.when(jnp.logical_or(i == 0, e != be_ref[jnp.maximum(i - 1, 0)]))
        def _():
            for cp in weight_copies(e):
                cp.wait()
            w1b[...] = w1s[...].astype(BF16)
            w2b[...] = w2s[...].astype(BF16)
            nxt = ne_ref[e]

            @pl.when(nxt >= 0)
            def _():
                for cp in weight_copies(nxt):
                    cp.start()

        def run(n_rows):
            slab_rows = pl.ds(0, n_rows * ROW_SLABS)
            x = _load_row_slabs(x_ref.at[slab_rows], n_rows).astype(BF16)
            h1 = _dot(x, w1b[...]) + b1_ref[0, 0]
            gate = jnp.minimum(h1[:, :D_MODEL], SWIGLU_LIMIT)
            up = jnp.clip(h1[:, D_MODEL:], -SWIGLU_LIMIT, SWIGLU_LIMIT)
            act = (up + 1.0) * gate * _sigmoid(SWIGLU_ALPHA * gate)
            y = _dot(act.astype(BF16), w2b[...]) + b2_ref[0, 0]
            _store_row_slabs(y_ref.at[slab_rows], y, n_rows)

        n_valid = jnp.minimum(end_ref[e] - i * MOE_BM, MOE_BM)
        n_parts = lax.div(n_valid + MOE_PART - 1, MOE_PART)
        for parts in range(1, MOE_BM // MOE_PART + 1):
            @pl.when(n_parts == parts)
            def _(rows=parts * MOE_PART):
                run(rows)
                if rows < MOE_BM:
                    y_ref[pl.ds(rows * ROW_SLABS, (MOE_BM - rows) * ROW_SLABS), :] = jnp.zeros(
                        ((MOE_BM - rows) * ROW_SLABS, LANES), F32)

    @pl.when(i >= nu)
    def _():
        y_ref[...] = jnp.zeros_like(y_ref)


def _moe(xs, tabs, layer, w1, b1, w2, b2, n_tok):
    _, n_blocks = _moe_rows(n_tok)
    d2 = 2 * D_MODEL
    blk = MOE_BM * ROW_SLABS
    grid_spec = pltpu.PrefetchScalarGridSpec(
        num_scalar_prefetch=4,
        grid=(n_blocks,),
        in_specs=[
            pl.BlockSpec((blk, LANES), lambda i, be, nu, *_: (jnp.minimum(i, nu[0] - 1), 0)),
            pl.BlockSpec((1, 1, 1, d2), lambda i, be, *_: (layer, be[i], 0, 0)),
            pl.BlockSpec((1, 1, 1, D_MODEL), lambda i, be, *_: (layer, be[i], 0, 0)),
            pl.BlockSpec(memory_space=pl.ANY),
            pl.BlockSpec(memory_space=pl.ANY),
        ],
        out_specs=pl.BlockSpec((blk, LANES), lambda i, *_: (i, 0)),
        scratch_shapes=[
            pltpu.VMEM((D_MODEL, d2), F32),
            pltpu.VMEM((D_MODEL, D_MODEL), F32),
            pltpu.VMEM((D_MODEL, d2), BF16),
            pltpu.VMEM((D_MODEL, D_MODEL), BF16),
            pltpu.SemaphoreType.DMA((2,)),
        ],
    )
    return pl.pallas_call(
        functools.partial(_moe_kernel, layer),
        grid_spec=grid_spec,
        out_shape=jax.ShapeDtypeStruct((n_blocks * blk, LANES), F32),
        compiler_params=_params(("arbitrary",), VMEM_LIMIT),
        name="moe_experts",
    )(tabs["block_e"], tabs["n_used"], tabs["row_end"], tabs["next_e"], xs,
      b1.reshape(DEPTH, N_EXPERTS, 1, d2), b2.reshape(DEPTH, N_EXPERTS, 1, D_MODEL), w1, w2)


def _combine_kernel(n_next, n_tiles, ss_ref, cn_ref, lpos_ref, tg_ref, x_ref, mod_ref, g_ref, ys_hbm, *rest):
    final = n_next == 0
    next_in, (o_ref, *next_out), (cbuf, sem) = rest[:n_next], rest[n_next:len(rest) - 2], rest[len(rest) - 2:]
    t = pl.program_id(0) * n_tiles + pl.program_id(1)
    n_total = pl.num_programs(0) * n_tiles
    slot = lax.rem(t, MOE_RING)
    ahead = MOE_RING - 1

    def segment_copy(tile, sl, e, local, n):
        return pltpu.make_async_copy(ys_hbm.at[_slab_rows(ss_ref[tile * N_EXPERTS + e], n)],
                                     cbuf.at[sl, _slab_rows(local, n)], sem.at[sl])

    @pl.when(t == 0)
    def _():
        for first in range(ahead):
            def segment(e, local, first=first):
                n = cn_ref[first * N_EXPERTS + e]

                @pl.when(n > 0)
                def _():
                    segment_copy(first, first, e, local, n).start()
                return local + n
            lax.fori_loop(0, N_EXPERTS, segment, 0)

    @pl.when(t + ahead < n_total)
    def _():
        tile = t + ahead
        sl = lax.rem(tile, MOE_RING)
        local = 0
        for e in range(N_EXPERTS):
            n = cn_ref[tile * N_EXPERTS + e]

            @pl.when(n > 0)
            def _(e=e, n=n, local=local):
                segment_copy(tile, sl, e, local, n).start(priority=e % 2)
            local = local + n

    pltpu.make_async_copy(ys_hbm.at[_slab_rows(0, TILE_ROWS)], cbuf.at[slot], sem.at[slot]).wait()

    lpos = lpos_ref[...]
    tg = tg_ref[...]
    riota = lax.broadcasted_iota(I32, (TILE_ROWS, TM), 0)
    hit = riota == lpos[0:1]
    gsel = jnp.where(hit, tg[0:1], 0.0)
    for k in range(1, TOP_K):
        hit_k = riota == lpos[k:k + 1]
        gsel = gsel + jnp.where(hit_k, tg[k:k + 1], 0.0)
        hit = jnp.logical_or(hit, hit_k)
    row_gate = jnp.sum(gsel, axis=1, keepdims=True)
    yg = (_load_row_slabs(cbuf.at[slot], TILE_ROWS) * row_gate).astype(BF16)
    f = _dot_tn(jnp.where(hit, 1.0, 0.0).astype(BF16), yg)

    mod = mod_ref[0, 0]
    xn = x_ref[0] + mod[5:6] * f
    if final:
        ms = jnp.mean(xn * xn, axis=-1, keepdims=True)
        xn = xn * lax.rsqrt(ms + EPS) * g_ref[...]
    o_ref[0] = xn
    if not final:
        _qkv_body(xn, *next_in, *next_out)


def _combine(ys, tabs, lpos, top_g, xres, n_tiles, mod, mod_seg, g, next_qkv):
    final = next_qkv is None
    in_specs = [
        pl.BlockSpec((SUBLANES, TM), lambda b, j, *_: (0, b * n_tiles + j)),
        pl.BlockSpec((SUBLANES, TM), lambda b, j, *_: (0, b * n_tiles + j)),
        pl.BlockSpec((1, TM, D_MODEL), lambda b, j, *_: (b, j, 0)),
        pl.BlockSpec((1, 1, SUBLANES, D_MODEL), lambda b, j, *_: (b, mod_seg(j), 0, 0)),
        pl.BlockSpec((1, D_MODEL), lambda b, j, *_: (0, 0)),
        pl.BlockSpec(memory_space=pl.ANY),
    ]
    out_specs = [pl.BlockSpec((1, TM, D_MODEL), lambda b, j, *_: (b, j, 0))]
    out_shape = [jax.ShapeDtypeStruct((BATCH, n_tiles * TM, D_MODEL), F32)]
    extra = ()
    if not final:
        qkv_in, qkv_out, qkv_shape = _qkv_specs()
        in_specs += qkv_in
        out_specs += qkv_out
        out_shape += qkv_shape
        extra = tuple(next_qkv)
    grid_spec = pltpu.PrefetchScalarGridSpec(
        num_scalar_prefetch=2,
        grid=(BATCH, n_tiles),
        in_specs=in_specs,
        out_specs=out_specs,
        scratch_shapes=[
            pltpu.VMEM((MOE_RING, TILE_ROWS * ROW_SLABS, LANES), F32),
            pltpu.SemaphoreType.DMA((MOE_RING,)),
        ],
    )
    out = pl.pallas_call(
        functools.partial(_combine_kernel, len(extra), n_tiles),
        grid_spec=grid_spec,
        out_shape=out_shape,
        compiler_params=_params(("arbitrary", "arbitrary"), VMEM_LIMIT),
        name="moe_combine",
    )(tabs["seg_start"], tabs["cnt"], lpos, top_g, xres, mod, g, ys, *extra)
    return out[0] if final else out


def _moe_ffn(h, top_e, top_g, cnt, layer, xres, n_tiles, mod, mod_seg, w1, b1, w2, b2, g, next_qkv):
    n_tok = BATCH * n_tiles * TM
    tabs = _route_tables(cnt[:, :, 0], n_tok)
    xs, lpos = _dispatch(h, top_e, tabs, n_tok)
    ys = _moe(xs, tabs, layer, w1, b1, w2, b2, n_tok)
    return _combine(ys, tabs, lpos, top_g, xres, n_tiles, mod, mod_seg, g, next_qkv)


def _qkv_body(x, mod_ref, g_ref, w_ref, qn_ref, kn_ref, cos_ref, sin_ref, q_ref, k_ref, v_ref):
    mod = mod_ref[0, 0]
    h = _norm_mod(x, g_ref[...], mod[0:1], mod[1:2]).astype(BF16)
    qkv = _dot(h, w_ref[...])
    cos = cos_ref[...]
    sin = sin_ref[...]

    def head(xh, gn):
        ms = jnp.mean(xh * xh, axis=-1, keepdims=True)
        y = xh * lax.rsqrt(ms + EPS) * gn
        return y * cos + pltpu.roll(y, C_HEAD_DIM // 2, 1) * sin

    qn = qn_ref[...]
    kn = kn_ref[...]
    for hq in range(C_HEADS):
        sl = slice(hq * C_HEAD_DIM, (hq + 1) * C_HEAD_DIM)
        q_ref[0, :, sl] = head(qkv[:, sl], qn).astype(BF16)
    for hk in range(C_KV_HEADS):
        src = slice((C_HEADS + hk) * C_HEAD_DIM, (C_HEADS + hk + 1) * C_HEAD_DIM)
        k_ref[0, :, hk * C_HEAD_DIM:(hk + 1) * C_HEAD_DIM] = head(qkv[:, src], kn).astype(BF16)
    ones = jnp.ones((TM, C_HEAD_DIM), BF16)
    for hk in range(C_KV_HEADS):
        src = slice((C_HEADS + C_KV_HEADS + hk) * C_HEAD_DIM, (C_HEADS + C_KV_HEADS + hk + 1) * C_HEAD_DIM)
        v_ref[0, :, 2 * hk * C_HEAD_DIM:(2 * hk + 1) * C_HEAD_DIM] = qkv[:, src].astype(BF16)
        v_ref[0, :, (2 * hk + 1) * C_HEAD_DIM:(2 * hk + 2) * C_HEAD_DIM] = ones


def _qkv_specs():
    kvw = C_KV_HEADS * C_HEAD_DIM
    in_specs = [
        pl.BlockSpec((1, 1, SUBLANES, D_MODEL), lambda b, j, *_: (b, jnp.minimum(j, 1), 0, 0)),
        pl.BlockSpec((1, D_MODEL), lambda b, j, *_: (0, 0)),
        pl.BlockSpec((D_MODEL, C_QKV), lambda b, j, *_: (0, 0)),
        pl.BlockSpec((1, C_HEAD_DIM), lambda b, j, *_: (0, 0)),
        pl.BlockSpec((1, C_HEAD_DIM), lambda b, j, *_: (0, 0)),
        pl.BlockSpec((TM, C_HEAD_DIM), lambda b, j, *_: (j, 0)),
        pl.BlockSpec((TM, C_HEAD_DIM), lambda b, j, *_: (j, 0)),
    ]
    out_specs = [
        pl.BlockSpec((1, TM, D_MODEL), lambda b, j, *_: (b, jnp.maximum(j - 1, 0), 0)),
        pl.BlockSpec((1, TM, kvw), lambda b, j, *_: (b, j, 0)),
        pl.BlockSpec((1, TM, 2 * kvw), lambda b, j, *_: (b, j, 0)),
    ]
    out_shape = [
        jax.ShapeDtypeStruct((BATCH, SEQ, D_MODEL), BF16),
        jax.ShapeDtypeStruct((BATCH, L_ALL, kvw), BF16),
        jax.ShapeDtypeStruct((BATCH, L_ALL, 2 * kvw), BF16),
    ]
    return in_specs, out_specs, out_shape


def _rope_permute(t):
    rows, cols = t.shape
    t = t.reshape(rows, cols // C_HEAD_DIM, 2, 2, C_HEAD_DIM // 4)
    return t.transpose(0, 1, 3, 2, 4).reshape(rows, cols)


def _rope_tables():
    rows = SEQ // GRID_W
    row = jnp.repeat(jnp.arange(rows), GRID_W).astype(F32)
    col = jnp.tile(jnp.arange(GRID_W), rows).astype(F32)
    half = C_HEAD_DIM // 2
    inv_freq = ROPE_THETA ** (-jnp.arange(0, half, 2, dtype=F32) / half)
    ar = row[:, None] * inv_freq
    ac = col[:, None] * inv_freq
    cos = jnp.concatenate([jnp.cos(ar), jnp.cos(ac), jnp.cos(ar), jnp.cos(ac)], axis=-1)
    sin = jnp.concatenate([-jnp.sin(ar), -jnp.sin(ac), jnp.sin(ar), jnp.sin(ac)], axis=-1)
    cos = jnp.concatenate([jnp.ones((CTX_LEN, C_HEAD_DIM), F32), cos], axis=0)
    sin = jnp.concatenate([jnp.zeros((CTX_LEN, C_HEAD_DIM), F32), sin], axis=0)
    return cos, sin


def _attn_kernel(q_ref, k_ref, v_ref, o_ref):
    c = (C_HEAD_DIM ** -0.5) * LOG2_E
    def head_cols(h):
        return slice(h * C_HEAD_DIM, (h + 1) * C_HEAD_DIM)

    def scores(h):
        return _dot_nt(q_ref[0, :, head_cols(h)], k_ref[0, :, head_cols(h // C_GROUP)])

    s_next = scores(0)
    for h in range(C_HEADS):
        s = s_next
        if h + 1 < C_HEADS:
            s_next = scores(h + 1)
        g = h // C_GROUP
        m = jnp.max(s, axis=-1, keepdims=True)
        p = jnp.exp2((s - m) * c).astype(BF16)
        oe = _dot(p, v_ref[0, :, 2 * g * C_HEAD_DIM:(2 * g + 2) * C_HEAD_DIM])
        o_ref[0, :, head_cols(h)] = (oe[:, :C_HEAD_DIM] / oe[:, C_HEAD_DIM:]).astype(BF16)


def _attention(q, k, v):
    kvw = C_KV_HEADS * C_HEAD_DIM
    return pl.pallas_call(
        _attn_kernel,
        grid=(BATCH, SEQ // ATT_TQ),
        in_specs=[
            pl.BlockSpec((1, ATT_TQ, D_MODEL), lambda b, j: (b, j, 0)),
            pl.BlockSpec((1, L_ALL, kvw), lambda b, j: (b, 0, 0)),
            pl.BlockSpec((1, L_ALL, 2 * kvw), lambda b, j: (b, 0, 0)),
        ],
        out_specs=pl.BlockSpec((1, ATT_TQ, D_MODEL), lambda b, j: (b, j, 0)),
        out_shape=jax.ShapeDtypeStruct((BATCH, SEQ, D_MODEL), BF16),
        compiler_params=_params(("arbitrary", "arbitrary"), VMEM_LIMIT),
        name="attention",
    )(q, k, v)


def _mod_table(ada_layer):
    m = ada_layer.reshape(ADA_ROWS, N_MOD, D_MODEL)
    m_lat = m[:BATCH]
    m_ctx = jnp.broadcast_to(m[BATCH], (BATCH, N_MOD, D_MODEL))
    t = jnp.stack([m_ctx, m_lat], axis=1)
    return jnp.pad(t, ((0, 0), (0, 0), (0, SUBLANES - N_MOD), (0, 0)))


def kernel(x, c, ctx, c_ctx, ada_w, ada_b, norm_mix, norm_ffn, ab_w_in, ab_gate_w, ab_gate_b, ab_out_norm,
           ab_conv_w, ab_conv_b, ab_w_out, attn_w_qkv, attn_q_norm, attn_k_norm, attn_w_o, router_w,
           router_b, moe_w1, moe_b1, moe_w2, moe_b2, final_norm):
    assert x.shape == (BATCH, SEQ, D_MODEL) and ctx.shape == (BATCH, CTX_LEN, D_MODEL), (x.shape, ctx.shape)
    assert moe_w1.shape == (DEPTH, N_EXPERTS, D_MODEL, 2 * D_MODEL), moe_w1.shape
    cond = jnp.zeros((ADA_ROWS, D_MODEL), F32).at[:BATCH].set(c).at[BATCH].set(c_ctx)
    ada = _ada(cond, ada_w, ada_b)
    mod0 = _mod_table(ada[0])
    mod1 = _mod_table(ada[1])
    seg_all = lambda j: jnp.minimum(j, 1)
    seg_lat = lambda j: 1
    n_tiles_all = L_ALL // TM
    n_tiles_lat = SEQ // TM
    fnorm = final_norm.reshape(1, D_MODEL)

    w_in = ab_w_in[0]
    lr0 = 2 * A_HEADS * A_DK + A_HEADS * A_DV
    lr1 = lr0 + 2 * A_GATE_RANK
    w_main = jnp.concatenate([w_in[:, :lr0], w_in[:, lr1:]], axis=1).astype(BF16)
    w_lr = jnp.pad(w_in[:, lr0:lr1], ((0, 0), (0, LANES - 2 * A_GATE_RANK))).astype(BF16)
    p, lr = _proj_in(ctx, x, mod0, norm_mix[0:1], w_main, w_lr)

    gw = ab_gate_w[0]
    gw_pad = jnp.zeros((2, LANES, A_HEADS * A_DK), F32)
    gw_pad = gw_pad.at[0, :A_GATE_RANK].set(gw[0]).at[1, A_GATE_RANK:2 * A_GATE_RANK].set(gw[1])
    cw = jnp.concatenate([ab_conv_w[0], ab_conv_b[0][None], jnp.zeros((SUBLANES - 4, B_WIDTH), F32)], axis=0)
    yg, yc = _gla_conv(p, lr, gw_pad.astype(BF16), ab_gate_b[0].reshape(2, 1, A_HEADS * A_DK),
                       ab_out_norm[0:1], cw)

    w_out = ab_w_out[0].astype(BF16)
    ngla = A_HEADS * A_DV
    xmid, h, top_e, top_g, cnt = _outproj(
        [yg, yc], [w_out[:ngla], w_out[ngla:]], [ctx, x], _ctx_latent_specs(), n_tiles_all, mod0, seg_all,
        norm_ffn[0:1], router_w[0].T, router_b[0].reshape(N_EXPERTS, 1))
    cos, sin = _rope_tables()
    qk_cols = (C_HEADS + C_KV_HEADS) * C_HEAD_DIM
    w_qkv = jnp.concatenate([_rope_permute(attn_w_qkv[0][:, :qk_cols]), attn_w_qkv[0][:, qk_cols:]],
                            axis=1).astype(BF16)
    next_qkv = (mod1, norm_mix[1:2], w_qkv, _rope_permute(attn_q_norm[0:1]), _rope_permute(attn_k_norm[0:1]),
                cos, sin)
    x1, q, k, v = _moe_ffn(h, top_e, top_g, cnt, 0, xmid, n_tiles_all, mod0, seg_all,
                           moe_w1, moe_b1, moe_w2, moe_b2, fnorm, next_qkv)

    o = _attention(q, k, v)
    xmid, h, top_e, top_g, cnt = _outproj(
        [o], [attn_w_o[0].astype(BF16)], [x1],
        [pl.BlockSpec((1, TM, D_MODEL), lambda b, j: (b, j + CTX_LEN // TM, 0))], n_tiles_lat, mod1, seg_lat,
        norm_ffn[1:2], router_w[1].T, router_b[1].reshape(N_EXPERTS, 1))
    return _moe_ffn(h, top_e, top_g, cnt, 1, xmid, n_tiles_lat, mod1, seg_lat,
                    moe_w1, moe_b1, moe_w2, moe_b2, fnorm, None)
```
